```python
import jax, jax.numpy as jnp
from jax import lax
import numpy as np

D_MODEL = 1024
BATCH = 8
SEQ = 2048
DEPTH = 1
DEC_BATCH = 128
DEC_SEQ = 1
PAST_LEN = 16384
PAGE_SIZE = 128

MIX_WIDTH = D_MODEL
N_HEADS = 8
QK_NOPE = 64
QK_ROPE = 32
V_DIM = 64
MLA_WIDTH = N_HEADS * V_DIM
Q_RANK = D_MODEL // 4
KV_RANK = D_MODEL // 8
CHUNK = 128
N_CG = 8
CG_WIDTH = MIX_WIDTH - MLA_WIDTH
CG_DIM = CG_WIDTH // N_CG
D_FF = 2816
CONV_W = 3
PLE_DIM = 256
ROPE_THETA = 10000.0
EPS = 1e-6
Q_BLOCK = 128
SCALE = (QK_NOPE + QK_ROPE) ** -0.5
OFF_KV = Q_RANK
OFF_KR = Q_RANK + KV_RANK
OFF_U = OFF_KR + QK_ROPE
OFF_V = OFF_U + CG_WIDTH
IN_WIDTH = OFF_V + CG_WIDTH

kernel_name = 'hybrid_mla_chunkmlp_convffn_ple_step'


def rmsnorm(x, g):
    xf = x.astype(jnp.float32)
    y = xf * lax.rsqrt(jnp.mean(xf * xf, axis=-1, keepdims=True) + EPS)
    return y.astype(x.dtype) * g


def rope(x, pos):
    d = x.shape[-1]
    half = d // 2
    inv = ROPE_THETA ** (-jnp.arange(half, dtype=jnp.float32) * (2.0 / d))
    ang = pos.astype(jnp.float32)[:, None] * inv[None, :]
    cos = jnp.cos(ang)[None, :, None, :]
    sin = jnp.sin(ang)[None, :, None, :]
    x1 = x[..., :half].astype(jnp.float32)
    x2 = x[..., half:].astype(jnp.float32)
    return jnp.concatenate([x1 * cos - x2 * sin, x1 * sin + x2 * cos], axis=-1).astype(x.dtype)


def keys_nope(c, w_uk, k_norm_w):
    return rmsnorm(jnp.einsum('bkr,rhd->bkhd', c, w_uk), k_norm_w)


def attend(q_nope, q_rope, k_nope, k_rope, c, w_uv, q_pos, k_pos):
    s = (jnp.einsum('bqhd,bkhd->bhqk', q_nope, k_nope)
         + jnp.einsum('bqhd,bkd->bhqk', q_rope, k_rope)).astype(jnp.float32) * SCALE
    s = jnp.where(k_pos[None, :] <= q_pos[:, None], s, -jnp.inf)
    p = jax.nn.softmax(s, axis=-1).astype(c.dtype)
    o_lat = jnp.einsum('bhqk,bkr->bqhr', p, c)
    return jnp.einsum('bqhr,rhd->bqhd', o_lat, w_uv)


def attend_prompt(q_nope, q_rope, c_kv, k_rope, w_uk, k_norm_w, w_uv):
    B, T = q_nope.shape[0], q_nope.shape[1]
    k_n = keys_nope(c_kv, w_uk, k_norm_w)
    pos = jnp.arange(T, dtype=jnp.int32)
    nb = T // Q_BLOCK
    qn_b = q_nope.reshape(B, nb, Q_BLOCK, N_HEADS, QK_NOPE).transpose(1, 0, 2, 3, 4)
    qr_b = q_rope.reshape(B, nb, Q_BLOCK, N_HEADS, QK_ROPE).transpose(1, 0, 2, 3, 4)

    def blk(args):
        qn, qr, qp = args
        return attend(qn, qr, k_n, k_rope, c_kv, w_uv, qp, pos)

    out = lax.map(blk, (qn_b, qr_b, pos.reshape(nb, Q_BLOCK)))
    return out.transpose(1, 0, 2, 3, 4).reshape(B, T, N_HEADS, V_DIM)


def make_attend_sample(cache_c, cache_r, page_table):
    past_len = page_table.shape[1] * PAGE_SIZE

    def attend_sample(q_nope, q_rope, c_kv, k_rope, w_uk, k_norm_w, w_uv):
        T = q_nope.shape[1]
        k_pos = jnp.arange(past_len + T, dtype=jnp.int32)
        q_pos = past_len + jnp.arange(T, dtype=jnp.int32)

        def one(args):
            qn, qr, cn, rn, pages = args
            past_c = cache_c[pages].reshape(past_len, KV_RANK)
            past_r = cache_r[pages].reshape(past_len, QK_ROPE)
            c_all = jnp.concatenate([past_c.astype(cn.dtype), cn], axis=0)[None]
            r_all = jnp.concatenate([past_r.astype(rn.dtype), rn], axis=0)[None]
            k_n = keys_nope(c_all, w_uk, k_norm_w)
            return attend(qn[None], qr[None], k_n, r_all, c_all, w_uv, q_pos, k_pos)[0]

        return lax.map(one, (q_nope, q_rope, c_kv, k_rope, page_table))

    return attend_sample


def chunk_mix(u, v, w_s, b_s):
    B, T, _ = v.shape
    n_ch = -(-T // CHUNK)
    pad = n_ch * CHUNK - T
    vp = jnp.pad(v, ((0, 0), (0, pad), (0, 0))).reshape(B, n_ch, CHUNK, N_CG, CG_DIM)
    w = w_s * jnp.tril(jnp.ones((CHUNK, CHUNK), w_s.dtype))[None]
    mixed = jnp.einsum('gts,bcsgd->bctgd', w, vp) + b_s.T[None, None, :, :, None]
    mixed = mixed.reshape(B, n_ch * CHUNK, CG_WIDTH)[:, :T]
    return u * mixed


def conv_ffn(hn, hist, w_ff_in, conv_w, conv_b, w_ff_out):
    a = hn @ w_ff_in
    T = a.shape[1]
    ap = jnp.concatenate([hist.astype(a.dtype), a], axis=1)
    c = conv_b
    for k in range(CONV_W):
        c = c + conv_w[k] * ap[:, k:k + T]
    g, up = c[..., :D_FF], c[..., D_FF:]
    return (jax.nn.silu(g) * up) @ w_ff_out, ap[:, ap.shape[1] - (CONV_W - 1):]


def layer(x, p_emb, pos, attend_fn, conv_hist, lw):
    B, T, _ = x.shape
    hn = rmsnorm(x, lw['attn_norm_w'])
    z = hn @ lw['w_in']
    q_lat = z[..., :OFF_KV]
    c_kv = rmsnorm(z[..., OFF_KV:OFF_KR], lw['kv_norm_w'])
    k_r = z[..., OFF_KR:OFF_U]
    u = jax.nn.gelu(z[..., OFF_U:OFF_V])
    v = jax.nn.gelu(z[..., OFF_V:])
    q = (rmsnorm(q_lat, lw['q_norm_w']) @ lw['w_uq']).reshape(B, T, N_HEADS, QK_NOPE + QK_ROPE)
    q_nope = rmsnorm(q[..., :QK_NOPE], lw['q_nope_norm_w'])
    q_rope = rope(rmsnorm(q[..., QK_NOPE:], lw['q_rope_norm_w']), pos)
    k_rope = rope(rmsnorm(k_r, lw['k_rope_norm_w'])[:, :, None, :], pos)[:, :, 0, :]
    attn = attend_fn(q_nope, q_rope, c_kv, k_rope, lw['w_uk'], lw['k_nope_norm_w'], lw['w_uv'])
    sg = chunk_mix(u, v, lw['w_s'], lw['b_s'])
    h = x + jnp.concatenate([attn.reshape(B, T, MLA_WIDTH), sg], axis=-1) @ lw['w_o']
    f, new_hist = conv_ffn(rmsnorm(h, lw['ffn_norm_w']), conv_hist, lw['w_ff_in'],
                           lw['conv_w'], lw['conv_b'], lw['w_ff_out'])
    h = h + f
    gate = jax.nn.sigmoid(rmsnorm(h, lw['ple_norm_w']) @ lw['w_ple_gate'])
    e = rmsnorm(p_emb @ lw['w_ple_proj'], lw['ple_post_norm_w'])
    h = h + gate * e
    n_keep = (T - 1) % CHUNK + 1
    return h, c_kv, k_rope, v[:, T - n_keep:], new_hist


def setup_inputs(seed: int = 0) -> dict:
    key = jax.random.key(seed)
    ks = jax.random.split(key, 40)
    f32 = jnp.float32

    def nrm(k, shape, scale):
        return jax.random.normal(k, shape, f32) * scale

    def gain(k, shape):
        return 1.0 + 0.05 * jax.random.normal(k, shape, f32)

    n_pages = PAST_LEN // PAGE_SIZE
    n_used = DEC_BATCH * n_pages
    n_pool = (n_used * 5) // 4
    page_table = jax.random.permutation(ks[5], n_pool)[:n_used].reshape(DEC_BATCH, n_pages).astype(jnp.int32)
    L = DEPTH
    return {
        'x_prompt': nrm(ks[0], (BATCH, SEQ, D_MODEL), 1.0),
        'x_sample': nrm(ks[1], (DEC_BATCH, DEC_SEQ, D_MODEL), 1.0),
        'cache_ckv': nrm(ks[2], (L, n_pool, PAGE_SIZE, KV_RANK), 1.0),
        'cache_krope': nrm(ks[3], (L, n_pool, PAGE_SIZE, QK_ROPE), 1.0),
        'state_conv': nrm(ks[4], (L, DEC_BATCH, CONV_W - 1, 2 * D_FF), 1.0),
        'page_table': page_table,
        'p_prompt': nrm(ks[6], (L, BATCH, SEQ, PLE_DIM), 1.0),
        'p_sample': nrm(ks[7], (L, DEC_BATCH, DEC_SEQ, PLE_DIM), 1.0),
        'attn_norm_w': gain(ks[8], (L, D_MODEL)),
        'w_in': nrm(ks[9], (L, D_MODEL, IN_WIDTH), D_MODEL ** -0.5),
        'q_norm_w': gain(ks[10], (L, Q_RANK)),
        'w_uq': nrm(ks[11], (L, Q_RANK, N_HEADS * (QK_NOPE + QK_ROPE)), Q_RANK ** -0.5),
        'q_nope_norm_w': gain(ks[12], (L, QK_NOPE)),
        'q_rope_norm_w': gain(ks[13], (L, QK_ROPE)),
        'kv_norm_w': gain(ks[14], (L, KV_RANK)),
        'k_rope_norm_w': gain(ks[15], (L, QK_ROPE)),
        'w_uk': nrm(ks[16], (L, KV_RANK, N_HEADS, QK_NOPE), KV_RANK ** -0.5),
        'k_nope_norm_w': gain(ks[17], (L, QK_NOPE)),
        'w_uv': nrm(ks[18], (L, KV_RANK, N_HEADS, V_DIM), KV_RANK ** -0.5),
        'w_s': nrm(ks[19], (L, N_CG, CHUNK, CHUNK), 0.5 * CHUNK ** -0.5),
        'b_s': 1.0 + 0.1 * jax.random.normal(ks[20], (L, N_CG, CHUNK), f32),
        'w_o': nrm(ks[21], (L, MIX_WIDTH, D_MODEL), MIX_WIDTH ** -0.5),
        'ffn_norm_w': gain(ks[22], (L, D_MODEL)),
        'w_ff_in': nrm(ks[23], (L, D_MODEL, 2 * D_FF), D_MODEL ** -0.5),
        'conv_w': nrm(ks[24], (L, CONV_W, 2 * D_FF), CONV_W ** -0.5),
        'conv_b': nrm(ks[25], (L, 2 * D_FF), 0.01),
        'w_ff_out': nrm(ks[26], (L, D_FF, D_MODEL), D_FF ** -0.5),
        'ple_norm_w': gain(ks[27], (L, D_MODEL)),
        'w_ple_gate': nrm(ks[28], (L, D_MODEL, D_MODEL), D_MODEL ** -0.5),
        'w_ple_proj': nrm(ks[29], (L, PLE_DIM, D_MODEL), PLE_DIM ** -0.5),
        'ple_post_norm_w': gain(ks[30], (L, D_MODEL)),
    }


def reference(x_prompt, x_sample, cache_ckv, cache_krope, state_conv, page_table, p_prompt, p_sample,
              attn_norm_w, w_in, q_norm_w, w_uq, q_nope_norm_w, q_rope_norm_w, kv_norm_w, k_rope_norm_w,
              w_uk, k_nope_norm_w, w_uv, w_s, b_s, w_o, ffn_norm_w, w_ff_in, conv_w, conv_b, w_ff_out,
              ple_norm_w, w_ple_gate, w_ple_proj, ple_post_norm_w):
    past_len = page_table.shape[1] * PAGE_SIZE
    T_p = x_prompt.shape[1]
    T_s = x_sample.shape[1]
    pos_p = jnp.arange(T_p, dtype=jnp.int32)
    pos_s = past_len + jnp.arange(T_s, dtype=jnp.int32)
    hp, hs = x_prompt, x_sample
    ckv_p, kr_p, ckv_s, kr_s, v_p, v_s, cv_p, cv_s = [], [], [], [], [], [], [], []
    for i in range(DEPTH):
        lw = {
            'attn_norm_w': attn_norm_w[i], 'w_in': w_in[i], 'q_norm_w': q_norm_w[i], 'w_uq': w_uq[i],
            'q_nope_norm_w': q_nope_norm_w[i], 'q_rope_norm_w': q_rope_norm_w[i],
            'kv_norm_w': kv_norm_w[i], 'k_rope_norm_w': k_rope_norm_w[i], 'w_uk': w_uk[i],
            'k_nope_norm_w': k_nope_norm_w[i], 'w_uv': w_uv[i], 'w_s': w_s[i], 'b_s': b_s[i],
            'w_o': w_o[i], 'ffn_norm_w': ffn_norm_w[i], 'w_ff_in': w_ff_in[i], 'conv_w': conv_w[i],
            'conv_b': conv_b[i], 'w_ff_out': w_ff_out[i], 'ple_norm_w': ple_norm_w[i],
            'w_ple_gate': w_ple_gate[i], 'w_ple_proj': w_ple_proj[i], 'ple_post_norm_w': ple_post_norm_w[i],
        }
        zero_hist = jnp.zeros((hp.shape[0], CONV_W - 1, 2 * D_FF), hp.dtype)
        hp, c1, r1, v1, h1 = layer(hp, p_prompt[i], pos_p, attend_prompt, zero_hist, lw)
        attend_s = make_attend_sample(cache_ckv[i], cache_krope[i], page_table)
        hs, c2, r2, v2, h2 = layer(hs, p_sample[i], pos_s, attend_s, state_conv[i], lw)
        ckv_p.append(c1); kr_p.append(r1); v_p.append(v1); cv_p.append(h1)
        ckv_s.append(c2); kr_s.append(r2); v_s.append(v2); cv_s.append(h2)
    new_ckv_prompt = jnp.stack(ckv_p, 0)
    new_krope_prompt = jnp.stack(kr_p, 0)
    new_ckv_sample = jnp.stack(ckv_s, 0)
    new_krope_sample = jnp.stack(kr_s, 0)
    new_v_prompt = jnp.stack(v_p, 0)
    new_v_sample = jnp.stack(v_s, 0)
    new_conv_prompt = jnp.stack(cv_p, 0)
    new_conv_sample = jnp.stack(cv_s, 0)
    return (hp, hs, new_ckv_prompt, new_krope_prompt, new_ckv_sample, new_krope_sample,
            new_v_prompt, new_v_sample, new_conv_prompt, new_conv_sample)
```

```python
import functools

import jax
import jax.numpy as jnp
from jax import lax
from jax.experimental import pallas as pl
from jax.experimental.pallas import tpu as pltpu

F32 = jnp.float32
BF16 = jnp.bfloat16

D_MODEL = 1024
N_HEADS = 8
QK_NOPE = 64
QK_ROPE = 32
ROPE_HALF = QK_ROPE // 2
V_DIM = 64
Q_RANK = 256
KV_RANK = 128
CHUNK = 128
N_CG = 8
CG_WIDTH = 512
CG_DIM = 64
D_FF = 2816
CONV_W = 3
PLE_DIM = 256
PAGE_SIZE = 128
ROPE_THETA = 10000.0
EPS = 1e-6
SCALE = (QK_NOPE + QK_ROPE) ** -0.5
OFF_KV = Q_RANK
OFF_KR = OFF_KV + KV_RANK
OFF_U = OFF_KR + QK_ROPE
OFF_V = OFF_U + CG_WIDTH
IN_WIDTH = OFF_V + CG_WIDTH
HEAD_PAD = 128
QK_WIDTH = N_HEADS * HEAD_PAD
Z_WIDTH = Q_RANK + KV_RANK + 2 * CG_WIDTH

LANES = 128
SUBLANES = 8
VMEM_LIMIT_BYTES = 56 * 1024 * 1024

TOK_TILE = 512
ATT_BLOCK = 256
FF_TILE = 256
KV_TILE = 2048
PAGES_PER_TILE = KV_TILE // PAGE_SIZE

NT_DIMS = (((1,), (1,)), ((), ()))


def _params(semantics):
    return pltpu.CompilerParams(dimension_semantics=semantics,
                                vmem_limit_bytes=VMEM_LIMIT_BYTES)


def _rms(x):
    return x * lax.rsqrt(jnp.mean(x * x, axis=-1, keepdims=True) + EPS)


def _dot(a, b):
    return jnp.dot(a, b, preferred_element_type=F32)


def _dot_nt(a, b):
    return lax.dot_general(a, b, NT_DIMS, preferred_element_type=F32)


def _full(shape):
    zeros = (0,) * len(shape)
    return pl.BlockSpec(shape, lambda *_: zeros)


def _in_proj_kernel(x_ref, ca_ref, sb_ref, cost_ref, sint_ref,
                    g_attn_ref, w_in_ref, g_q_ref, w_qa_ref, w_qb_ref, gq_ref, ga_ref, gb_ref,
                    g_kv_ref, w_ukt_ref, g_kn_ref, w_krt_ref, g_kr_ref,
                    q_ref, kt_ref, ckv_ref, cb_ref, krt_ref, u_ref, v_ref):
    tm = x_ref.shape[0]
    hb = (_rms(x_ref[...]) * g_attn_ref[...]).astype(BF16)
    z = _dot(hb, w_in_ref[...])
    u_ref[...] = jax.nn.gelu(z[:, OFF_KR:OFF_KR + CG_WIDTH])
    v_ref[...] = jax.nn.gelu(z[:, OFF_KR + CG_WIDTH:])
    ckv = _rms(z[:, OFF_KV:OFF_KR]) * g_kv_ref[...]
    ckv_ref[...] = ckv
    cb = ckv.astype(BF16)
    cb_ref[...] = cb

    qln = (_rms(z[:, :Q_RANK]) * g_q_ref[...]).astype(BF16)
    qa = _dot(qln, w_qa_ref[...])
    qb = _dot(qln, w_qb_ref[...])
    sq = qa * qa
    sq_hi = sq.astype(BF16)
    sq_lo = (sq - sq_hi.astype(F32)).astype(BF16)
    gq = gq_ref[...]
    pair = 2 * HEAD_PAD
    ms = jnp.concatenate(
        [_dot(sq_hi[:, i * pair:(i + 1) * pair], gq) + _dot(sq_lo[:, i * pair:(i + 1) * pair], gq)
         for i in range(N_HEADS // 2)], axis=1)
    ca = jnp.tile(ca_ref[...], (1, N_HEADS))
    sb = jnp.tile(sb_ref[...], (1, N_HEADS))
    q = lax.rsqrt(ms + EPS) * (qa * ga_ref[...] * ca + qb * gb_ref[...] * sb)
    q_ref[...] = q.astype(BF16)

    reps = tm // LANES
    knt = _dot_nt(w_ukt_ref[...], cb).reshape(N_HEADS, QK_NOPE, tm)
    kn = knt * lax.rsqrt(jnp.mean(knt * knt, axis=1, keepdims=True) + EPS)
    kn = kn * jnp.tile(g_kn_ref[...], (1, reps)).reshape(N_HEADS, QK_NOPE, tm)
    krt = _dot_nt(w_krt_ref[...], hb)
    krt = krt * lax.rsqrt(jnp.mean(krt * krt, axis=0, keepdims=True) + EPS)
    krt = krt * jnp.tile(g_kr_ref[...], (1, reps))
    x1, x2 = krt[:ROPE_HALF], krt[ROPE_HALF:]
    cos, sin = cost_ref[...], sint_ref[...]
    kr = jnp.concatenate([x1 * cos - x2 * sin, x1 * sin + x2 * cos], axis=0)
    krt_ref[...] = kr
    kr_b = kr.astype(BF16)
    pad = jnp.zeros((HEAD_PAD - QK_NOPE - QK_ROPE, tm), BF16)
    for h in range(N_HEADS):
        kt_ref[h] = jnp.concatenate([kn[h].astype(BF16), kr_b, pad], axis=0)


def _in_proj(x, tabs, w, tm):
    n = x.shape[0]
    ca, sb, cost, sint = tabs
    pos_tiles = ca.shape[0] // tm
    tok = lambda width: pl.BlockSpec((tm, width), lambda i: (i, 0))
    tok_t = lambda rows: pl.BlockSpec((rows, tm), lambda i: (0, i))
    pos = pl.BlockSpec((tm, HEAD_PAD), lambda i: (i % pos_tiles, 0))
    pos_t = pl.BlockSpec((ROPE_HALF, tm), lambda i: (0, i % pos_tiles))
    weights = [w['g_attn'], w['w_in'], w['g_q'], w['w_qa'], w['w_qb'], w['gq'], w['ga'], w['gb'],
               w['g_kv'], w['w_ukt'], w['g_kn'], w['w_krt'], w['g_kr']]
    return pl.pallas_call(
        _in_proj_kernel,
        grid=(n // tm,),
        in_specs=[tok(D_MODEL), pos, pos, pos_t, pos_t]
                 + [_full(a.shape) for a in weights],
        out_specs=[tok(QK_WIDTH),
                   pl.BlockSpec((N_HEADS, HEAD_PAD, tm), lambda i: (0, 0, i)),
                   tok(KV_RANK), tok(KV_RANK), tok_t(QK_ROPE), tok(CG_WIDTH), tok(CG_WIDTH)],
        out_shape=[jax.ShapeDtypeStruct((n, QK_WIDTH), BF16),
                   jax.ShapeDtypeStruct((N_HEADS, HEAD_PAD, n), BF16),
                   jax.ShapeDtypeStruct((n, KV_RANK), F32),
                   jax.ShapeDtypeStruct((n, KV_RANK), BF16),
                   jax.ShapeDtypeStruct((QK_ROPE, n), F32),
                   jax.ShapeDtypeStruct((n, CG_WIDTH), F32),
                   jax.ShapeDtypeStruct((n, CG_WIDTH), F32)],
        compiler_params=_params(("parallel",)),
        name="in_proj",
    )(x, ca, sb, cost, sint, *weights)


def _softmax_step(s, m, l, acc, vals):
    m_new = jnp.maximum(m, jnp.max(s, axis=-1, keepdims=True))
    alpha = jnp.exp(m - m_new)
    p = jnp.exp(s - m_new)
    l = alpha * l + jnp.sum(p, axis=-1, keepdims=True)
    acc = alpha * acc + _dot(p.astype(BF16), vals)
    return m_new, l, acc


def _prompt_attn_kernel(q_ref, kt_ref, v_ref, o_ref):
    qi = pl.program_id(1)
    blk = ATT_BLOCK
    row = lax.broadcasted_iota(jnp.int32, (blk, blk), 0)
    col = lax.broadcasted_iota(jnp.int32, (blk, blk), 1)
    diag_off = pl.multiple_of(qi * blk, blk)
    for h in range(N_HEADS):
        qh = q_ref[:, h * HEAD_PAD:(h + 1) * HEAD_PAD]

        def scores(off):
            return _dot(qh, kt_ref[h, :, pl.ds(off, blk)]) * SCALE

        def body(j, carry):
            off = pl.multiple_of(j * blk, blk)
            return _softmax_step(scores(off), *carry, v_ref[pl.ds(off, blk), :])

        init = (jnp.full((blk, 1), -jnp.inf, F32), jnp.zeros((blk, 1), F32),
                jnp.zeros((blk, KV_RANK), F32))
        carry = lax.fori_loop(0, qi, body, init)
        s = jnp.where(col <= row, scores(diag_off), -jnp.inf)
        _, l, acc = _softmax_step(s, *carry, v_ref[pl.ds(diag_off, blk), :])
        o_ref[:, h * HEAD_PAD:(h + 1) * HEAD_PAD] = (acc / l).astype(BF16)


def _prompt_attn(q, kt, cb, batch, seq):
    nq = seq // ATT_BLOCK
    return pl.pallas_call(
        _prompt_attn_kernel,
        grid=(batch, nq),
        in_specs=[pl.BlockSpec((ATT_BLOCK, QK_WIDTH), lambda b, i: (b * nq + i, 0)),
                  pl.BlockSpec((N_HEADS, HEAD_PAD, seq), lambda b, i: (0, 0, b)),
                  pl.BlockSpec((seq, KV_RANK), lambda b, i: (b, 0))],
        out_specs=pl.BlockSpec((ATT_BLOCK, N_HEADS * KV_RANK), lambda b, i: (b * nq + i, 0)),
        out_shape=jax.ShapeDtypeStruct((batch * seq, N_HEADS * KV_RANK), BF16),
        compiler_params=_params(("parallel", "parallel")),
        name="prompt_attn",
    )(q, kt, cb)


def _sample_attn_kernel(pt_ref, qn_ref, qr_ref, cs_ref, krs_ref, wuk_ref, wukb_ref, gkn_ref,
                        cache_c, cache_r, o_ref,
                        wext, cbuf, rbuf, sems, m_s, l_s, acc_s, *, n_tiles, n_pages):
    b = pl.program_id(0)
    kt = pl.program_id(1)
    step = b * n_tiles + kt
    slot = lax.rem(step, 2)
    n_steps = pl.num_programs(0) * n_tiles

    def page_copies(bb, kk, sl):
        copies = []
        for i in range(PAGES_PER_TILE):
            page = pt_ref[bb * n_pages + kk * PAGES_PER_TILE + i]
            rows = pl.ds(i * PAGE_SIZE, PAGE_SIZE)
            copies.append(pltpu.make_async_copy(cache_c.at[page], cbuf.at[sl, rows], sems.at[0, sl]))
            copies.append(pltpu.make_async_copy(cache_r.at[page], rbuf.at[sl, rows], sems.at[1, sl]))
        return copies

    @pl.when(step == 0)
    def _():
        for c in page_copies(0, 0, 0):
            c.start()

    @pl.when(step + 1 < n_steps)
    def _():
        nxt = step + 1
        for c in page_copies(nxt // n_tiles, lax.rem(nxt, n_tiles), 1 - slot):
            c.start()

    def process(c_blk, kr_blk, mask):
        tk = c_blk.shape[0]
        cb = c_blk.astype(BF16)
        knt = _dot_nt(wext[...], cb)
        kn3 = knt[:N_HEADS * QK_NOPE].reshape(N_HEADS, QK_NOPE, tk)
        r = lax.rsqrt(jnp.sum(kn3 * kn3, axis=1) * (1.0 / QK_NOPE) + EPS)
        s_nope = knt[N_HEADS * QK_NOPE:N_HEADS * QK_NOPE + N_HEADS] * r
        s_rope = _dot_nt(qr_ref[0].astype(BF16), kr_blk.astype(BF16))
        s = (s_nope + s_rope) * SCALE
        if mask is not None:
            s = jnp.where(mask, s, -jnp.inf)
        m_new, l_new, acc_new = _softmax_step(s, m_s[...], l_s[...], acc_s[...], cb)
        m_s[...] = m_new
        l_s[...] = l_new
        acc_s[...] = acc_new

    @pl.when(kt == 0)
    def _():
        nw = N_HEADS * QK_NOPE
        qg = jnp.tile(qn_ref[0], (1, N_HEADS)) * gkn_ref[...]
        head = lax.broadcasted_iota(jnp.int32, (N_HEADS, nw), 0)
        lane = lax.broadcasted_iota(jnp.int32, (N_HEADS, nw), 1)
        qbd = jnp.where((lane >= head * QK_NOPE) & (lane < (head + 1) * QK_NOPE), qg, 0.0)
        qa = jnp.dot(qbd, wuk_ref[...], precision=lax.Precision.HIGHEST,
                     preferred_element_type=F32)
        wext[:nw, :] = wukb_ref[...]
        wext[nw:, :] = jnp.concatenate(
            [qa, jnp.zeros((wext.shape[0] - nw - N_HEADS, KV_RANK), F32)], axis=0).astype(BF16)
        m_s[...] = jnp.full(m_s.shape, -jnp.inf, F32)
        l_s[...] = jnp.zeros(l_s.shape, F32)
        acc_s[...] = jnp.zeros(acc_s.shape, F32)
        own = lax.broadcasted_iota(jnp.int32, (N_HEADS, cs_ref.shape[0]), 1) == b
        process(cs_ref[...], krs_ref[...], own)

    for c in page_copies(b, kt, slot):
        c.wait()
    process(cbuf[slot], rbuf[slot], None)

    @pl.when(kt == n_tiles - 1)
    def _():
        o_ref[0] = acc_s[...] / l_s[...]


def _sample_attn(page_table, qn, qr, cs, krs, w, cache_c, cache_r):
    n_samp, n_pages = page_table.shape
    n_tiles = (n_pages * PAGE_SIZE) // KV_TILE
    wext_rows = N_HEADS * QK_NOPE + 2 * SUBLANES
    grid_spec = pltpu.PrefetchScalarGridSpec(
        num_scalar_prefetch=1,
        grid=(n_samp, n_tiles),
        in_specs=[pl.BlockSpec((1, N_HEADS, QK_NOPE), lambda b, k, pt: (b, 0, 0)),
                  pl.BlockSpec((1, N_HEADS, QK_ROPE), lambda b, k, pt: (b, 0, 0)),
                  pl.BlockSpec(cs.shape, lambda b, k, pt: (0, 0)),
                  pl.BlockSpec(krs.shape, lambda b, k, pt: (0, 0)),
                  pl.BlockSpec(w['w_ukt_f32'].shape, lambda b, k, pt: (0, 0)),
                  pl.BlockSpec(w['w_ukt'].shape, lambda b, k, pt: (0, 0)),
                  pl.BlockSpec(w['g_kn_row'].shape, lambda b, k, pt: (0, 0)),
                  pl.BlockSpec(memory_space=pl.ANY),
                  pl.BlockSpec(memory_space=pl.ANY)],
        out_specs=pl.BlockSpec((1, N_HEADS, KV_RANK), lambda b, k, pt: (b, 0, 0)),
        scratch_shapes=[pltpu.VMEM((wext_rows, KV_RANK), BF16),
                        pltpu.VMEM((2, KV_TILE, KV_RANK), F32),
                        pltpu.VMEM((2, KV_TILE, QK_ROPE), F32),
                        pltpu.SemaphoreType.DMA((2, 2)),
                        pltpu.VMEM((N_HEADS, 1), F32),
                        pltpu.VMEM((N_HEADS, 1), F32),
                        pltpu.VMEM((N_HEADS, KV_RANK), F32)])
    return pl.pallas_call(
        functools.partial(_sample_attn_kernel, n_tiles=n_tiles, n_pages=n_pages),
        grid_spec=grid_spec,
        out_shape=jax.ShapeDtypeStruct((n_samp, N_HEADS, KV_RANK), F32),
        compiler_params=_params(("arbitrary", "arbitrary")),
        name="sample_attn",
    )(page_table.reshape(-1), qn, qr, cs, krs, w['w_ukt_f32'], w['w_ukt'], w['g_kn_row'],
      cache_c, cache_r)


def _mix_proj_kernel(x_ref, o_ref, u_ref, v_ref, wuv_ref, ws_ref, bias_ref, wo_ref, g_ref,
                     h_ref, hn_ref, *, single_token):
    tm = x_ref.shape[0]
    attn = _dot(o_ref[...], wuv_ref[...])
    v = v_ref[...]
    if single_token:
        mixed = v * ws_ref[...] + bias_ref[...]
    else:
        vb = v.astype(BF16)
        row = lax.broadcasted_iota(jnp.int32, (CHUNK, CHUNK), 0)
        col = lax.broadcasted_iota(jnp.int32, (CHUNK, CHUNK), 1)
        w_tril = [jnp.where(col <= row, ws_ref[g], 0.0).astype(BF16) for g in range(N_CG)]
        low_half = lax.broadcasted_iota(jnp.int32, (CHUNK, LANES), 1) < CG_DIM
        chunks = []
        for c in range(tm // CHUNK):
            cols = []
            for j in range(CG_WIDTH // LANES):
                vp = vb[c * CHUNK:(c + 1) * CHUNK, j * LANES:(j + 1) * LANES]
                cols.append(jnp.where(low_half, _dot(w_tril[2 * j], vp), _dot(w_tril[2 * j + 1], vp)))
            chunks.append(jnp.concatenate(cols, axis=1) + bias_ref[...])
        mixed = jnp.concatenate(chunks, axis=0)
    sg = u_ref[...] * mixed
    width = attn.shape[1]
    h = x_ref[...] + _dot(attn.astype(BF16), wo_ref[:width, :]) + _dot(sg.astype(BF16), wo_ref[width:, :])
    h_ref[...] = h
    hn_ref[...] = (_rms(h) * g_ref[...]).astype(BF16)


def _mix_proj(x, o_lat, u, v, w, tm, single_token):
    n = x.shape[0]
    tok = lambda width: pl.BlockSpec((tm, width), lambda i: (i, 0))
    ws, bias = (w['ws_diag'], w['bias_first']) if single_token else (w['w_s'], w['bias_tab'])
    weights = [w['w_uv_bd'], ws, bias, w['w_o'], w['g_ffn']]
    return pl.pallas_call(
        functools.partial(_mix_proj_kernel, single_token=single_token),
        grid=(n // tm,),
        in_specs=[tok(D_MODEL), tok(N_HEADS * KV_RANK), tok(CG_WIDTH), tok(CG_WIDTH)]
                 + [_full(a.shape) for a in weights],
        out_specs=[tok(D_MODEL), tok(D_MODEL)],
        out_shape=[jax.ShapeDtypeStruct((n, D_MODEL), F32),
                   jax.ShapeDtypeStruct((n, D_MODEL), BF16)],
        compiler_params=_params(("parallel",)),
        name="mix_proj_single" if single_token else "mix_proj",
    )(x, o_lat, u, v, *weights)


def _conv_ffn_kernel(*refs, single_token):
    if single_token:
        (h_ref, hn_ref, wg_ref, wu_ref, cwg_ref, cwu_ref, cbg_ref, cbu_ref, wout_ref,
         hg0_ref, hg1_ref, hu0_ref, hu1_ref, out_ref, ag_ref, au_ref) = refs
    else:
        (h_ref, hn_ref, wg_ref, wu_ref, cwg_ref, cwu_ref, cbg_ref, cbu_ref, wout_ref,
         out_ref, ag_ref, au_ref, carry_g, carry_u) = refs
    t = pl.program_id(1)
    j = pl.program_id(2)
    tm = hn_ref.shape[0]
    hn = hn_ref[...]

    def conv(a, cw_ref, cb_ref, carry, prev2, prev1, last_ref):
        if single_token:
            a2, a1 = prev2[...], prev1[...]
            last_ref[...] = a
        else:
            @pl.when(t == 0)
            def _():
                carry[j] = jnp.zeros(carry.shape[1:], F32)
            ext = jnp.concatenate([carry[j], a], axis=0)
            a1 = ext[SUBLANES - 1:SUBLANES - 1 + tm]
            a2 = ext[SUBLANES - 2:SUBLANES - 2 + tm]
            tail = a[tm - SUBLANES:]
            carry[j] = tail
            last_ref[0] = tail[SUBLANES - (CONV_W - 1):]
        return cb_ref[...] + cw_ref[0:1, :] * a2 + cw_ref[1:2, :] * a1 + cw_ref[2:3, :] * a

    if single_token:
        cg = conv(_dot(hn, wg_ref[...]), cwg_ref, cbg_ref, None, hg0_ref, hg1_ref, ag_ref)
        cu = conv(_dot(hn, wu_ref[...]), cwu_ref, cbu_ref, None, hu0_ref, hu1_ref, au_ref)
    else:
        cg = conv(_dot(hn, wg_ref[...]), cwg_ref, cbg_ref, carry_g, None, None, ag_ref)
        cu = conv(_dot(hn, wu_ref[...]), cwu_ref, cbu_ref, carry_u, None, None, au_ref)
    act = (jax.nn.silu(cg) * cu).astype(BF16)
    f = _dot(act, wout_ref[...])

    @pl.when(j == 0)
    def _():
        out_ref[...] = h_ref[...] + f

    @pl.when(j > 0)
    def _():
        out_ref[...] += f


def _conv_ffn(h, hn, w, batch, seq, tm, hist=None):
    single_token = hist is not None
    nt = seq // tm
    nj = D_FF // FF_TILE
    tok = lambda width: pl.BlockSpec((tm, width), lambda b, t, j: (b * nt + t, 0))
    col_g = lambda rows: pl.BlockSpec((rows, FF_TILE), lambda b, t, j: (0, j))
    col_u = lambda rows: pl.BlockSpec((rows, FF_TILE), lambda b, t, j: (0, nj + j))
    in_specs = [tok(D_MODEL), tok(D_MODEL), col_g(D_MODEL), col_u(D_MODEL),
                col_g(CONV_W), col_u(CONV_W), col_g(1), col_u(1),
                pl.BlockSpec((FF_TILE, D_MODEL), lambda b, t, j: (j, 0))]
    args = [h, hn, w['w_ff_in'], w['w_ff_in'], w['conv_w'], w['conv_w'], w['conv_b'], w['conv_b'],
            w['w_ff_out']]
    if single_token:
        in_specs += [col_g(tm), col_g(tm), col_u(tm), col_u(tm)]
        args += [hist[0], hist[1], hist[0], hist[1]]
        last_rows = tm
        last_specs = [pl.BlockSpec((tm, FF_TILE), lambda b, t, j: (0, j))] * 2
        last_shape = jax.ShapeDtypeStruct((tm, D_FF), F32)
        scratch = []
    else:
        last_rows = CONV_W - 1
        last_specs = [pl.BlockSpec((1, last_rows, FF_TILE), lambda b, t, j: (b * nt + t, 0, j))] * 2
        last_shape = jax.ShapeDtypeStruct((batch * nt, last_rows, D_FF), F32)
        scratch = [pltpu.VMEM((nj, SUBLANES, FF_TILE), F32)] * 2
    return pl.pallas_call(
        functools.partial(_conv_ffn_kernel, single_token=single_token),
        grid=(batch, nt, nj),
        in_specs=in_specs,
        out_specs=[tok(D_MODEL)] + last_specs,
        out_shape=[jax.ShapeDtypeStruct((batch * seq, D_MODEL), F32), last_shape, last_shape],
        scratch_shapes=scratch,
        compiler_params=_params(("arbitrary", "arbitrary", "arbitrary")),
        name="conv_ffn_single" if single_token else "conv_ffn",
    )(*args)


def _ple_kernel(h_ref, p_ref, g_ref, wgate_ref, wproj_ref, gpost_ref, y_ref):
    h = h_ref[...]
    gate = jax.nn.sigmoid(_dot((_rms(h) * g_ref[...]).astype(BF16), wgate_ref[...]))
    e = _rms(_dot(p_ref[...].astype(BF16), wproj_ref[...])) * gpost_ref[...]
    y_ref[...] = h + gate * e


def _ple(h, p, w, tm):
    n = h.shape[0]
    tok = lambda width: pl.BlockSpec((tm, width), lambda i: (i, 0))
    weights = [w['g_ple'], w['w_gate'], w['w_proj'], w['g_post']]
    return pl.pallas_call(
        _ple_kernel,
        grid=(n // tm,),
        in_specs=[tok(D_MODEL), tok(PLE_DIM)] + [_full(a.shape) for a in weights],
        out_specs=tok(D_MODEL),
        out_shape=jax.ShapeDtypeStruct((n, D_MODEL), F32),
        compiler_params=_params(("parallel",)),
        name="ple",
    )(h, p, *weights)


def _rope_tables(pos):
    inv = ROPE_THETA ** (-jnp.arange(ROPE_HALF, dtype=F32) * (2.0 / QK_ROPE))
    ang = pos.astype(F32)[:, None] * inv[None, :]
    cos, sin = jnp.cos(ang), jnp.sin(ang)
    n = pos.shape[0]
    ones = jnp.ones((n, QK_NOPE), F32)
    zq = jnp.zeros((n, QK_NOPE), F32)
    zp = jnp.zeros((n, HEAD_PAD - QK_NOPE - QK_ROPE), F32)
    ca = jnp.concatenate([ones, cos, cos, zp], axis=1)
    sb = jnp.concatenate([zq, -sin, sin, zp], axis=1)
    return ca, sb, cos.T, sin.T


def _prep_weights(i, attn_norm_w, w_in, q_norm_w, w_uq, q_nope_norm_w, q_rope_norm_w, kv_norm_w,
                  k_rope_norm_w, w_uk, k_nope_norm_w, w_uv, w_s, b_s, w_o, ffn_norm_w, w_ff_in,
                  conv_w, conv_b, w_ff_out, ple_norm_w, w_ple_gate, w_ple_proj, ple_post_norm_w):
    row = lambda g: g[i][None, :]
    swap = lambda a: jnp.concatenate([a[..., ROPE_HALF:], a[..., :ROPE_HALF]], axis=-1)
    win = w_in[i]
    wq = w_uq[i].reshape(Q_RANK, N_HEADS, QK_NOPE + QK_ROPE)
    nope, rope = wq[..., :QK_NOPE], wq[..., QK_NOPE:]
    pad = HEAD_PAD - QK_NOPE - QK_ROPE
    zeros = lambda width: jnp.zeros((Q_RANK, N_HEADS, width), F32)
    w_qa = jnp.concatenate([nope, rope, zeros(pad)], axis=-1).reshape(Q_RANK, QK_WIDTH)
    w_qb = jnp.concatenate([zeros(QK_NOPE), swap(rope), zeros(pad)], axis=-1).reshape(Q_RANK, QK_WIDTH)
    gn, gr = q_nope_norm_w[i], q_rope_norm_w[i]
    ga = jnp.tile(jnp.concatenate([gn, gr, jnp.zeros((pad,), F32)]), N_HEADS)[None, :]
    gb = jnp.tile(jnp.concatenate([jnp.zeros((QK_NOPE,), F32), swap(gr), jnp.zeros((pad,), F32)]),
                  N_HEADS)[None, :]
    lane = jnp.arange(2 * HEAD_PAD)
    slot, off = lane // HEAD_PAD, lane % HEAD_PAD
    grp = jnp.where(off < QK_NOPE, 0, jnp.where(off < QK_NOPE + QK_ROPE, 1, 2))
    same = (slot[:, None] == slot[None, :]) & (grp[:, None] == grp[None, :]) & (grp[:, None] < 2)
    gq = jnp.where(same, jnp.where(grp[:, None] == 0, 1.0 / QK_NOPE, 1.0 / QK_ROPE), 0.0)
    w_ukt = w_uk[i].reshape(KV_RANK, N_HEADS * QK_NOPE).T
    eye = jnp.eye(N_HEADS, dtype=F32)
    w_uv_bd = jnp.einsum('rhd,hg->hrgd', w_uv[i], eye).reshape(N_HEADS * KV_RANK, N_HEADS * V_DIM)
    return {
        'g_attn': row(attn_norm_w),
        'w_in': jnp.concatenate([win[:, :OFF_KR], win[:, OFF_U:]], axis=1).astype(BF16),
        'g_q': row(q_norm_w), 'w_qa': w_qa.astype(BF16), 'w_qb': w_qb.astype(BF16),
        'gq': gq.astype(BF16), 'ga': ga, 'gb': gb,
        'g_kv': row(kv_norm_w),
        'w_ukt': w_ukt.astype(BF16), 'w_ukt_f32': w_ukt,
        'g_kn': jnp.broadcast_to(jnp.tile(k_nope_norm_w[i], N_HEADS)[:, None], (N_HEADS * QK_NOPE, LANES)),
        'g_kn_row': jnp.tile(k_nope_norm_w[i], N_HEADS)[None, :],
        'w_krt': win[:, OFF_KR:OFF_U].T.astype(BF16),
        'g_kr': jnp.broadcast_to(k_rope_norm_w[i][:, None], (QK_ROPE, LANES)),
        'w_uv_bd': w_uv_bd.astype(BF16),
        'w_s': w_s[i],
        'bias_tab': jnp.repeat(b_s[i].T, CG_DIM, axis=1),
        'ws_diag': jnp.repeat(w_s[i][:, 0, 0], CG_DIM)[None, :],
        'bias_first': jnp.repeat(b_s[i][:, 0], CG_DIM)[None, :],
        'w_o': w_o[i].astype(BF16), 'g_ffn': row(ffn_norm_w),
        'w_ff_in': w_ff_in[i].astype(BF16), 'conv_w': conv_w[i], 'conv_b': row(conv_b),
        'w_ff_out': w_ff_out[i].astype(BF16),
        'g_ple': row(ple_norm_w), 'w_gate': w_ple_gate[i].astype(BF16),
        'w_proj': w_ple_proj[i].astype(BF16), 'g_post': row(ple_post_norm_w),
    }


def kernel(x_prompt, x_sample, cache_ckv, cache_krope, state_conv, page_table, p_prompt, p_sample,
           attn_norm_w, w_in, q_norm_w, w_uq, q_nope_norm_w, q_rope_norm_w, kv_norm_w, k_rope_norm_w,
           w_uk, k_nope_norm_w, w_uv, w_s, b_s, w_o, ffn_norm_w, w_ff_in, conv_w, conv_b, w_ff_out,
           ple_norm_w, w_ple_gate, w_ple_proj, ple_post_norm_w):
    batch, seq, _ = x_prompt.shape
    n_samp, seq_s, _ = x_sample.shape
    depth = w_in.shape[0]
    assert depth == 1 and seq_s == 1 and seq % TOK_TILE == 0 and seq % CHUNK == 0
    past_len = page_table.shape[1] * PAGE_SIZE
    assert past_len % KV_TILE == 0
    w = _prep_weights(0, attn_norm_w, w_in, q_norm_w, w_uq, q_nope_norm_w, q_rope_norm_w, kv_norm_w,
                      k_rope_norm_w, w_uk, k_nope_norm_w, w_uv, w_s, b_s, w_o, ffn_norm_w, w_ff_in,
                      conv_w, conv_b, w_ff_out, ple_norm_w, w_ple_gate, w_ple_proj, ple_post_norm_w)

    n_tok = batch * seq
    tabs = _rope_tables(jnp.arange(seq, dtype=jnp.int32))
    xp = x_prompt.reshape(n_tok, D_MODEL)
    q, kt, ckv, cb, krt, u, v = _in_proj(xp, tabs, w, TOK_TILE)
    o_lat = _prompt_attn(q, kt, cb, batch, seq)
    h, hn = _mix_proj(xp, o_lat, u, v, w, TOK_TILE, single_token=False)
    h2, ag, au = _conv_ffn(h, hn, w, batch, seq, TOK_TILE)
    y_prompt = _ple(h2, p_prompt[0].reshape(n_tok, PLE_DIM), w, TOK_TILE).reshape(batch, seq, D_MODEL)
    n_keep = (seq - 1) % CHUNK + 1
    new_ckv_p = ckv.reshape(1, batch, seq, KV_RANK)
    new_kr_p = krt.T.reshape(1, batch, seq, QK_ROPE)
    new_v_p = v.reshape(batch, seq, CG_WIDTH)[:, seq - n_keep:][None]
    last = lambda a: a.reshape(batch, seq // TOK_TILE, CONV_W - 1, D_FF)[:, -1]
    new_conv_p = jnp.concatenate([last(ag), last(au)], axis=-1)[None]

    tabs_s = _rope_tables(jnp.full((n_samp,), past_len, dtype=jnp.int32))
    xs = x_sample.reshape(n_samp, D_MODEL)
    q_s, _, ckv_s, _, krt_s, u_s, v_s = _in_proj(xs, tabs_s, w, n_samp)
    q3 = q_s.astype(F32).reshape(n_samp, N_HEADS, HEAD_PAD)
    kr_s = krt_s.T
    o_s = _sample_attn(page_table, q3[:, :, :QK_NOPE], q3[:, :, QK_NOPE:QK_NOPE + QK_ROPE],
                       ckv_s, kr_s, w, cache_ckv[0], cache_krope[0])
    o_s = o_s.reshape(n_samp, N_HEADS * KV_RANK).astype(BF16)
    h_s, hn_s = _mix_proj(xs, o_s, u_s, v_s, w, n_samp, single_token=True)
    hist = (state_conv[0, :, 0, :], state_conv[0, :, 1, :])
    h2_s, ag_s, au_s = _conv_ffn(h_s, hn_s, w, 1, n_samp, n_samp, hist=hist)
    y_sample = _ple(h2_s, p_sample[0].reshape(n_samp, PLE_DIM), w, n_samp).reshape(n_samp, 1, D_MODEL)
    a_s = jnp.concatenate([ag_s, au_s], axis=-1)
    new_conv_s = jnp.stack([hist[1], a_s], axis=1)[None]

    return (y_prompt, y_sample, new_ckv_p, new_kr_p,
            ckv_s.reshape(1, n_samp, 1, KV_RANK), kr_s.reshape(1, n_samp, 1, QK_ROPE),
            new_v_p, v_s.reshape(1, n_samp, 1, CG_WIDTH), new_conv_p, new_conv_s)
```

```python
import functools

import jax
import jax.numpy as jnp
from jax import lax
from jax.experimental import pallas as pl
from jax.experimental.pallas import tpu as pltpu

F32 = jnp.float32
BF16 = jnp.bfloat16

D_MODEL = 1024
N_HEADS = 8
QK_NOPE = 64
QK_ROPE = 32
ROPE_HALF = QK_ROPE // 2
V_DIM = 64
Q_RANK = 256
KV_RANK = 128
CHUNK = 128
N_CG = 8
CG_WIDTH = 512
CG_DIM = 64
D_FF = 2816
CONV_W = 3
PLE_DIM = 256
PAGE_SIZE = 128
ROPE_THETA = 10000.0
EPS = 1e-6
SCALE = (QK_NOPE + QK_ROPE) ** -0.5
Q_SCALE = SCALE * 1.4426950408889634
OFF_KV = Q_RANK
OFF_KR = OFF_KV + KV_RANK
OFF_U = OFF_KR + QK_ROPE
OFF_V = OFF_U + CG_WIDTH
IN_WIDTH = OFF_V + CG_WIDTH
HEAD_PAD = 128
QK_WIDTH = N_HEADS * HEAD_PAD
Z_WIDTH = Q_RANK + KV_RANK + 2 * CG_WIDTH

LANES = 128
SUBLANES = 8
VMEM_LIMIT_BYTES = 56 * 1024 * 1024

TOK_TILE = 512
ATT_BLOCK = 256
FF_TILE = 256
KV_TILE = 2048

NT_DIMS = (((1,), (1,)), ((), ()))


def _params(semantics):
    return pltpu.CompilerParams(dimension_semantics=semantics,
                                vmem_limit_bytes=VMEM_LIMIT_BYTES)


def _rms(x):
    return x * lax.rsqrt(jnp.mean(x * x, axis=-1, keepdims=True) + EPS)


def _dot(a, b):
    return jnp.dot(a, b, preferred_element_type=F32)


def _dot_nt(a, b):
    return lax.dot_general(a, b, NT_DIMS, preferred_element_type=F32)


def _full(shape):
    zeros = (0,) * len(shape)
    return pl.BlockSpec(shape, lambda *_: zeros)


def _in_proj_kernel(x_ref, ca_ref, sb_ref, cost_ref, sint_ref,
                    g_attn_ref, w_in_ref, g_q_ref, w_qa_ref, w_qb_ref, gq_ref, ga_ref, gb_ref,
                    g_kv_ref, w_ukt_ref, g_kn_ref, w_krt_ref, g_kr_ref,
                    q_ref, kt_ref, ckv_ref, cb_ref, krt_ref, u_ref, v_ref):
    tm = x_ref.shape[0]
    hb = (_rms(x_ref[...]) * g_attn_ref[...]).astype(BF16)
    z = _dot(hb, w_in_ref[...])
    u_ref[...] = jax.nn.gelu(z[:, OFF_KR:OFF_KR + CG_WIDTH])
    v_ref[...] = jax.nn.gelu(z[:, OFF_KR + CG_WIDTH:])
    ckv = _rms(z[:, OFF_KV:OFF_KR]) * g_kv_ref[...]
    ckv_ref[...] = ckv
    cb = ckv.astype(BF16)
    cb_ref[...] = cb

    qln = (_rms(z[:, :Q_RANK]) * g_q_ref[...]).astype(BF16)
    qa = _dot(qln, w_qa_ref[...])
    qb = _dot(qln, w_qb_ref[...])
    sq = qa * qa
    sq_hi = sq.astype(BF16)
    sq_lo = (sq - sq_hi.astype(F32)).astype(BF16)
    gq = gq_ref[...]
    pair = 2 * HEAD_PAD
    ms = jnp.concatenate(
        [_dot(sq_hi[:, i * pair:(i + 1) * pair], gq) + _dot(sq_lo[:, i * pair:(i + 1) * pair], gq)
         for i in range(N_HEADS // 2)], axis=1)
    ca = jnp.tile(ca_ref[...], (1, N_HEADS))
    sb = jnp.tile(sb_ref[...], (1, N_HEADS))
    q = lax.rsqrt(ms + EPS) * (qa * ga_ref[...] * ca + qb * gb_ref[...] * sb)
    q_ref[...] = (q * Q_SCALE).astype(BF16)

    reps = tm // LANES
    knt = _dot_nt(w_ukt_ref[...], cb).reshape(N_HEADS, QK_NOPE, tm)
    kn = knt * lax.rsqrt(jnp.mean(knt * knt, axis=1, keepdims=True) + EPS)
    kn = kn * jnp.tile(g_kn_ref[...], (1, reps)).reshape(N_HEADS, QK_NOPE, tm)
    krt = _dot_nt(w_krt_ref[...], hb)
    krt = krt * lax.rsqrt(jnp.mean(krt * krt, axis=0, keepdims=True) + EPS)
    krt = krt * jnp.tile(g_kr_ref[...], (1, reps))
    x1, x2 = krt[:ROPE_HALF], krt[ROPE_HALF:]
    cos, sin = cost_ref[...], sint_ref[...]
    kr = jnp.concatenate([x1 * cos - x2 * sin, x1 * sin + x2 * cos], axis=0)
    krt_ref[...] = kr
    kr_b = kr.astype(BF16)
    pad = jnp.zeros((HEAD_PAD - QK_NOPE - QK_ROPE, tm), BF16)
    for h in range(N_HEADS):
        kt_ref[h] = jnp.concatenate([kn[h].astype(BF16), kr_b, pad], axis=0)


def _in_proj(x, tabs, w, tm):
    n = x.shape[0]
    ca, sb, cost, sint = tabs
    pos_tiles = ca.shape[0] // tm
    tok = lambda width: pl.BlockSpec((tm, width), lambda i: (i, 0))
    tok_t = lambda rows: pl.BlockSpec((rows, tm), lambda i: (0, i))
    pos = pl.BlockSpec((tm, HEAD_PAD), lambda i: (i % pos_tiles, 0))
    pos_t = pl.BlockSpec((ROPE_HALF, tm), lambda i: (0, i % pos_tiles))
    weights = [w['g_attn'], w['w_in'], w['g_q'], w['w_qa'], w['w_qb'], w['gq'], w['ga'], w['gb'],
               w['g_kv'], w['w_ukt'], w['g_kn'], w['w_krt'], w['g_kr']]
    return pl.pallas_call(
        _in_proj_kernel,
        grid=(n // tm,),
        in_specs=[tok(D_MODEL), pos, pos, pos_t, pos_t]
                 + [_full(a.shape) for a in weights],
        out_specs=[tok(QK_WIDTH),
                   pl.BlockSpec((N_HEADS, HEAD_PAD, tm), lambda i: (0, 0, i)),
                   tok(KV_RANK), tok(KV_RANK), tok_t(QK_ROPE), tok(CG_WIDTH), tok(CG_WIDTH)],
        out_shape=[jax.ShapeDtypeStruct((n, QK_WIDTH), BF16),
                   jax.ShapeDtypeStruct((N_HEADS, HEAD_PAD, n), BF16),
                   jax.ShapeDtypeStruct((n, KV_RANK), F32),
                   jax.ShapeDtypeStruct((n, KV_RANK), BF16),
                   jax.ShapeDtypeStruct((QK_ROPE, n), F32),
                   jax.ShapeDtypeStruct((n, CG_WIDTH), F32),
                   jax.ShapeDtypeStruct((n, CG_WIDTH), F32)],
        compiler_params=_params(("parallel",)),
        name="in_proj",
    )(x, ca, sb, cost, sint, *weights)


def _prompt_attn_kernel(q_ref, kt_ref, v_ref, o_ref, m_s, l_s, acc_s):
    qi = pl.program_id(1)
    blk = ATT_BLOCK
    m_s[...] = jnp.full(m_s.shape, -jnp.inf, F32)
    l_s[...] = jnp.zeros(l_s.shape, F32)
    acc_s[...] = jnp.zeros(acc_s.shape, F32)

    def key_block(off, mask):
        vals = v_ref[pl.ds(off, blk), :]
        for h in range(N_HEADS):
            s = _dot(q_ref[:, h * HEAD_PAD:(h + 1) * HEAD_PAD], kt_ref[h, :, pl.ds(off, blk)])
            if mask is not None:
                s = jnp.where(mask, s, -jnp.inf)
            m_old = m_s[h]
            m_new = jnp.maximum(m_old, jnp.max(s, axis=-1, keepdims=True))
            alpha = jnp.exp2(m_old - m_new)
            p = jnp.exp2(s - m_new)
            l_s[h] = alpha * l_s[h] + jnp.sum(p, axis=-1, keepdims=True)
            acc_s[h] = alpha * acc_s[h] + _dot(p.astype(BF16), vals)
            m_s[h] = m_new

    def body(j, carry):
        key_block(pl.multiple_of(j * blk, blk), None)
        return carry

    lax.fori_loop(0, qi, body, 0)
    row = lax.broadcasted_iota(jnp.int32, (blk, blk), 0)
    col = lax.broadcasted_iota(jnp.int32, (blk, blk), 1)
    key_block(pl.multiple_of(qi * blk, blk), col <= row)
    for h in range(N_HEADS):
        o_ref[:, h * HEAD_PAD:(h + 1) * HEAD_PAD] = (acc_s[h] / l_s[h]).astype(BF16)


def _prompt_attn(q, kt, cb, batch, seq):
    nq = seq // ATT_BLOCK
    return pl.pallas_call(
        _prompt_attn_kernel,
        grid=(batch, nq),
        in_specs=[pl.BlockSpec((ATT_BLOCK, QK_WIDTH), lambda b, i: (b * nq + i, 0)),
                  pl.BlockSpec((N_HEADS, HEAD_PAD, seq), lambda b, i: (0, 0, b)),
                  pl.BlockSpec((seq, KV_RANK), lambda b, i: (b, 0))],
        out_specs=pl.BlockSpec((ATT_BLOCK, N_HEADS * KV_RANK), lambda b, i: (b * nq + i, 0)),
        out_shape=jax.ShapeDtypeStruct((batch * seq, N_HEADS * KV_RANK), BF16),
        scratch_shapes=[pltpu.VMEM((N_HEADS, ATT_BLOCK, 1), F32),
                        pltpu.VMEM((N_HEADS, ATT_BLOCK, 1), F32),
                        pltpu.VMEM((N_HEADS, ATT_BLOCK, KV_RANK), F32)],
        compiler_params=_params(("parallel", "parallel")),
        name="prompt_attn",
    )(q, kt, cb)


def _sample_attn_kernel(pt_ref, qn_ref, qr_ref, cs_ref, krs_ref, wuk_ref, wukb_ref, gkn_ref,
                        cache_c, cache_r, o_ref,
                        wext, cbuf, rbuf, sems, cb_all, s_all, *, n_pages):
    b = pl.program_id(0)
    slot = lax.rem(b, 2)
    past = n_pages * PAGE_SIZE

    def page_copy(bb, sl, i):
        page = pt_ref[bb * n_pages + i]
        off = pl.multiple_of(i * PAGE_SIZE, PAGE_SIZE)
        return (pltpu.make_async_copy(cache_c.at[page], cbuf.at[sl, pl.ds(off, PAGE_SIZE)], sems.at[0, sl]),
                pltpu.make_async_copy(cache_r.at[page], rbuf.at[sl, :, pl.ds(off, PAGE_SIZE)], sems.at[1, sl]))

    def start_pages(bb, sl):
        def issue(i, _):
            for c in page_copy(bb, sl, i):
                c.start()
            return 0
        lax.fori_loop(0, n_pages, issue, 0)

    @pl.when(b == 0)
    def _():
        start_pages(0, 0)

    @pl.when(b + 1 < pl.num_programs(0))
    def _():
        start_pages(b + 1, 1 - slot)

    nw = N_HEADS * QK_NOPE
    qg = jnp.tile(qn_ref[0], (1, N_HEADS)) * gkn_ref[...]
    head = lax.broadcasted_iota(jnp.int32, (N_HEADS, nw), 0)
    lane = lax.broadcasted_iota(jnp.int32, (N_HEADS, nw), 1)
    qbd = jnp.where((lane >= head * QK_NOPE) & (lane < (head + 1) * QK_NOPE), qg, 0.0)
    qa = jnp.dot(qbd, wuk_ref[...], precision=lax.Precision.HIGHEST,
                 preferred_element_type=F32)
    wext[:nw, :] = wukb_ref[...]
    wext[nw:, :] = jnp.concatenate(
        [qa, jnp.zeros((wext.shape[0] - nw - N_HEADS, KV_RANK), F32)], axis=0).astype(BF16)
    qr = qr_ref[0].astype(BF16)

    def scores(c_blk, krt_blk):
        tk = c_blk.shape[0]
        cb = c_blk.astype(BF16)
        knt = _dot_nt(wext[...], cb)
        kn3 = knt[:nw].reshape(N_HEADS, QK_NOPE, tk)
        r = lax.rsqrt(jnp.sum(kn3 * kn3, axis=1) * (1.0 / QK_NOPE) + EPS)
        s_rope = _dot(qr, krt_blk.astype(BF16))
        return cb, knt[nw:nw + N_HEADS] * r + s_rope

    own = lax.broadcasted_iota(jnp.int32, (N_HEADS, cs_ref.shape[0]), 1) == b
    cb_own, s_own = scores(cs_ref[...], krs_ref[...])
    s_own = jnp.where(own, s_own, -jnp.inf)

    def wait_page(i, _):
        for c in page_copy(b, slot, i):
            c.wait()
        return 0
    lax.fori_loop(0, n_pages, wait_page, 0)

    def tile(j, _):
        off = pl.multiple_of(j * KV_TILE, KV_TILE)
        cb, s = scores(cbuf[slot, pl.ds(off, KV_TILE)], rbuf[slot, :, pl.ds(off, KV_TILE)])
        cb_all[pl.ds(off, KV_TILE), :] = cb
        s_all[:, pl.ds(off, KV_TILE)] = s
        return 0
    lax.fori_loop(0, past // KV_TILE, tile, 0, unroll=True)

    s = s_all[...]
    m = jnp.maximum(jnp.max(s, axis=-1, keepdims=True), jnp.max(s_own, axis=-1, keepdims=True))
    p = jnp.exp2(s - m)
    p_own = jnp.exp2(s_own - m)
    l = jnp.sum(p, axis=-1, keepdims=True) + jnp.sum(p_own, axis=-1, keepdims=True)
    acc = _dot(p.astype(BF16), cb_all[...]) + _dot(p_own.astype(BF16), cb_own)
    o_ref[0] = acc / l


def _sample_attn(page_table, qn, qr, cs, krs_t, w, cache_c, cache_rt):
    n_samp, n_pages = page_table.shape
    past = n_pages * PAGE_SIZE
    wext_rows = N_HEADS * QK_NOPE + 2 * SUBLANES
    grid_spec = pltpu.PrefetchScalarGridSpec(
        num_scalar_prefetch=1,
        grid=(n_samp,),
        in_specs=[pl.BlockSpec((1, N_HEADS, QK_NOPE), lambda b, pt: (b, 0, 0)),
                  pl.BlockSpec((1, N_HEADS, QK_ROPE), lambda b, pt: (b, 0, 0)),
                  pl.BlockSpec(cs.shape, lambda b, pt: (0, 0)),
                  pl.BlockSpec(krs_t.shape, lambda b, pt: (0, 0)),
                  pl.BlockSpec(w['w_ukt_f32'].shape, lambda b, pt: (0, 0)),
                  pl.BlockSpec(w['w_ukt'].shape, lambda b, pt: (0, 0)),
                  pl.BlockSpec(w['g_kn_row'].shape, lambda b, pt: (0, 0)),
                  pl.BlockSpec(memory_space=pl.ANY),
                  pl.BlockSpec(memory_space=pl.ANY)],
        out_specs=pl.BlockSpec((1, N_HEADS, KV_RANK), lambda b, pt: (b, 0, 0)),
        scratch_shapes=[pltpu.VMEM((wext_rows, KV_RANK), BF16),
                        pltpu.VMEM((2, past, KV_RANK), F32),
                        pltpu.VMEM((2, QK_ROPE, past), F32),
                        pltpu.SemaphoreType.DMA((2, 2)),
                        pltpu.VMEM((past, KV_RANK), BF16),
                        pltpu.VMEM((N_HEADS, past), F32)])
    return pl.pallas_call(
        functools.partial(_sample_attn_kernel, n_pages=n_pages),
        grid_spec=grid_spec,
        out_shape=jax.ShapeDtypeStruct((n_samp, N_HEADS, KV_RANK), F32),
        compiler_params=_params(("arbitrary",)),
        name="sample_attn",
    )(page_table.reshape(-1), qn, qr, cs, krs_t, w['w_ukt_f32'], w['w_ukt'], w['g_kn_row'],
      cache_c, cache_rt)


def _mix_proj_kernel(x_ref, o_ref, u_ref, v_ref, wuv_ref, ws_ref, bias_ref, wo_ref, g_ref,
                     h_ref, hn_ref, *, single_token):
    tm = x_ref.shape[0]
    attn = _dot(o_ref[...], wuv_ref[...])
    v = v_ref[...]
    if single_token:
        mixed = v * ws_ref[...] + bias_ref[...]
    else:
        vb = v.astype(BF16)
        row = lax.broadcasted_iota(jnp.int32, (CHUNK, CHUNK), 0)
        col = lax.broadcasted_iota(jnp.int32, (CHUNK, CHUNK), 1)
        w_tril = [jnp.where(col <= row, ws_ref[g], 0.0).astype(BF16) for g in range(N_CG)]
        low_half = lax.broadcasted_iota(jnp.int32, (CHUNK, LANES), 1) < CG_DIM
        chunks = []
        for c in range(tm // CHUNK):
            cols = []
            for j in range(CG_WIDTH // LANES):
                vp = vb[c * CHUNK:(c + 1) * CHUNK, j * LANES:(j + 1) * LANES]
                cols.append(jnp.where(low_half, _dot(w_tril[2 * j], vp), _dot(w_tril[2 * j + 1], vp)))
            chunks.append(jnp.concatenate(cols, axis=1) + bias_ref[...])
        mixed = jnp.concatenate(chunks, axis=0)
    sg = u_ref[...] * mixed
    width = attn.shape[1]
    h = x_ref[...] + _dot(attn.astype(BF16), wo_ref[:width, :]) + _dot(sg.astype(BF16), wo_ref[width:, :])
    h_ref[...] = h
    hn_ref[...] = (_rms(h) * g_ref[...]).astype(BF16)


def _mix_proj(x, o_lat, u, v, w, tm, single_token):
    n = x.shape[0]
    tok = lambda width: pl.BlockSpec((tm, width), lambda i: (i, 0))
    ws, bias = (w['ws_diag'], w['bias_first']) if single_token else (w['w_s'], w['bias_tab'])
    weights = [w['w_uv_bd'], ws, bias, w['w_o'], w['g_ffn']]
    return pl.pallas_call(
        functools.partial(_mix_proj_kernel, single_token=single_token),
        grid=(n // tm,),
        in_specs=[tok(D_MODEL), tok(N_HEADS * KV_RANK), tok(CG_WIDTH), tok(CG_WIDTH)]
                 + [_full(a.shape) for a in weights],
        out_specs=[tok(D_MODEL), tok(D_MODEL)],
        out_shape=[jax.ShapeDtypeStruct((n, D_MODEL), F32),
                   jax.ShapeDtypeStruct((n, D_MODEL), BF16)],
        compiler_params=_params(("parallel",)),
        name="mix_proj_single" if single_token else "mix_proj",
    )(x, o_lat, u, v, *weights)


def _conv_ffn_kernel(*refs, single_token):
    if single_token:
        h_ref, hn_ref, win_ref, cw_ref, cb_ref, wout_ref, h0_ref, h1_ref, out_ref, a_ref, act_s = refs
    else:
        h_ref, hn_ref, win_ref, cw_ref, cb_ref, wout_ref, out_ref, last_ref, act_s, carry = refs
        t = pl.program_id(1)

        @pl.when(t == 0)
        def _():
            carry[...] = jnp.zeros(carry.shape, F32)
    tm = hn_ref.shape[0]
    hn = hn_ref[...]

    def conv(cols):
        a = _dot(hn, win_ref[:, cols])
        if single_token:
            a2, a1 = h0_ref[:, cols], h1_ref[:, cols]
            a_ref[:, cols] = a
        else:
            ext = jnp.concatenate([carry[:, cols], a], axis=0)
            a1 = ext[SUBLANES - 1:SUBLANES - 1 + tm]
            a2 = ext[SUBLANES - 2:SUBLANES - 2 + tm]
            tail = a[tm - SUBLANES:]
            carry[:, cols] = tail
            last_ref[0, :, cols] = tail[SUBLANES - (CONV_W - 1):]
        return cb_ref[:, cols] + cw_ref[0:1, cols] * a2 + cw_ref[1:2, cols] * a1 + cw_ref[2:3, cols] * a

    for j in range(D_FF // FF_TILE):
        gate = slice(j * FF_TILE, (j + 1) * FF_TILE)
        up = slice(D_FF + j * FF_TILE, D_FF + (j + 1) * FF_TILE)
        act_s[:, gate] = (jax.nn.silu(conv(gate)) * conv(up)).astype(BF16)
    out_ref[...] = h_ref[...] + _dot(act_s[...], wout_ref[...])


def _conv_ffn(h, hn, w, batch, seq, tm, hist=None):
    single_token = hist is not None
    nt = seq // tm
    tok = lambda width: pl.BlockSpec((tm, width), lambda b, t: (b * nt + t, 0))
    resident = lambda a: pl.BlockSpec(a.shape, lambda b, t: (0, 0), pipeline_mode=pl.Buffered(1))
    weights = [w['w_ff_in'], w['conv_w'], w['conv_b'], w['w_ff_out']]
    in_specs = [tok(D_MODEL), tok(D_MODEL)] + [resident(a) for a in weights]
    args = [h, hn] + weights
    scratch = [pltpu.VMEM((tm, D_FF), BF16)]
    if single_token:
        in_specs += [resident(hist[0]), resident(hist[1])]
        args += list(hist)
        last_spec = pl.BlockSpec((tm, 2 * D_FF), lambda b, t: (0, 0))
        last_shape = jax.ShapeDtypeStruct((tm, 2 * D_FF), F32)
    else:
        last_spec = pl.BlockSpec((1, CONV_W - 1, 2 * D_FF), lambda b, t: (b * nt + t, 0, 0))
        last_shape = jax.ShapeDtypeStruct((batch * nt, CONV_W - 1, 2 * D_FF), F32)
        scratch.append(pltpu.VMEM((SUBLANES, 2 * D_FF), F32))
    return pl.pallas_call(
        functools.partial(_conv_ffn_kernel, single_token=single_token),
        grid=(batch, nt),
        in_specs=in_specs,
        out_specs=[tok(D_MODEL), last_spec],
        out_shape=[jax.ShapeDtypeStruct((batch * seq, D_MODEL), F32), last_shape],
        scratch_shapes=scratch,
        compiler_params=_params(("arbitrary", "arbitrary")),
        name="conv_ffn_single" if single_token else "conv_ffn",
    )(*args)


def _ple_kernel(h_ref, p_ref, g_ref, wgate_ref, wproj_ref, gpost_ref, y_ref):
    h = h_ref[...]
    gate = jax.nn.sigmoid(_dot((_rms(h) * g_ref[...]).astype(BF16), wgate_ref[...]))
    e = _rms(_dot(p_ref[...].astype(BF16), wproj_ref[...])) * gpost_ref[...]
    y_ref[...] = h + gate * e


def _ple(h, p, w, tm):
    n = h.shape[0]
    tok = lambda width: pl.BlockSpec((tm, width), lambda i: (i, 0))
    weights = [w['g_ple'], w['w_gate'], w['w_proj'], w['g_post']]
    return pl.pallas_call(
        _ple_kernel,
        grid=(n // tm,),
        in_specs=[tok(D_MODEL), tok(PLE_DIM)] + [_full(a.shape) for a in weights],
        out_specs=tok(D_MODEL),
        out_shape=jax.ShapeDtypeStruct((n, D_MODEL), F32),
        compiler_params=_params(("parallel",)),
        name="ple",
    )(h, p, *weights)


def _rope_tables(pos):
    inv = ROPE_THETA ** (-jnp.arange(ROPE_HALF, dtype=F32) * (2.0 / QK_ROPE))
    ang = pos.astype(F32)[:, None] * inv[None, :]
    cos, sin = jnp.cos(ang), jnp.sin(ang)
    n = pos.shape[0]
    ones = jnp.ones((n, QK_NOPE), F32)
    zq = jnp.zeros((n, QK_NOPE), F32)
    zp = jnp.zeros((n, HEAD_PAD - QK_NOPE - QK_ROPE), F32)
    ca = jnp.concatenate([ones, cos, cos, zp], axis=1)
    sb = jnp.concatenate([zq, -sin, sin, zp], axis=1)
    return ca, sb, cos.T, sin.T


def _prep_weights(i, attn_norm_w, w_in, q_norm_w, w_uq, q_nope_norm_w, q_rope_norm_w, kv_norm_w,
                  k_rope_norm_w, w_uk, k_nope_norm_w, w_uv, w_s, b_s, w_o, ffn_norm_w, w_ff_in,
                  conv_w, conv_b, w_ff_out, ple_norm_w, w_ple_gate, w_ple_proj, ple_post_norm_w):
    row = lambda g: g[i][None, :]
    swap = lambda a: jnp.concatenate([a[..., ROPE_HALF:], a[..., :ROPE_HALF]], axis=-1)
    win = w_in[i]
    wq = w_uq[i].reshape(Q_RANK, N_HEADS, QK_NOPE + QK_ROPE)
    nope, rope = wq[..., :QK_NOPE], wq[..., QK_NOPE:]
    pad = HEAD_PAD - QK_NOPE - QK_ROPE
    zeros = lambda width: jnp.zeros((Q_RANK, N_HEADS, width), F32)
    w_qa = jnp.concatenate([nope, rope, zeros(pad)], axis=-1).reshape(Q_RANK, QK_WIDTH)
    w_qb = jnp.concatenate([zeros(QK_NOPE), swap(rope), zeros(pad)], axis=-1).reshape(Q_RANK, QK_WIDTH)
    gn, gr = q_nope_norm_w[i], q_rope_norm_w[i]
    ga = jnp.tile(jnp.concatenate([gn, gr, jnp.zeros((pad,), F32)]), N_HEADS)[None, :]
    gb = jnp.tile(jnp.concatenate([jnp.zeros((QK_NOPE,), F32), swap(gr), jnp.zeros((pad,), F32)]),
                  N_HEADS)[None, :]
    lane = jnp.arange(2 * HEAD_PAD)
    slot, off = lane // HEAD_PAD, lane % HEAD_PAD
    grp = jnp.where(off < QK_NOPE, 0, jnp.where(off < QK_NOPE + QK_ROPE, 1, 2))
    same = (slot[:, None] == slot[None, :]) & (grp[:, None] == grp[None, :]) & (grp[:, None] < 2)
    gq = jnp.where(same, jnp.where(grp[:, None] == 0, 1.0 / QK_NOPE, 1.0 / QK_ROPE), 0.0)
    w_ukt = w_uk[i].reshape(KV_RANK, N_HEADS * QK_NOPE).T
    eye = jnp.eye(N_HEADS, dtype=F32)
    w_uv_bd = jnp.einsum('rhd,hg->hrgd', w_uv[i], eye).reshape(N_HEADS * KV_RANK, N_HEADS * V_DIM)
    return {
        'g_attn': row(attn_norm_w),
        'w_in': jnp.concatenate([win[:, :OFF_KR], win[:, OFF_U:]], axis=1).astype(BF16),
        'g_q': row(q_norm_w), 'w_qa': w_qa.astype(BF16), 'w_qb': w_qb.astype(BF16),
        'gq': gq.astype(BF16), 'ga': ga, 'gb': gb,
        'g_kv': row(kv_norm_w),
        'w_ukt': w_ukt.astype(BF16), 'w_ukt_f32': w_ukt,
        'g_kn': jnp.broadcast_to(jnp.tile(k_nope_norm_w[i], N_HEADS)[:, None], (N_HEADS * QK_NOPE, LANES)),
        'g_kn_row': jnp.tile(k_nope_norm_w[i], N_HEADS)[None, :],
        'w_krt': win[:, OFF_KR:OFF_U].T.astype(BF16),
        'g_kr': jnp.broadcast_to(k_rope_norm_w[i][:, None], (QK_ROPE, LANES)),
        'w_uv_bd': w_uv_bd.astype(BF16),
        'w_s': w_s[i],
        'bias_tab': jnp.repeat(b_s[i].T, CG_DIM, axis=1),
        'ws_diag': jnp.repeat(w_s[i][:, 0, 0], CG_DIM)[None, :],
        'bias_first': jnp.repeat(b_s[i][:, 0], CG_DIM)[None, :],
        'w_o': w_o[i].astype(BF16), 'g_ffn': row(ffn_norm_w),
        'w_ff_in': w_ff_in[i].astype(BF16), 'conv_w': conv_w[i], 'conv_b': row(conv_b),
        'w_ff_out': w_ff_out[i].astype(BF16),
        'g_ple': row(ple_norm_w), 'w_gate': w_ple_gate[i].astype(BF16),
        'w_proj': w_ple_proj[i].astype(BF16), 'g_post': row(ple_post_norm_w),
    }


def kernel(x_prompt, x_sample, cache_ckv, cache_krope, state_conv, page_table, p_prompt, p_sample,
           attn_norm_w, w_in, q_norm_w, w_uq, q_nope_norm_w, q_rope_norm_w, kv_norm_w, k_rope_norm_w,
           w_uk, k_nope_norm_w, w_uv, w_s, b_s, w_o, ffn_norm_w, w_ff_in, conv_w, conv_b, w_ff_out,
           ple_norm_w, w_ple_gate, w_ple_proj, ple_post_norm_w):
    batch, seq, _ = x_prompt.shape
    n_samp, seq_s, _ = x_sample.shape
    depth = w_in.shape[0]
    assert depth == 1 and seq_s == 1 and seq % TOK_TILE == 0 and seq % CHUNK == 0
    past_len = page_table.shape[1] * PAGE_SIZE
    assert past_len % KV_TILE == 0
    w = _prep_weights(0, attn_norm_w, w_in, q_norm_w, w_uq, q_nope_norm_w, q_rope_norm_w, kv_norm_w,
                      k_rope_norm_w, w_uk, k_nope_norm_w, w_uv, w_s, b_s, w_o, ffn_norm_w, w_ff_in,
                      conv_w, conv_b, w_ff_out, ple_norm_w, w_ple_gate, w_ple_proj, ple_post_norm_w)

    n_tok = batch * seq
    tabs = _rope_tables(jnp.arange(seq, dtype=jnp.int32))
    xp = x_prompt.reshape(n_tok, D_MODEL)
    q, kt, ckv, cb, krt, u, v = _in_proj(xp, tabs, w, TOK_TILE)
    o_lat = _prompt_attn(q, kt, cb, batch, seq)
    h, hn = _mix_proj(xp, o_lat, u, v, w, TOK_TILE, single_token=False)
    h2, a_last = _conv_ffn(h, hn, w, batch, seq, TOK_TILE)
    y_prompt = _ple(h2, p_prompt[0].reshape(n_tok, PLE_DIM), w, TOK_TILE).reshape(batch, seq, D_MODEL)
    n_keep = (seq - 1) % CHUNK + 1
    new_ckv_p = ckv.reshape(1, batch, seq, KV_RANK)
    new_kr_p = krt.T.reshape(1, batch, seq, QK_ROPE)
    new_v_p = v.reshape(batch, seq, CG_WIDTH)[:, seq - n_keep:][None]
    new_conv_p = a_last.reshape(batch, seq // TOK_TILE, CONV_W - 1, 2 * D_FF)[:, -1][None]

    tabs_s = _rope_tables(jnp.full((n_samp,), past_len, dtype=jnp.int32))
    xs = x_sample.reshape(n_samp, D_MODEL)
    q_s, _, ckv_s, _, krt_s, u_s, v_s = _in_proj(xs, tabs_s, w, n_samp)
    q3 = q_s.astype(F32).reshape(n_samp, N_HEADS, HEAD_PAD)
    kr_s = krt_s.T
    cache_rt = jnp.swapaxes(cache_krope[0], 1, 2)
    o_s = _sample_attn(page_table, q3[:, :, :QK_NOPE], q3[:, :, QK_NOPE:QK_NOPE + QK_ROPE],
                       ckv_s, krt_s, w, cache_ckv[0], cache_rt)
    o_s = o_s.reshape(n_samp, N_HEADS * KV_RANK).astype(BF16)
    h_s, hn_s = _mix_proj(xs, o_s, u_s, v_s, w, n_samp, single_token=True)
    hist = (state_conv[0, :, 0, :], state_conv[0, :, 1, :])
    h2_s, a_s = _conv_ffn(h_s, hn_s, w, 1, n_samp, n_samp, hist=hist)
    y_sample = _ple(h2_s, p_sample[0].reshape(n_samp, PLE_DIM), w, n_samp).reshape(n_samp, 1, D_MODEL)
    new_conv_s = jnp.stack([hist[1], a_s], axis=1)[None]

    return (y_prompt, y_sample, new_ckv_p, new_kr_p,
            ckv_s.reshape(1, n_samp, 1, KV_RANK), kr_s.reshape(1, n_samp, 1, QK_ROPE),
            new_v_p, v_s.reshape(1, n_samp, 1, CG_WIDTH), new_conv_p, new_conv_s)
```

```python
import functools

import jax
import jax.numpy as jnp
from jax import lax
from jax.experimental import pallas as pl
from jax.experimental.pallas import tpu as pltpu

F32 = jnp.float32
BF16 = jnp.bfloat16

D_MODEL = 1024
N_HEADS = 8
QK_NOPE = 64
QK_ROPE = 32
ROPE_HALF = QK_ROPE // 2
V_DIM = 64
Q_RANK = 256
KV_RANK = 128
CHUNK = 128
N_CG = 8
CG_WIDTH = 512
CG_DIM = 64
D_FF = 2816
CONV_W = 3
PLE_DIM = 256
PAGE_SIZE = 128
ROPE_THETA = 10000.0
EPS = 1e-6
SCALE = (QK_NOPE + QK_ROPE) ** -0.5
Q_SCALE = SCALE * 1.4426950408889634
OFF_KV = Q_RANK
OFF_KR = OFF_KV + KV_RANK
OFF_U = OFF_KR + QK_ROPE
OFF_V = OFF_U + CG_WIDTH
IN_WIDTH = OFF_V + CG_WIDTH
HEAD_PAD = 128
QK_WIDTH = N_HEADS * HEAD_PAD
Z_WIDTH = Q_RANK + KV_RANK + 2 * CG_WIDTH

LANES = 128
SUBLANES = 8
VMEM_LIMIT_BYTES = 56 * 1024 * 1024

TOK_TILE = 512
ATT_BLOCK = 512
FF_TILE = 256
KV_TILE = 2048

NT_DIMS = (((1,), (1,)), ((), ()))


def _params(semantics):
    return pltpu.CompilerParams(dimension_semantics=semantics,
                                vmem_limit_bytes=VMEM_LIMIT_BYTES)


def _rms(x):
    return x * lax.rsqrt(jnp.mean(x * x, axis=-1, keepdims=True) + EPS)


def _dot(a, b):
    return jnp.dot(a, b, preferred_element_type=F32)


def _dot_nt(a, b):
    return lax.dot_general(a, b, NT_DIMS, preferred_element_type=F32)


def _full(shape):
    zeros = (0,) * len(shape)
    return pl.BlockSpec(shape, lambda *_: zeros)


def _in_proj_kernel(x_ref, ca_ref, sb_ref, cost_ref, sint_ref,
                    g_attn_ref, w_in_ref, g_q_ref, w_qa_ref, w_qb_ref, gq_ref, ga_ref, gb_ref,
                    g_kv_ref, w_ukt_ref, g_kn_ref, w_krt_ref, g_kr_ref,
                    q_ref, kt_ref, ckv_ref, cb_ref, krt_ref, u_ref, v_ref):
    tm = x_ref.shape[0]
    hb = (_rms(x_ref[...]) * g_attn_ref[...]).astype(BF16)
    z = _dot(hb, w_in_ref[...])
    u_ref[...] = jax.nn.gelu(z[:, OFF_KR:OFF_KR + CG_WIDTH])
    v_ref[...] = jax.nn.gelu(z[:, OFF_KR + CG_WIDTH:])
    ckv = _rms(z[:, OFF_KV:OFF_KR]) * g_kv_ref[...]
    ckv_ref[...] = ckv
    cb = ckv.astype(BF16)
    cb_ref[...] = cb

    qln = (_rms(z[:, :Q_RANK]) * g_q_ref[...]).astype(BF16)
    qa = _dot(qln, w_qa_ref[...])
    qb = _dot(qln, w_qb_ref[...])
    sq = qa * qa
    sq_hi = sq.astype(BF16)
    sq_lo = (sq - sq_hi.astype(F32)).astype(BF16)
    gq = gq_ref[...]
    pair = 2 * HEAD_PAD
    ms = jnp.concatenate(
        [_dot(sq_hi[:, i * pair:(i + 1) * pair], gq) + _dot(sq_lo[:, i * pair:(i + 1) * pair], gq)
         for i in range(N_HEADS // 2)], axis=1)
    ca = jnp.tile(ca_ref[...], (1, N_HEADS))
    sb = jnp.tile(sb_ref[...], (1, N_HEADS))
    q = lax.rsqrt(ms + EPS) * (qa * ga_ref[...] * ca + qb * gb_ref[...] * sb)
    q_ref[...] = (q * Q_SCALE).astype(BF16)

    reps = tm // LANES
    knt = _dot_nt(w_ukt_ref[...], cb).reshape(N_HEADS, QK_NOPE, tm)
    kn = knt * lax.rsqrt(jnp.mean(knt * knt, axis=1, keepdims=True) + EPS)
    kn = kn * jnp.tile(g_kn_ref[...], (1, reps)).reshape(N_HEADS, QK_NOPE, tm)
    krt = _dot_nt(w_krt_ref[...], hb)
    krt = krt * lax.rsqrt(jnp.mean(krt * krt, axis=0, keepdims=True) + EPS)
    krt = krt * jnp.tile(g_kr_ref[...], (1, reps))
    x1, x2 = krt[:ROPE_HALF], krt[ROPE_HALF:]
    cos, sin = cost_ref[...], sint_ref[...]
    kr = jnp.concatenate([x1 * cos - x2 * sin, x1 * sin + x2 * cos], axis=0)
    krt_ref[...] = kr
    kr_b = kr.astype(BF16)
    pad = jnp.zeros((HEAD_PAD - QK_NOPE - QK_ROPE, tm), BF16)
    for h in range(N_HEADS):
        kt_ref[h] = jnp.concatenate([kn[h].astype(BF16), kr_b, pad], axis=0)


def _in_proj(x, tabs, w, tm):
    n = x.shape[0]
    ca, sb, cost, sint = tabs
    pos_tiles = ca.shape[0] // tm
    tok = lambda width: pl.BlockSpec((tm, width), lambda i: (i, 0))
    tok_t = lambda rows: pl.BlockSpec((rows, tm), lambda i: (0, i))
    pos = pl.BlockSpec((tm, HEAD_PAD), lambda i: (i % pos_tiles, 0))
    pos_t = pl.BlockSpec((ROPE_HALF, tm), lambda i: (0, i % pos_tiles))
    weights = [w['g_attn'], w['w_in'], w['g_q'], w['w_qa'], w['w_qb'], w['gq'], w['ga'], w['gb'],
               w['g_kv'], w['w_ukt'], w['g_kn'], w['w_krt'], w['g_kr']]
    return pl.pallas_call(
        _in_proj_kernel,
        grid=(n // tm,),
        in_specs=[tok(D_MODEL), pos, pos, pos_t, pos_t]
                 + [_full(a.shape) for a in weights],
        out_specs=[tok(QK_WIDTH),
                   pl.BlockSpec((N_HEADS, HEAD_PAD, tm), lambda i: (0, 0, i)),
                   tok(KV_RANK), tok(KV_RANK), tok_t(QK_ROPE), tok(CG_WIDTH), tok(CG_WIDTH)],
        out_shape=[jax.ShapeDtypeStruct((n, QK_WIDTH), BF16),
                   jax.ShapeDtypeStruct((N_HEADS, HEAD_PAD, n), BF16),
                   jax.ShapeDtypeStruct((n, KV_RANK), F32),
                   jax.ShapeDtypeStruct((n, KV_RANK), BF16),
                   jax.ShapeDtypeStruct((QK_ROPE, n), F32),
                   jax.ShapeDtypeStruct((n, CG_WIDTH), F32),
                   jax.ShapeDtypeStruct((n, CG_WIDTH), F32)],
        compiler_params=_params(("parallel",)),
        name="in_proj",
    )(x, ca, sb, cost, sint, *weights)


def _prompt_attn_kernel(q_ref, kt_ref, v_ref, o_ref, m_s, acc_s):
    qi = pl.program_id(1)
    blk = ATT_BLOCK
    m_s[...] = jnp.full(m_s.shape, -jnp.inf, F32)
    acc_s[...] = jnp.zeros(acc_s.shape, F32)
    ones = jnp.ones((blk, KV_RANK), BF16)

    def key_block(off, mask):
        vals = jnp.concatenate([v_ref[pl.ds(off, blk), :], ones], axis=1)
        for h in range(N_HEADS):
            s = _dot(q_ref[:, h * HEAD_PAD:(h + 1) * HEAD_PAD], kt_ref[h, :, pl.ds(off, blk)])
            if mask is not None:
                s = jnp.where(mask, s, -jnp.inf)
            m_old = m_s[h]
            m_new = jnp.maximum(m_old, jnp.max(s, axis=-1, keepdims=True))
            p = jnp.exp2(s - m_new)
            acc_s[h] = jnp.exp2(m_old - m_new) * acc_s[h] + _dot(p.astype(BF16), vals)
            m_s[h] = m_new

    def body(j, carry):
        key_block(pl.multiple_of(j * blk, blk), None)
        return carry

    lax.fori_loop(0, qi, body, 0)
    row = lax.broadcasted_iota(jnp.int32, (blk, blk), 0)
    col = lax.broadcasted_iota(jnp.int32, (blk, blk), 1)
    key_block(pl.multiple_of(qi * blk, blk), col <= row)
    for h in range(N_HEADS):
        acc = acc_s[h]
        o_ref[:, h * HEAD_PAD:(h + 1) * HEAD_PAD] = (acc[:, :KV_RANK] / acc[:, KV_RANK:]).astype(BF16)


def _prompt_attn(q, kt, cb, batch, seq):
    nq = seq // ATT_BLOCK
    return pl.pallas_call(
        _prompt_attn_kernel,
        grid=(batch, nq),
        in_specs=[pl.BlockSpec((ATT_BLOCK, QK_WIDTH), lambda b, i: (b * nq + i, 0)),
                  pl.BlockSpec((N_HEADS, HEAD_PAD, seq), lambda b, i: (0, 0, b)),
                  pl.BlockSpec((seq, KV_RANK), lambda b, i: (b, 0))],
        out_specs=pl.BlockSpec((ATT_BLOCK, N_HEADS * KV_RANK), lambda b, i: (b * nq + i, 0)),
        out_shape=jax.ShapeDtypeStruct((batch * seq, N_HEADS * KV_RANK), BF16),
        scratch_shapes=[pltpu.VMEM((N_HEADS, ATT_BLOCK, 1), F32),
                        pltpu.VMEM((N_HEADS, ATT_BLOCK, 2 * KV_RANK), F32)],
        compiler_params=_params(("parallel", "parallel")),
        name="prompt_attn",
    )(q, kt, cb)


def _sample_attn_kernel(pt_ref, qn_ref, qr_ref, cs_ref, krs_ref, wuk_ref, wukb_ref, gkn_ref,
                        cache_c, cache_r, o_ref,
                        wext, cbuf, rbuf, sems, cb_all, s_all, *, n_pages):
    b = pl.program_id(0)
    last = pl.num_programs(0) - 1
    slot = lax.rem(b, 2)
    past = n_pages * PAGE_SIZE
    n_tiles = past // KV_TILE
    pages_per_tile = KV_TILE // PAGE_SIZE
    nxt = jnp.minimum(b + 1, last)

    def start_page(bb, sl, i):
        page = pt_ref[bb * n_pages + i]
        off = pl.multiple_of(i * PAGE_SIZE, PAGE_SIZE)
        pltpu.make_async_copy(cache_c.at[page], cbuf.at[sl, pl.ds(off, PAGE_SIZE)], sems.at[0, sl]).start()
        pltpu.make_async_copy(cache_r.at[page], rbuf.at[sl, :, pl.ds(off, PAGE_SIZE)], sems.at[1, sl]).start()

    def wait_pages(sl):
        pltpu.make_async_copy(cbuf.at[sl], cbuf.at[sl], sems.at[0, sl]).wait()
        pltpu.make_async_copy(rbuf.at[sl], rbuf.at[sl], sems.at[1, sl]).wait()

    @pl.when(b == 0)
    def _():
        def issue(i, carry):
            start_page(0, 0, i)
            return carry
        lax.fori_loop(0, n_pages, issue, 0)

    nw = N_HEADS * QK_NOPE
    qg = jnp.tile(qn_ref[0], (1, N_HEADS)) * gkn_ref[...]
    head = lax.broadcasted_iota(jnp.int32, (N_HEADS, nw), 0)
    lane = lax.broadcasted_iota(jnp.int32, (N_HEADS, nw), 1)
    qbd = jnp.where((lane >= head * QK_NOPE) & (lane < (head + 1) * QK_NOPE), qg, 0.0)
    qa = jnp.dot(qbd, wuk_ref[...], precision=lax.Precision.HIGHEST,
                 preferred_element_type=F32)
    wext[:nw, :] = wukb_ref[...]
    wext[nw:, :] = jnp.concatenate(
        [qa, jnp.zeros((wext.shape[0] - nw - N_HEADS, KV_RANK), F32)], axis=0).astype(BF16)
    qr = qr_ref[0].astype(BF16)

    def scores(c_blk, krt_blk):
        tk = c_blk.shape[0]
        cb = c_blk.astype(BF16)
        knt = _dot_nt(wext[...], cb)
        kn3 = knt[:nw].reshape(N_HEADS, QK_NOPE, tk)
        r = lax.rsqrt(jnp.sum(kn3 * kn3, axis=1) * (1.0 / QK_NOPE) + EPS)
        s_rope = _dot(qr, krt_blk.astype(BF16))
        return cb, knt[nw:nw + N_HEADS] * r + s_rope

    own = lax.broadcasted_iota(jnp.int32, (N_HEADS, cs_ref.shape[0]), 1) == b
    cb_own, s_own = scores(cs_ref[...], krs_ref[...])
    s_own = jnp.where(own, s_own, -jnp.inf)

    wait_pages(slot)
    for j in range(n_tiles):
        for i in range(pages_per_tile):
            start_page(nxt, 1 - slot, j * pages_per_tile + i)
        keys = slice(j * KV_TILE, (j + 1) * KV_TILE)
        cb, s = scores(cbuf[slot, keys, :], rbuf[slot, :, keys])
        cb_all[keys, :] = cb
        s_all[:, keys] = s

    s = s_all[...]
    m = jnp.maximum(jnp.max(s, axis=-1, keepdims=True), jnp.max(s_own, axis=-1, keepdims=True))
    p = jnp.exp2(s - m)
    p_own = jnp.exp2(s_own - m)
    l = jnp.sum(p, axis=-1, keepdims=True) + jnp.sum(p_own, axis=-1, keepdims=True)
    acc = _dot(p.astype(BF16), cb_all[...]) + _dot(p_own.astype(BF16), cb_own)
    o_ref[0] = acc / l

    @pl.when(b == last)
    def _():
        wait_pages(1 - slot)


def _sample_attn(page_table, qn, qr, cs, krs_t, w, cache_c, cache_rt):
    n_samp, n_pages = page_table.shape
    past = n_pages * PAGE_SIZE
    wext_rows = N_HEADS * QK_NOPE + 2 * SUBLANES
    grid_spec = pltpu.PrefetchScalarGridSpec(
        num_scalar_prefetch=1,
        grid=(n_samp,),
        in_specs=[pl.BlockSpec((1, N_HEADS, QK_NOPE), lambda b, pt: (b, 0, 0)),
                  pl.BlockSpec((1, N_HEADS, QK_ROPE), lambda b, pt: (b, 0, 0)),
                  pl.BlockSpec(cs.shape, lambda b, pt: (0, 0)),
                  pl.BlockSpec(krs_t.shape, lambda b, pt: (0, 0)),
                  pl.BlockSpec(w['w_ukt_f32'].shape, lambda b, pt: (0, 0)),
                  pl.BlockSpec(w['w_ukt'].shape, lambda b, pt: (0, 0)),
                  pl.BlockSpec(w['g_kn_row'].shape, lambda b, pt: (0, 0)),
                  pl.BlockSpec(memory_space=pl.ANY),
                  pl.BlockSpec(memory_space=pl.ANY)],
        out_specs=pl.BlockSpec((1, N_HEADS, KV_RANK), lambda b, pt: (b, 0, 0)),
        scratch_shapes=[pltpu.VMEM((wext_rows, KV_RANK), BF16),
                        pltpu.VMEM((2, past, KV_RANK), F32),
                        pltpu.VMEM((2, QK_ROPE, past), F32),
                        pltpu.SemaphoreType.DMA((2, 2)),
                        pltpu.VMEM((past, KV_RANK), BF16),
                        pltpu.VMEM((N_HEADS, past), F32)])
    return pl.pallas_call(
        functools.partial(_sample_attn_kernel, n_pages=n_pages),
        grid_spec=grid_spec,
        out_shape=jax.ShapeDtypeStruct((n_samp, N_HEADS, KV_RANK), F32),
        compiler_params=_params(("arbitrary",)),
        name="sample_attn",
    )(page_table.reshape(-1), qn, qr, cs, krs_t, w['w_ukt_f32'], w['w_ukt'], w['g_kn_row'],
      cache_c, cache_rt)


def _mix_proj_kernel(x_ref, o_ref, u_ref, v_ref, wuv_ref, ws_ref, bias_ref, wo_ref, g_ref,
                     h_ref, hn_ref, *, single_token):
    tm = x_ref.shape[0]
    attn = _dot(o_ref[...], wuv_ref[...])
    v = v_ref[...]
    if single_token:
        mixed = v * ws_ref[...] + bias_ref[...]
    else:
        vb = v.astype(BF16)
        row = lax.broadcasted_iota(jnp.int32, (CHUNK, CHUNK), 0)
        col = lax.broadcasted_iota(jnp.int32, (CHUNK, CHUNK), 1)
        w_tril = [jnp.where(col <= row, ws_ref[g], 0.0).astype(BF16) for g in range(N_CG)]
        low_half = lax.broadcasted_iota(jnp.int32, (CHUNK, LANES), 1) < CG_DIM
        chunks = []
        for c in range(tm // CHUNK):
            cols = []
            for j in range(CG_WIDTH // LANES):
                vp = vb[c * CHUNK:(c + 1) * CHUNK, j * LANES:(j + 1) * LANES]
                cols.append(jnp.where(low_half, _dot(w_tril[2 * j], vp), _dot(w_tril[2 * j + 1], vp)))
            chunks.append(jnp.concatenate(cols, axis=1) + bias_ref[...])
        mixed = jnp.concatenate(chunks, axis=0)
    sg = u_ref[...] * mixed
    width = attn.shape[1]
    h = x_ref[...] + _dot(attn.astype(BF16), wo_ref[:width, :]) + _dot(sg.astype(BF16), wo_ref[width:, :])
    h_ref[...] = h
    hn_ref[...] = (_rms(h) * g_ref[...]).astype(BF16)


def _mix_proj(x, o_lat, u, v, w, tm, single_token):
    n = x.shape[0]
    tok = lambda width: pl.BlockSpec((tm, width), lambda i: (i, 0))
    ws, bias = (w['ws_diag'], w['bias_first']) if single_token else (w['w_s'], w['bias_tab'])
    weights = [w['w_uv_bd'], ws, bias, w['w_o'], w['g_ffn']]
    return pl.pallas_call(
        functools.partial(_mix_proj_kernel, single_token=single_token),
        grid=(n // tm,),
        in_specs=[tok(D_MODEL), tok(N_HEADS * KV_RANK), tok(CG_WIDTH), tok(CG_WIDTH)]
                 + [_full(a.shape) for a in weights],
        out_specs=[tok(D_MODEL), tok(D_MODEL)],
        out_shape=[jax.ShapeDtypeStruct((n, D_MODEL), F32),
                   jax.ShapeDtypeStruct((n, D_MODEL), BF16)],
        compiler_params=_params(("parallel",)),
        name="mix_proj_single" if single_token else "mix_proj",
    )(x, o_lat, u, v, *weights)


def _conv_ffn_kernel(*refs, single_token):
    if single_token:
        h_ref, hn_ref, win_ref, cw_ref, cb_ref, wout_ref, h0_ref, h1_ref, out_ref, a_ref, act_s = refs
    else:
        h_ref, hn_ref, win_ref, cw_ref, cb_ref, wout_ref, out_ref, last_ref, act_s, carry = refs
        t = pl.program_id(1)

        @pl.when(t == 0)
        def _():
            carry[...] = jnp.zeros(carry.shape, F32)
    tm = hn_ref.shape[0]
    hn = hn_ref[...]

    def conv(cols):
        a = _dot(hn, win_ref[:, cols])
        if single_token:
            a2, a1 = h0_ref[:, cols], h1_ref[:, cols]
            a_ref[:, cols] = a
        else:
            ext = jnp.concatenate([carry[:, cols], a], axis=0)
            a1 = ext[SUBLANES - 1:SUBLANES - 1 + tm]
            a2 = ext[SUBLANES - 2:SUBLANES - 2 + tm]
            tail = a[tm - SUBLANES:]
            carry[:, cols] = tail
            last_ref[0, :, cols] = tail[SUBLANES - (CONV_W - 1):]
        return cb_ref[:, cols] + cw_ref[0:1, cols] * a2 + cw_ref[1:2, cols] * a1 + cw_ref[2:3, cols] * a

    for j in range(D_FF // FF_TILE):
        gate = slice(j * FF_TILE, (j + 1) * FF_TILE)
        up = slice(D_FF + j * FF_TILE, D_FF + (j + 1) * FF_TILE)
        act_s[:, gate] = (jax.nn.silu(conv(gate)) * conv(up)).astype(BF16)
    out_ref[...] = h_ref[...] + _dot(act_s[...], wout_ref[...])


def _conv_ffn(h, hn, w, batch, seq, tm, hist=None):
    single_token = hist is not None
    nt = seq // tm
    tok = lambda width: pl.BlockSpec((tm, width), lambda b, t: (b * nt + t, 0))
    resident = lambda a: pl.BlockSpec(a.shape, lambda b, t: (0, 0), pipeline_mode=pl.Buffered(1))
    weights = [w['w_ff_in'], w['conv_w'], w['conv_b'], w['w_ff_out']]
    in_specs = [tok(D_MODEL), tok(D_MODEL)] + [resident(a) for a in weights]
    args = [h, hn] + weights
    scratch = [pltpu.VMEM((tm, D_FF), BF16)]
    if single_token:
        in_specs += [resident(hist[0]), resident(hist[1])]
        args += list(hist)
        last_spec = pl.BlockSpec((tm, 2 * D_FF), lambda b, t: (0, 0))
        last_shape = jax.ShapeDtypeStruct((tm, 2 * D_FF), F32)
    else:
        last_spec = pl.BlockSpec((1, CONV_W - 1, 2 * D_FF), lambda b, t: (b * nt + t, 0, 0))
        last_shape = jax.ShapeDtypeStruct((batch * nt, CONV_W - 1, 2 * D_FF), F32)
        scratch.append(pltpu.VMEM((SUBLANES, 2 * D_FF), F32))
    return pl.pallas_call(
        functools.partial(_conv_ffn_kernel, single_token=single_token),
        grid=(batch, nt),
        in_specs=in_specs,
        out_specs=[tok(D_MODEL), last_spec],
        out_shape=[jax.ShapeDtypeStruct((batch * seq, D_MODEL), F32), last_shape],
        scratch_shapes=scratch,
        compiler_params=_params(("arbitrary", "arbitrary")),
        name="conv_ffn_single" if single_token else "conv_ffn",
    )(*args)


def _ple_kernel(h_ref, p_ref, g_ref, wgate_ref, wproj_ref, gpost_ref, y_ref):
    h = h_ref[...]
    gate = jax.nn.sigmoid(_dot((_rms(h) * g_ref[...]).astype(BF16), wgate_ref[...]))
    e = _rms(_dot(p_ref[...].astype(BF16), wproj_ref[...])) * gpost_ref[...]
    y_ref[...] = h + gate * e


def _ple(h, p, w, tm):
    n = h.shape[0]
    tok = lambda width: pl.BlockSpec((tm, width), lambda i: (i, 0))
    weights = [w['g_ple'], w['w_gate'], w['w_proj'], w['g_post']]
    return pl.pallas_call(
        _ple_kernel,
        grid=(n // tm,),
        in_specs=[tok(D_MODEL), tok(PLE_DIM)] + [_full(a.shape) for a in weights],
        out_specs=tok(D_MODEL),
        out_shape=jax.ShapeDtypeStruct((n, D_MODEL), F32),
        compiler_params=_params(("parallel",)),
        name="ple",
    )(h, p, *weights)


def _rope_tables(pos):
    inv = ROPE_THETA ** (-jnp.arange(ROPE_HALF, dtype=F32) * (2.0 / QK_ROPE))
    ang = pos.astype(F32)[:, None] * inv[None, :]
    cos, sin = jnp.cos(ang), jnp.sin(ang)
    n = pos.shape[0]
    ones = jnp.ones((n, QK_NOPE), F32)
    zq = jnp.zeros((n, QK_NOPE), F32)
    zp = jnp.zeros((n, HEAD_PAD - QK_NOPE - QK_ROPE), F32)
    ca = jnp.concatenate([ones, cos, cos, zp], axis=1)
    sb = jnp.concatenate([zq, -sin, sin, zp], axis=1)
    return ca, sb, cos.T, sin.T


def _prep_weights(i, attn_norm_w, w_in, q_norm_w, w_uq, q_nope_norm_w, q_rope_norm_w, kv_norm_w,
                  k_rope_norm_w, w_uk, k_nope_norm_w, w_uv, w_s, b_s, w_o, ffn_norm_w, w_ff_in,
                  conv_w, conv_b, w_ff_out, ple_norm_w, w_ple_gate, w_ple_proj, ple_post_norm_w):
    row = lambda g: g[i][None, :]
    swap = lambda a: jnp.concatenate([a[..., ROPE_HALF:], a[..., :ROPE_HALF]], axis=-1)
    win = w_in[i]
    wq = w_uq[i].reshape(Q_RANK, N_HEADS, QK_NOPE + QK_ROPE)
    nope, rope = wq[..., :QK_NOPE], wq[..., QK_NOPE:]
    pad = HEAD_PAD - QK_NOPE - QK_ROPE
    zeros = lambda width: jnp.zeros((Q_RANK, N_HEADS, width), F32)
    w_qa = jnp.concatenate([nope, rope, zeros(pad)], axis=-1).reshape(Q_RANK, QK_WIDTH)
    w_qb = jnp.concatenate([zeros(QK_NOPE), swap(rope), zeros(pad)], axis=-1).reshape(Q_RANK, QK_WIDTH)
    gn, gr = q_nope_norm_w[i], q_rope_norm_w[i]
    ga = jnp.tile(jnp.concatenate([gn, gr, jnp.zeros((pad,), F32)]), N_HEADS)[None, :]
    gb = jnp.tile(jnp.concatenate([jnp.zeros((QK_NOPE,), F32), swap(gr), jnp.zeros((pad,), F32)]),
                  N_HEADS)[None, :]
    lane = jnp.arange(2 * HEAD_PAD)
    slot, off = lane // HEAD_PAD, lane % HEAD_PAD
    grp = jnp.where(off < QK_NOPE, 0, jnp.where(off < QK_NOPE + QK_ROPE, 1, 2))
    same = (slot[:, None] == slot[None, :]) & (grp[:, None] == grp[None, :]) & (grp[:, None] < 2)
    gq = jnp.where(same, jnp.where(grp[:, None] == 0, 1.0 / QK_NOPE, 1.0 / QK_ROPE), 0.0)
    w_ukt = w_uk[i].reshape(KV_RANK, N_HEADS * QK_NOPE).T
    eye = jnp.eye(N_HEADS, dtype=F32)
    w_uv_bd = jnp.einsum('rhd,hg->hrgd', w_uv[i], eye).reshape(N_HEADS * KV_RANK, N_HEADS * V_DIM)
    return {
        'g_attn': row(attn_norm_w),
        'w_in': jnp.concatenate([win[:, :OFF_KR], win[:, OFF_U:]], axis=1).astype(BF16),
        'g_q': row(q_norm_w), 'w_qa': w_qa.astype(BF16), 'w_qb': w_qb.astype(BF16),
        'gq': gq.astype(BF16), 'ga': ga, 'gb': gb,
        'g_kv': row(kv_norm_w),
        'w_ukt': w_ukt.astype(BF16), 'w_ukt_f32': w_ukt,
        'g_kn': jnp.broadcast_to(jnp.tile(k_nope_norm_w[i], N_HEADS)[:, None], (N_HEADS * QK_NOPE, LANES)),
        'g_kn_row': jnp.tile(k_nope_norm_w[i], N_HEADS)[None, :],
        'w_krt': win[:, OFF_KR:OFF_U].T.astype(BF16),
        'g_kr': jnp.broadcast_to(k_rope_norm_w[i][:, None], (QK_ROPE, LANES)),
        'w_uv_bd': w_uv_bd.astype(BF16),
        'w_s': w_s[i],
        'bias_tab': jnp.repeat(b_s[i].T, CG_DIM, axis=1),
        'ws_diag': jnp.repeat(w_s[i][:, 0, 0], CG_DIM)[None, :],
        'bias_first': jnp.repeat(b_s[i][:, 0], CG_DIM)[None, :],
        'w_o': w_o[i].astype(BF16), 'g_ffn': row(ffn_norm_w),
        'w_ff_in': w_ff_in[i].astype(BF16), 'conv_w': conv_w[i], 'conv_b': row(conv_b),
        'w_ff_out': w_ff_out[i].astype(BF16),
        'g_ple': row(ple_norm_w), 'w_gate': w_ple_gate[i].astype(BF16),
        'w_proj': w_ple_proj[i].astype(BF16), 'g_post': row(ple_post_norm_w),
    }


def kernel(x_prompt, x_sample, cache_ckv, cache_krope, state_conv, page_table, p_prompt, p_sample,
           attn_norm_w, w_in, q_norm_w, w_uq, q_nope_norm_w, q_rope_norm_w, kv_norm_w, k_rope_norm_w,
           w_uk, k_nope_norm_w, w_uv, w_s, b_s, w_o, ffn_norm_w, w_ff_in, conv_w, conv_b, w_ff_out,
           ple_norm_w, w_ple_gate, w_ple_proj, ple_post_norm_w):
    batch, seq, _ = x_prompt.shape
    n_samp, seq_s, _ = x_sample.shape
    depth = w_in.shape[0]
    assert depth == 1 and seq_s == 1 and seq % TOK_TILE == 0 and seq % CHUNK == 0
    past_len = page_table.shape[1] * PAGE_SIZE
    assert past_len % KV_TILE == 0
    w = _prep_weights(0, attn_norm_w, w_in, q_norm_w, w_uq, q_nope_norm_w, q_rope_norm_w, kv_norm_w,
                      k_rope_norm_w, w_uk, k_nope_norm_w, w_uv, w_s, b_s, w_o, ffn_norm_w, w_ff_in,
                      conv_w, conv_b, w_ff_out, ple_norm_w, w_ple_gate, w_ple_proj, ple_post_norm_w)

    n_tok = batch * seq
    tabs = _rope_tables(jnp.arange(seq, dtype=jnp.int32))
    xp = x_prompt.reshape(n_tok, D_MODEL)
    q, kt, ckv, cb, krt, u, v = _in_proj(xp, tabs, w, TOK_TILE)
    o_lat = _prompt_attn(q, kt, cb, batch, seq)
    h, hn = _mix_proj(xp, o_lat, u, v, w, TOK_TILE, single_token=False)
    h2, a_last = _conv_ffn(h, hn, w, batch, seq, TOK_TILE)
    y_prompt = _ple(h2, p_prompt[0].reshape(n_tok, PLE_DIM), w, TOK_TILE).reshape(batch, seq, D_MODEL)
    n_keep = (seq - 1) % CHUNK + 1
    new_ckv_p = ckv.reshape(1, batch, seq, KV_RANK)
    new_kr_p = krt.T.reshape(1, batch, seq, QK_ROPE)
    new_v_p = v.reshape(batch, seq, CG_WIDTH)[:, seq - n_keep:][None]
    new_conv_p = a_last.reshape(batch, seq // TOK_TILE, CONV_W - 1, 2 * D_FF)[:, -1][None]

    tabs_s = _rope_tables(jnp.full((n_samp,), past_len, dtype=jnp.int32))
    xs = x_sample.reshape(n_samp, D_MODEL)
    q_s, _, ckv_s, _, krt_s, u_s, v_s = _in_proj(xs, tabs_s, w, n_samp)
    q3 = q_s.astype(F32).reshape(n_samp, N_HEADS, HEAD_PAD)
    kr_s = krt_s.T
    cache_rt = jnp.swapaxes(cache_krope[0], 1, 2)
    o_s = _sample_attn(page_table, q3[:, :, :QK_NOPE], q3[:, :, QK_NOPE:QK_NOPE + QK_ROPE],
                       ckv_s, krt_s, w, cache_ckv[0], cache_rt)
    o_s = o_s.reshape(n_samp, N_HEADS * KV_RANK).astype(BF16)
    h_s, hn_s = _mix_proj(xs, o_s, u_s, v_s, w, n_samp, single_token=True)
    hist = (state_conv[0, :, 0, :], state_conv[0, :, 1, :])
    h2_s, a_s = _conv_ffn(h_s, hn_s, w, 1, n_samp, n_samp, hist=hist)
    y_sample = _ple(h2_s, p_sample[0].reshape(n_samp, PLE_DIM), w, n_samp).reshape(n_samp, 1, D_MODEL)
    new_conv_s = jnp.stack([hist[1], a_s], axis=1)[None]

    return (y_prompt, y_sample, new_ckv_p, new_kr_p,
            ckv_s.reshape(1, n_samp, 1, KV_RANK), kr_s.reshape(1, n_samp, 1, QK_ROPE),
            new_v_p, v_s.reshape(1, n_samp, 1, CG_WIDTH), new_conv_p, new_conv_s)
```

```python
import functools

import jax
import jax.numpy as jnp
from jax import lax
from jax.experimental import pallas as pl
from jax.experimental.pallas import tpu as pltpu

F32 = jnp.float32
BF16 = jnp.bfloat16

D_MODEL = 1024
N_HEADS = 8
QK_NOPE = 64
QK_ROPE = 32
ROPE_HALF = QK_ROPE // 2
V_DIM = 64
Q_RANK = 256
KV_RANK = 128
CHUNK = 128
N_CG = 8
CG_WIDTH = 512
CG_DIM = 64
D_FF = 2816
CONV_W = 3
PLE_DIM = 256
PAGE_SIZE = 128
ROPE_THETA = 10000.0
EPS = 1e-6
SCALE = (QK_NOPE + QK_ROPE) ** -0.5
Q_SCALE = SCALE * 1.4426950408889634
OFF_KV = Q_RANK
OFF_KR = OFF_KV + KV_RANK
OFF_U = OFF_KR + QK_ROPE
OFF_V = OFF_U + CG_WIDTH
IN_WIDTH = OFF_V + CG_WIDTH
HEAD_PAD = 128
QK_WIDTH = N_HEADS * HEAD_PAD
Z_WIDTH = Q_RANK + KV_RANK + 2 * CG_WIDTH

LANES = 128
SUBLANES = 8
VMEM_LIMIT_BYTES = 56 * 1024 * 1024

TOK_TILE = 512
ATT_BLOCK = 1024
KEY_BLOCK = 512
ONES_ROWS = 16
FF_TILE = 256
KV_TILE = 2048

NT_DIMS = (((1,), (1,)), ((), ()))


def _params(semantics):
    return pltpu.CompilerParams(dimension_semantics=semantics,
                                vmem_limit_bytes=VMEM_LIMIT_BYTES)


def _rms(x):
    return x * lax.rsqrt(jnp.mean(x * x, axis=-1, keepdims=True) + EPS)


def _dot(a, b):
    return jnp.dot(a, b, preferred_element_type=F32)


def _dot_nt(a, b):
    return lax.dot_general(a, b, NT_DIMS, preferred_element_type=F32)


def _full(shape):
    zeros = (0,) * len(shape)
    return pl.BlockSpec(shape, lambda *_: zeros)


def _in_proj_kernel(x_ref, ca_ref, sb_ref, cost_ref, sint_ref,
                    g_attn_ref, w_in_ref, g_q_ref, w_qt_ref, g_qn_ref, g_qr_ref,
                    g_kv_ref, w_kvt_ref, g_kvt_ref, w_uk_ref, gm_ref, ga_ref, gb_ref,
                    qt_ref, k_ref, ckv_ref, vt_ref, kr_ref, u_ref, v_ref):
    tm = x_ref.shape[0]
    reps = tm // LANES
    hb = (_rms(x_ref[...]) * g_attn_ref[...]).astype(BF16)
    z = _dot(hb, w_in_ref[...])
    u_ref[...] = jax.nn.gelu(z[:, OFF_KR:OFF_KR + CG_WIDTH])
    v_ref[...] = jax.nn.gelu(z[:, OFF_KR + CG_WIDTH:OFF_KR + 2 * CG_WIDTH])
    ckv = _rms(z[:, OFF_KV:OFF_KR]) * g_kv_ref[...]
    ckv_ref[...] = ckv

    kra = z[:, Z_WIDTH:Z_WIDTH + HEAD_PAD]
    krb = z[:, Z_WIDTH + HEAD_PAD:]
    ka = _dot(ckv.astype(BF16), w_uk_ref[...]) + jnp.tile(kra, (1, N_HEADS))
    sq = ka * ka
    sq_hi = sq.astype(BF16)
    sq_lo = (sq - sq_hi.astype(F32)).astype(BF16)
    gm = gm_ref[...]
    pair = 2 * HEAD_PAD
    ms = jnp.concatenate(
        [_dot(sq_hi[:, i * pair:(i + 1) * pair], gm) + _dot(sq_lo[:, i * pair:(i + 1) * pair], gm)
         for i in range(N_HEADS // 2)], axis=1)
    ca = jnp.tile(ca_ref[...], (1, N_HEADS))
    sb = jnp.tile(sb_ref[...], (1, N_HEADS))
    k = lax.rsqrt(ms + EPS) * (ka * ga_ref[...] * ca + jnp.tile(krb, (1, N_HEADS)) * gb_ref[...] * sb)
    k_ref[...] = k.astype(BF16)
    kr_ref[...] = k[:, :HEAD_PAD]

    ct = _dot_nt(w_kvt_ref[...], hb)
    ct = ct * lax.rsqrt(jnp.mean(ct * ct, axis=0, keepdims=True) + EPS)
    vt_ref[...] = (ct * jnp.tile(g_kvt_ref[...], (1, reps))).astype(BF16)

    qln = (_rms(z[:, :Q_RANK]) * g_q_ref[...]).astype(BF16)
    qt = _dot_nt(w_qt_ref[...], qln).reshape(N_HEADS, HEAD_PAD, tm)
    nope = qt[:, :QK_NOPE]
    rope = qt[:, QK_NOPE:QK_NOPE + QK_ROPE]
    g_qn = jnp.tile(g_qn_ref[...], (1, reps))[None]
    g_qr = jnp.tile(g_qr_ref[...], (1, reps))[None]
    nope = nope * lax.rsqrt(jnp.mean(nope * nope, axis=1, keepdims=True) + EPS) * (g_qn * Q_SCALE)
    rope = rope * lax.rsqrt(jnp.mean(rope * rope, axis=1, keepdims=True) + EPS) * (g_qr * Q_SCALE)
    x1, x2 = rope[:, :ROPE_HALF], rope[:, ROPE_HALF:]
    cos, sin = cost_ref[...][None], sint_ref[...][None]
    pad = jnp.zeros((N_HEADS, HEAD_PAD - QK_NOPE - QK_ROPE, tm), F32)
    qt_ref[...] = jnp.concatenate([nope, x1 * cos - x2 * sin, x1 * sin + x2 * cos, pad], axis=1).astype(BF16)


def _in_proj(x, tabs, w, tm):
    n = x.shape[0]
    ca, sb, cost, sint = tabs
    pos_tiles = ca.shape[0] // tm
    tok = lambda width: pl.BlockSpec((tm, width), lambda i: (i, 0))
    tok_t = lambda rows: pl.BlockSpec((rows, tm), lambda i: (0, i))
    pos = pl.BlockSpec((tm, HEAD_PAD), lambda i: (i % pos_tiles, 0))
    pos_t = pl.BlockSpec((ROPE_HALF, tm), lambda i: (0, i % pos_tiles))
    weights = [w['g_attn'], w['w_in'], w['g_q'], w['w_qt'], w['g_qn'], w['g_qr'],
               w['g_kv'], w['w_kvt'], w['g_kvt'], w['w_uk'], w['gm'], w['ga'], w['gb']]
    return pl.pallas_call(
        _in_proj_kernel,
        grid=(n // tm,),
        in_specs=[tok(D_MODEL), pos, pos, pos_t, pos_t]
                 + [_full(a.shape) for a in weights],
        out_specs=[pl.BlockSpec((N_HEADS, HEAD_PAD, tm), lambda i: (0, 0, i)),
                   tok(QK_WIDTH), tok(KV_RANK), tok_t(KV_RANK), tok(HEAD_PAD),
                   tok(CG_WIDTH), tok(CG_WIDTH)],
        out_shape=[jax.ShapeDtypeStruct((N_HEADS, HEAD_PAD, n), BF16),
                   jax.ShapeDtypeStruct((n, QK_WIDTH), BF16),
                   jax.ShapeDtypeStruct((n, KV_RANK), F32),
                   jax.ShapeDtypeStruct((KV_RANK, n), BF16),
                   jax.ShapeDtypeStruct((n, HEAD_PAD), F32),
                   jax.ShapeDtypeStruct((n, CG_WIDTH), F32),
                   jax.ShapeDtypeStruct((n, CG_WIDTH), F32)],
        compiler_params=_params(("parallel",)),
        name="in_proj",
    )(x, ca, sb, cost, sint, *weights)


def _prompt_attn_kernel(qt_ref, k_ref, vt_ref, o_ref, m_s, acc_s):
    qi = pl.program_id(1)
    kb = KEY_BLOCK
    per_q = ATT_BLOCK // kb
    m_s[...] = jnp.full(m_s.shape, -jnp.inf, F32)
    acc_s[...] = jnp.zeros(acc_s.shape, F32)
    ones = jnp.ones((ONES_ROWS, kb), BF16)

    def key_block(off, first_q, masked):
        cols = slice(first_q, ATT_BLOCK)
        vals = jnp.concatenate([vt_ref[:, pl.ds(off, kb)], ones], axis=0)
        if masked:
            key = lax.broadcasted_iota(jnp.int32, (kb, ATT_BLOCK - first_q), 0)
            qry = lax.broadcasted_iota(jnp.int32, (kb, ATT_BLOCK - first_q), 1)
            visible = key <= qry
        for h in range(N_HEADS):
            s = _dot(k_ref[pl.ds(off, kb), h * HEAD_PAD:(h + 1) * HEAD_PAD], qt_ref[h, :, cols])
            if masked:
                s = jnp.where(visible, s, -jnp.inf)
            m_old = m_s[h, :, cols]
            m_new = jnp.maximum(m_old, jnp.max(s, axis=0, keepdims=True))
            p = jnp.exp2(s - m_new)
            acc_s[h, :, cols] = jnp.exp2(m_old - m_new) * acc_s[h, :, cols] + _dot(vals, p.astype(BF16))
            m_s[h, :, cols] = m_new

    def body(j, carry):
        key_block(pl.multiple_of(j * kb, kb), 0, False)
        return carry

    lax.fori_loop(0, qi * per_q, body, 0)
    for d in range(per_q):
        key_block(pl.multiple_of(qi * ATT_BLOCK + d * kb, kb), d * kb, True)
    for h in range(N_HEADS):
        acc = acc_s[h]
        o_t = acc[:KV_RANK] / acc[KV_RANK:KV_RANK + 1]
        o_ref[:, h * HEAD_PAD:(h + 1) * HEAD_PAD] = o_t.T.astype(BF16)


def _prompt_attn(qt, k, vt, batch, seq):
    nq = seq // ATT_BLOCK
    return pl.pallas_call(
        _prompt_attn_kernel,
        grid=(batch, nq),
        in_specs=[pl.BlockSpec((N_HEADS, HEAD_PAD, ATT_BLOCK), lambda b, i: (0, 0, b * nq + i)),
                  pl.BlockSpec((seq, QK_WIDTH), lambda b, i: (b, 0)),
                  pl.BlockSpec((KV_RANK, seq), lambda b, i: (0, b))],
        out_specs=pl.BlockSpec((ATT_BLOCK, N_HEADS * KV_RANK), lambda b, i: (b * nq + i, 0)),
        out_shape=jax.ShapeDtypeStruct((batch * seq, N_HEADS * KV_RANK), BF16),
        scratch_shapes=[pltpu.VMEM((N_HEADS, 1, ATT_BLOCK), F32),
                        pltpu.VMEM((N_HEADS, KV_RANK + ONES_ROWS, ATT_BLOCK), F32)],
        compiler_params=_params(("parallel", "parallel")),
        name="prompt_attn",
    )(qt, k, vt)


def _absorb_q_kernel(qn_ref, g_ref, wukt_ref, qa_ref):
    qg = qn_ref[...] * g_ref[...]
    lane = lax.broadcasted_iota(jnp.int32, qg.shape, 1)
    for h in range(N_HEADS):
        q_h = jnp.where((lane >= h * QK_NOPE) & (lane < (h + 1) * QK_NOPE), qg, 0.0)
        qa_ref[:, h * KV_RANK:(h + 1) * KV_RANK] = jnp.dot(
            q_h, wukt_ref[...], precision=lax.Precision.HIGHEST, preferred_element_type=F32)


def _absorb_q(qn, w):
    n_samp = qn.shape[0]
    args = [qn, w['g_kn_row'], w['w_ukt_f32']]
    return pl.pallas_call(
        _absorb_q_kernel,
        grid=(1,),
        in_specs=[_full(a.shape) for a in args],
        out_specs=_full((n_samp, N_HEADS * KV_RANK)),
        out_shape=jax.ShapeDtypeStruct((n_samp, N_HEADS * KV_RANK), F32),
        compiler_params=_params(("arbitrary",)),
        name="absorb_q",
    )(*args)


def _sample_attn_kernel(pt_ref, q_ref, qa_ref, kown_ref, cown_ref, wukb_ref,
                        cache_c, cache_r, o_ref,
                        wext, cbuf, rbuf, sems, cb_all, s_all, *, n_pages):
    b = pl.program_id(0)
    last = pl.num_programs(0) - 1
    slot = lax.rem(b, 2)
    past = n_pages * PAGE_SIZE
    n_tiles = past // KV_TILE
    pages_per_tile = KV_TILE // PAGE_SIZE
    nw = N_HEADS * QK_NOPE
    nxt = jnp.minimum(b + 1, last)

    def start_page(bb, sl, i):
        page = pt_ref[bb * n_pages + i]
        off = pl.multiple_of(i * PAGE_SIZE, PAGE_SIZE)
        pltpu.make_async_copy(cache_c.at[page], cbuf.at[sl, pl.ds(off, PAGE_SIZE)], sems.at[0, sl]).start()
        pltpu.make_async_copy(cache_r.at[page], rbuf.at[sl, :, pl.ds(off, PAGE_SIZE)], sems.at[1, sl]).start()

    def wait_pages(sl):
        pltpu.make_async_copy(cbuf.at[sl], cbuf.at[sl], sems.at[0, sl]).wait()
        pltpu.make_async_copy(rbuf.at[sl], rbuf.at[sl], sems.at[1, sl]).wait()

    @pl.when(b == 0)
    def _():
        def issue(i, carry):
            start_page(0, 0, i)
            return carry
        lax.fori_loop(0, n_pages, issue, 0)
        wext[:nw, :] = wukb_ref[...]

    wext[nw:, :] = jnp.concatenate(
        [qa_ref[0], jnp.zeros((wext.shape[0] - nw - N_HEADS, KV_RANK), F32)], axis=0).astype(BF16)
    qr = q_ref[0][:, QK_NOPE:QK_NOPE + QK_ROPE].astype(BF16)

    def scores(c_blk, krt_blk):
        tk = c_blk.shape[0]
        cb = c_blk.astype(BF16)
        knt = _dot_nt(wext[...], cb)
        kn3 = knt[:nw].reshape(N_HEADS, QK_NOPE, tk)
        r = lax.rsqrt(jnp.sum(kn3 * kn3, axis=1) * (1.0 / QK_NOPE) + EPS)
        s_rope = _dot(qr, krt_blk.astype(BF16))
        return cb, knt[nw:nw + N_HEADS] * r + s_rope

    s_own = jnp.sum(q_ref[0] * kown_ref[0].astype(F32), axis=-1, keepdims=True)

    wait_pages(slot)
    for j in range(n_tiles):
        for i in range(pages_per_tile):
            start_page(nxt, 1 - slot, j * pages_per_tile + i)
        keys = slice(j * KV_TILE, (j + 1) * KV_TILE)
        cb, s = scores(cbuf[slot, keys, :], rbuf[slot, :, keys])
        cb_all[keys, :] = cb
        s_all[:, keys] = s

    s = s_all[...]
    m = jnp.maximum(jnp.max(s, axis=-1, keepdims=True), s_own)
    p = jnp.exp2(s - m)
    p_own = jnp.exp2(s_own - m)
    l = jnp.sum(p, axis=-1, keepdims=True) + p_own
    round_bf16 = lambda a: a.astype(BF16).astype(F32)
    acc = _dot(p.astype(BF16), cb_all[...]) + round_bf16(p_own) * round_bf16(cown_ref[0])
    o_ref[0] = acc / l

    @pl.when(b == last)
    def _():
        wait_pages(1 - slot)


def _sample_attn(page_table, q, qa, k_own, c_own, w, cache_c, cache_rt):
    n_samp, n_pages = page_table.shape
    past = n_pages * PAGE_SIZE
    wext_rows = N_HEADS * QK_NOPE + 2 * SUBLANES
    per_sample = lambda a: pl.BlockSpec((1,) + a.shape[1:], lambda b, pt: (b, 0, 0))
    grid_spec = pltpu.PrefetchScalarGridSpec(
        num_scalar_prefetch=1,
        grid=(n_samp,),
        in_specs=[per_sample(q), per_sample(qa), per_sample(k_own), per_sample(c_own),
                  pl.BlockSpec(w['w_ukt'].shape, lambda b, pt: (0, 0)),
                  pl.BlockSpec(memory_space=pl.ANY),
                  pl.BlockSpec(memory_space=pl.ANY)],
        out_specs=pl.BlockSpec((1, N_HEADS, KV_RANK), lambda b, pt: (b, 0, 0)),
        scratch_shapes=[pltpu.VMEM((wext_rows, KV_RANK), BF16),
                        pltpu.VMEM((2, past, KV_RANK), F32),
                        pltpu.VMEM((2, QK_ROPE, past), F32),
                        pltpu.SemaphoreType.DMA((2, 2)),
                        pltpu.VMEM((past, KV_RANK), BF16),
                        pltpu.VMEM((N_HEADS, past), F32)])
    return pl.pallas_call(
        functools.partial(_sample_attn_kernel, n_pages=n_pages),
        grid_spec=grid_spec,
        out_shape=jax.ShapeDtypeStruct((n_samp, N_HEADS, KV_RANK), F32),
        compiler_params=_params(("arbitrary",)),
        name="sample_attn",
    )(page_table.reshape(-1), q, qa, k_own, c_own, w['w_ukt'], cache_c, cache_rt)


def _mix_proj_kernel(x_ref, o_ref, u_ref, v_ref, wuv_ref, ws_ref, bias_ref, wo_ref, g_ref,
                     h_ref, hn_ref, *, single_token):
    tm = x_ref.shape[0]
    attn = _dot(o_ref[...], wuv_ref[...])
    v = v_ref[...]
    if single_token:
        mixed = v * ws_ref[...] + bias_ref[...]
    else:
        vb = v.astype(BF16)
        row = lax.broadcasted_iota(jnp.int32, (CHUNK, CHUNK), 0)
        col = lax.broadcasted_iota(jnp.int32, (CHUNK, CHUNK), 1)
        w_tril = [jnp.where(col <= row, ws_ref[g], 0.0).astype(BF16) for g in range(N_CG)]
        low_half = lax.broadcasted_iota(jnp.int32, (CHUNK, LANES), 1) < CG_DIM
        chunks = []
        for c in range(tm // CHUNK):
            cols = []
            for j in range(CG_WIDTH // LANES):
                vp = vb[c * CHUNK:(c + 1) * CHUNK, j * LANES:(j + 1) * LANES]
                cols.append(jnp.where(low_half, _dot(w_tril[2 * j], vp), _dot(w_tril[2 * j + 1], vp)))
            chunks.append(jnp.concatenate(cols, axis=1) + bias_ref[...])
        mixed = jnp.concatenate(chunks, axis=0)
    sg = u_ref[...] * mixed
    width = attn.shape[1]
    h = x_ref[...] + _dot(attn.astype(BF16), wo_ref[:width, :]) + _dot(sg.astype(BF16), wo_ref[width:, :])
    h_ref[...] = h
    hn_ref[...] = (_rms(h) * g_ref[...]).astype(BF16)


def _mix_proj(x, o_lat, u, v, w, tm, single_token):
    n = x.shape[0]
    tok = lambda width: pl.BlockSpec((tm, width), lambda i: (i, 0))
    ws, bias = (w['ws_diag'], w['bias_first']) if single_token else (w['w_s'], w['bias_tab'])
    weights = [w['w_uv_bd'], ws, bias, w['w_o'], w['g_ffn']]
    return pl.pallas_call(
        functools.partial(_mix_proj_kernel, single_token=single_token),
        grid=(n // tm,),
        in_specs=[tok(D_MODEL), tok(N_HEADS * KV_RANK), tok(CG_WIDTH), tok(CG_WIDTH)]
                 + [_full(a.shape) for a in weights],
        out_specs=[tok(D_MODEL), tok(D_MODEL)],
        out_shape=[jax.ShapeDtypeStruct((n, D_MODEL), F32),
                   jax.ShapeDtypeStruct((n, D_MODEL), BF16)],
        compiler_params=_params(("parallel",)),
        name="mix_proj_single" if single_token else "mix_proj",
    )(x, o_lat, u, v, *weights)


def _conv_ffn_kernel(*refs, single_token):
    if single_token:
        h_ref, hn_ref, win_ref, cw_ref, cb_ref, wout_ref, h0_ref, h1_ref, out_ref, a_ref, act_s = refs
    else:
        h_ref, hn_ref, win_ref, cw_ref, cb_ref, wout_ref, out_ref, last_ref, act_s, carry = refs
        t = pl.program_id(1)

        @pl.when(t == 0)
        def _():
            carry[...] = jnp.zeros(carry.shape, F32)
    tm = hn_ref.shape[0]
    hn = hn_ref[...]

    def conv(cols):
        a = _dot(hn, win_ref[:, cols])
        if single_token:
            a2, a1 = h0_ref[:, cols], h1_ref[:, cols]
            a_ref[:, cols] = a
        else:
            ext = jnp.concatenate([carry[:, cols], a], axis=0)
            a1 = ext[SUBLANES - 1:SUBLANES - 1 + tm]
            a2 = ext[SUBLANES - 2:SUBLANES - 2 + tm]
            tail = a[tm - SUBLANES:]
            carry[:, cols] = tail
            last_ref[0, :, cols] = tail[SUBLANES - (CONV_W - 1):]
        return cb_ref[:, cols] + cw_ref[0:1, cols] * a2 + cw_ref[1:2, cols] * a1 + cw_ref[2:3, cols] * a

    for j in range(D_FF // FF_TILE):
        gate = slice(j * FF_TILE, (j + 1) * FF_TILE)
        up = slice(D_FF + j * FF_TILE, D_FF + (j + 1) * FF_TILE)
        act_s[:, gate] = (jax.nn.silu(conv(gate)) * conv(up)).astype(BF16)
    out_ref[...] = h_ref[...] + _dot(act_s[...], wout_ref[...])


def _conv_ffn(h, hn, w, batch, seq, tm, hist=None):
    single_token = hist is not None
    nt = seq // tm
    tok = lambda width: pl.BlockSpec((tm, width), lambda b, t: (b * nt + t, 0))
    resident = lambda a: pl.BlockSpec(a.shape, lambda b, t: (0, 0), pipeline_mode=pl.Buffered(1))
    weights = [w['w_ff_in'], w['conv_w'], w['conv_b'], w['w_ff_out']]
    in_specs = [tok(D_MODEL), tok(D_MODEL)] + [resident(a) for a in weights]
    args = [h, hn] + weights
    scratch = [pltpu.VMEM((tm, D_FF), BF16)]
    if single_token:
        in_specs += [resident(hist[0]), resident(hist[1])]
        args += list(hist)
        last_spec = pl.BlockSpec((tm, 2 * D_FF), lambda b, t: (0, 0))
        last_shape = jax.ShapeDtypeStruct((tm, 2 * D_FF), F32)
    else:
        last_spec = pl.BlockSpec((1, CONV_W - 1, 2 * D_FF), lambda b, t: (b * nt + t, 0, 0))
        last_shape = jax.ShapeDtypeStruct((batch * nt, CONV_W - 1, 2 * D_FF), F32)
        scratch.append(pltpu.VMEM((SUBLANES, 2 * D_FF), F32))
    return pl.pallas_call(
        functools.partial(_conv_ffn_kernel, single_token=single_token),
        grid=(batch, nt),
        in_specs=in_specs,
        out_specs=[tok(D_MODEL), last_spec],
        out_shape=[jax.ShapeDtypeStruct((batch * seq, D_MODEL), F32), last_shape],
        scratch_shapes=scratch,
        compiler_params=_params(("arbitrary", "arbitrary")),
        name="conv_ffn_single" if single_token else "conv_ffn",
    )(*args)


def _ple_kernel(h_ref, p_ref, g_ref, wgate_ref, wproj_ref, gpost_ref, y_ref):
    h = h_ref[...]
    gate = jax.nn.sigmoid(_dot((_rms(h) * g_ref[...]).astype(BF16), wgate_ref[...]))
    e = _rms(_dot(p_ref[...].astype(BF16), wproj_ref[...])) * gpost_ref[...]
    y_ref[...] = h + gate * e


def _ple(h, p, w, tm):
    n = h.shape[0]
    tok = lambda width: pl.BlockSpec((tm, width), lambda i: (i, 0))
    weights = [w['g_ple'], w['w_gate'], w['w_proj'], w['g_post']]
    return pl.pallas_call(
        _ple_kernel,
        grid=(n // tm,),
        in_specs=[tok(D_MODEL), tok(PLE_DIM)] + [_full(a.shape) for a in weights],
        out_specs=tok(D_MODEL),
        out_shape=jax.ShapeDtypeStruct((n, D_MODEL), F32),
        compiler_params=_params(("parallel",)),
        name="ple",
    )(h, p, *weights)


def _rope_tables(pos):
    inv = ROPE_THETA ** (-jnp.arange(ROPE_HALF, dtype=F32) * (2.0 / QK_ROPE))
    ang = pos.astype(F32)[:, None] * inv[None, :]
    cos, sin = jnp.cos(ang), jnp.sin(ang)
    n = pos.shape[0]
    ones = jnp.ones((n, QK_NOPE), F32)
    zq = jnp.zeros((n, QK_NOPE), F32)
    zp = jnp.zeros((n, HEAD_PAD - QK_NOPE - QK_ROPE), F32)
    ca = jnp.concatenate([ones, cos, cos, zp], axis=1)
    sb = jnp.concatenate([zq, -sin, sin, zp], axis=1)
    return ca, sb, cos.T, sin.T


def _prep_weights(i, attn_norm_w, w_in, q_norm_w, w_uq, q_nope_norm_w, q_rope_norm_w, kv_norm_w,
                  k_rope_norm_w, w_uk, k_nope_norm_w, w_uv, w_s, b_s, w_o, ffn_norm_w, w_ff_in,
                  conv_w, conv_b, w_ff_out, ple_norm_w, w_ple_gate, w_ple_proj, ple_post_norm_w):
    row = lambda g: g[i][None, :]
    swap = lambda a: jnp.concatenate([a[..., ROPE_HALF:], a[..., :ROPE_HALF]], axis=-1)
    win = w_in[i]
    wq = w_uq[i].reshape(Q_RANK, N_HEADS, QK_NOPE + QK_ROPE)
    nope, rope = wq[..., :QK_NOPE], wq[..., QK_NOPE:]
    pad = HEAD_PAD - QK_NOPE - QK_ROPE
    w_qt = jnp.concatenate([nope, rope, jnp.zeros((Q_RANK, N_HEADS, pad), F32)],
                           axis=-1).reshape(Q_RANK, QK_WIDTH).T
    col = lambda g: jnp.broadcast_to(g[:, None], (g.shape[0], LANES))
    kr_cols = win[:, OFF_KR:OFF_U]
    slot_of = lambda c: jnp.concatenate([jnp.zeros((D_MODEL, QK_NOPE), F32), c,
                                         jnp.zeros((D_MODEL, pad), F32)], axis=1)
    w_in_e = jnp.concatenate([win[:, :OFF_KR], win[:, OFF_U:], slot_of(kr_cols), slot_of(swap(kr_cols))],
                             axis=1)
    w_uk_e = jnp.concatenate([w_uk[i], jnp.zeros((KV_RANK, N_HEADS, HEAD_PAD - QK_NOPE), F32)],
                             axis=-1).reshape(KV_RANK, QK_WIDTH)
    gn, gr = k_nope_norm_w[i], k_rope_norm_w[i]
    ga = jnp.tile(jnp.concatenate([gn, gr, jnp.zeros((pad,), F32)]), N_HEADS)[None, :]
    gb = jnp.tile(jnp.concatenate([jnp.zeros((QK_NOPE,), F32), swap(gr), jnp.zeros((pad,), F32)]),
                  N_HEADS)[None, :]
    lane = jnp.arange(2 * HEAD_PAD)
    slot, off = lane // HEAD_PAD, lane % HEAD_PAD
    grp = jnp.where(off < QK_NOPE, 0, jnp.where(off < QK_NOPE + QK_ROPE, 1, 2))
    same = (slot[:, None] == slot[None, :]) & (grp[:, None] == grp[None, :]) & (grp[:, None] < 2)
    gm = jnp.where(same, jnp.where(grp[:, None] == 0, 1.0 / QK_NOPE, 1.0 / QK_ROPE), 0.0)
    w_ukt = w_uk[i].reshape(KV_RANK, N_HEADS * QK_NOPE).T
    eye = jnp.eye(N_HEADS, dtype=F32)
    w_uv_bd = jnp.einsum('rhd,hg->hrgd', w_uv[i], eye).reshape(N_HEADS * KV_RANK, N_HEADS * V_DIM)
    return {
        'g_attn': row(attn_norm_w),
        'w_in': w_in_e.astype(BF16),
        'g_q': row(q_norm_w), 'w_qt': w_qt.astype(BF16),
        'g_qn': col(q_nope_norm_w[i]), 'g_qr': col(q_rope_norm_w[i]),
        'g_kv': row(kv_norm_w), 'w_kvt': win[:, OFF_KV:OFF_KR].T.astype(BF16), 'g_kvt': col(kv_norm_w[i]),
        'w_uk': w_uk_e.astype(BF16), 'gm': gm.astype(BF16), 'ga': ga, 'gb': gb,
        'w_ukt': w_ukt.astype(BF16), 'w_ukt_f32': w_ukt,
        'g_kn_row': jnp.tile(k_nope_norm_w[i], N_HEADS)[None, :],
        'w_uv_bd': w_uv_bd.astype(BF16),
        'w_s': w_s[i],
        'bias_tab': jnp.repeat(b_s[i].T, CG_DIM, axis=1),
        'ws_diag': jnp.repeat(w_s[i][:, 0, 0], CG_DIM)[None, :],
        'bias_first': jnp.repeat(b_s[i][:, 0], CG_DIM)[None, :],
        'w_o': w_o[i].astype(BF16), 'g_ffn': row(ffn_norm_w),
        'w_ff_in': w_ff_in[i].astype(BF16), 'conv_w': conv_w[i], 'conv_b': row(conv_b),
        'w_ff_out': w_ff_out[i].astype(BF16),
        'g_ple': row(ple_norm_w), 'w_gate': w_ple_gate[i].astype(BF16),
        'w_proj': w_ple_proj[i].astype(BF16), 'g_post': row(ple_post_norm_w),
    }


def kernel(x_prompt, x_sample, cache_ckv, cache_krope, state_conv, page_table, p_prompt, p_sample,
           attn_norm_w, w_in, q_norm_w, w_uq, q_nope_norm_w, q_rope_norm_w, kv_norm_w, k_rope_norm_w,
           w_uk, k_nope_norm_w, w_uv, w_s, b_s, w_o, ffn_norm_w, w_ff_in, conv_w, conv_b, w_ff_out,
           ple_norm_w, w_ple_gate, w_ple_proj, ple_post_norm_w):
    batch, seq, _ = x_prompt.shape
    n_samp, seq_s, _ = x_sample.shape
    depth = w_in.shape[0]
    assert depth == 1 and seq_s == 1 and seq % TOK_TILE == 0 and seq % CHUNK == 0
    past_len = page_table.shape[1] * PAGE_SIZE
    assert past_len % KV_TILE == 0
    w = _prep_weights(0, attn_norm_w, w_in, q_norm_w, w_uq, q_nope_norm_w, q_rope_norm_w, kv_norm_w,
                      k_rope_norm_w, w_uk, k_nope_norm_w, w_uv, w_s, b_s, w_o, ffn_norm_w, w_ff_in,
                      conv_w, conv_b, w_ff_out, ple_norm_w, w_ple_gate, w_ple_proj, ple_post_norm_w)

    n_tok = batch * seq
    tabs = _rope_tables(jnp.arange(seq, dtype=jnp.int32))
    xp = x_prompt.reshape(n_tok, D_MODEL)
    qt, k, ckv, vt, kr, u, v = _in_proj(xp, tabs, w, TOK_TILE)
    o_lat = _prompt_attn(qt, k, vt, batch, seq)
    h, hn = _mix_proj(xp, o_lat, u, v, w, TOK_TILE, single_token=False)
    h2, a_last = _conv_ffn(h, hn, w, batch, seq, TOK_TILE)
    y_prompt = _ple(h2, p_prompt[0].reshape(n_tok, PLE_DIM), w, TOK_TILE).reshape(batch, seq, D_MODEL)
    n_keep = (seq - 1) % CHUNK + 1
    new_ckv_p = ckv.reshape(1, batch, seq, KV_RANK)
    rope_lanes = slice(QK_NOPE, QK_NOPE + QK_ROPE)
    new_kr_p = kr[:, rope_lanes].reshape(1, batch, seq, QK_ROPE)
    new_v_p = v.reshape(batch, seq, CG_WIDTH)[:, seq - n_keep:][None]
    new_conv_p = a_last.reshape(batch, seq // TOK_TILE, CONV_W - 1, 2 * D_FF)[:, -1][None]

    tabs_s = _rope_tables(jnp.full((n_samp,), past_len, dtype=jnp.int32))
    xs = x_sample.reshape(n_samp, D_MODEL)
    qt_s, k_s, ckv_s, _, kr_row_s, u_s, v_s = _in_proj(xs, tabs_s, w, n_samp)
    q3 = jnp.transpose(qt_s, (2, 0, 1)).astype(F32)
    kr_s = kr_row_s[:, rope_lanes]
    qa = _absorb_q(q3[:, :, :QK_NOPE].reshape(n_samp, N_HEADS * QK_NOPE), w)
    cache_rt = jnp.swapaxes(cache_krope[0], 1, 2)
    o_s = _sample_attn(page_table, q3, qa.reshape(n_samp, N_HEADS, KV_RANK),
                       k_s.reshape(n_samp, N_HEADS, HEAD_PAD), ckv_s.reshape(n_samp, 1, KV_RANK),
                       w, cache_ckv[0], cache_rt)
    o_s = o_s.reshape(n_samp, N_HEADS * KV_RANK).astype(BF16)
    h_s, hn_s = _mix_proj(xs, o_s, u_s, v_s, w, n_samp, single_token=True)
    hist = (state_conv[0, :, 0, :], state_conv[0, :, 1, :])
    h2_s, a_s = _conv_ffn(h_s, hn_s, w, 1, n_samp, n_samp, hist=hist)
    y_sample = _ple(h2_s, p_sample[0].reshape(n_samp, PLE_DIM), w, n_samp).reshape(n_samp, 1, D_MODEL)
    new_conv_s = jnp.stack([hist[1], a_s], axis=1)[None]

    return (y_prompt, y_sample, new_ckv_p, new_kr_p,
            ckv_s.reshape(1, n_samp, 1, KV_RANK), kr_s.reshape(1, n_samp, 1, QK_ROPE),
            new_v_p, v_s.reshape(1, n_samp, 1, CG_WIDTH), new_conv_p, new_conv_s)
```

```python
import functools

import jax
import jax.numpy as jnp
from jax import lax
from jax.experimental import pallas as pl
from jax.experimental.pallas import tpu as pltpu

F32 = jnp.float32
BF16 = jnp.bfloat16

D_MODEL = 1024
N_HEADS = 8
QK_NOPE = 64
QK_ROPE = 32
ROPE_HALF = QK_ROPE // 2
V_DIM = 64
Q_RANK = 256
KV_RANK = 128
CHUNK = 128
N_CG = 8
CG_WIDTH = 512
CG_DIM = 64
D_FF = 2816
CONV_W = 3
PLE_DIM = 256
PAGE_SIZE = 128
ROPE_THETA = 10000.0
EPS = 1e-6
SCALE = (QK_NOPE + QK_ROPE) ** -0.5
Q_SCALE = SCALE * 1.4426950408889634
OFF_KV = Q_RANK
OFF_KR = OFF_KV + KV_RANK
OFF_U = OFF_KR + QK_ROPE
OFF_V = OFF_U + CG_WIDTH
IN_WIDTH = OFF_V + CG_WIDTH
HEAD_PAD = 128
QK_WIDTH = N_HEADS * HEAD_PAD
Z_WIDTH = Q_RANK + KV_RANK + 2 * CG_WIDTH

LANES = 128
SUBLANES = 8
VMEM_LIMIT_BYTES = 56 * 1024 * 1024

TOK_TILE = 512
ATT_BLOCK = 1024
KEY_BLOCK = 512
ONES_ROWS = 16
FF_TILE = 256
KV_TILE = 2048

NT_DIMS = (((1,), (1,)), ((), ()))


def _params(semantics):
    return pltpu.CompilerParams(dimension_semantics=semantics,
                                vmem_limit_bytes=VMEM_LIMIT_BYTES)


def _rms(x):
    return x * lax.rsqrt(jnp.mean(x * x, axis=-1, keepdims=True) + EPS)


def _dot(a, b):
    return jnp.dot(a, b, preferred_element_type=F32)


def _dot_nt(a, b):
    return lax.dot_general(a, b, NT_DIMS, preferred_element_type=F32)


def _full(shape):
    zeros = (0,) * len(shape)
    return pl.BlockSpec(shape, lambda *_: zeros)


def _in_proj_kernel(x_ref, ca_ref, sb_ref, cost_ref, sint_ref,
                    g_attn_ref, w_in_ref, g_q_ref, w_qt_ref, g_qn_ref, g_qr_ref,
                    g_kv_ref, w_kvt_ref, g_kvt_ref, w_uk_ref, gm_ref, ga_ref, gb_ref,
                    qt_ref, k_ref, ckv_ref, vt_ref, kr_ref, u_ref, v_ref):
    tm = x_ref.shape[0]
    reps = tm // LANES
    hb = (_rms(x_ref[...]) * g_attn_ref[...]).astype(BF16)
    z = _dot(hb, w_in_ref[...])
    u_ref[...] = jax.nn.gelu(z[:, OFF_KR:OFF_KR + CG_WIDTH])
    v_ref[...] = jax.nn.gelu(z[:, OFF_KR + CG_WIDTH:OFF_KR + 2 * CG_WIDTH])
    ckv = _rms(z[:, OFF_KV:OFF_KR]) * g_kv_ref[...]
    ckv_ref[...] = ckv

    kra = z[:, Z_WIDTH:Z_WIDTH + HEAD_PAD]
    krb = z[:, Z_WIDTH + HEAD_PAD:]
    ka = _dot(ckv.astype(BF16), w_uk_ref[...]) + jnp.tile(kra, (1, N_HEADS))
    sq = ka * ka
    sq_hi = sq.astype(BF16)
    sq_lo = (sq - sq_hi.astype(F32)).astype(BF16)
    gm = gm_ref[...]
    pair = 2 * HEAD_PAD
    ms = jnp.concatenate(
        [_dot(sq_hi[:, i * pair:(i + 1) * pair], gm) + _dot(sq_lo[:, i * pair:(i + 1) * pair], gm)
         for i in range(N_HEADS // 2)], axis=1)
    ca = jnp.tile(ca_ref[...], (1, N_HEADS))
    sb = jnp.tile(sb_ref[...], (1, N_HEADS))
    k = lax.rsqrt(ms + EPS) * (ka * ga_ref[...] * ca + jnp.tile(krb, (1, N_HEADS)) * gb_ref[...] * sb)
    k_ref[...] = k.astype(BF16)
    kr_ref[...] = k[:, :HEAD_PAD]

    ct = _dot_nt(w_kvt_ref[...], hb)
    ct = ct * lax.rsqrt(jnp.mean(ct * ct, axis=0, keepdims=True) + EPS)
    vt_ref[...] = (ct * jnp.tile(g_kvt_ref[...], (1, reps))).astype(BF16)

    qln = (_rms(z[:, :Q_RANK]) * g_q_ref[...]).astype(BF16)
    qt = _dot_nt(w_qt_ref[...], qln).reshape(N_HEADS, HEAD_PAD, tm)
    nope = qt[:, :QK_NOPE]
    rope = qt[:, QK_NOPE:QK_NOPE + QK_ROPE]
    g_qn = jnp.tile(g_qn_ref[...], (1, reps))[None]
    g_qr = jnp.tile(g_qr_ref[...], (1, reps))[None]
    nope = nope * lax.rsqrt(jnp.mean(nope * nope, axis=1, keepdims=True) + EPS) * (g_qn * Q_SCALE)
    rope = rope * lax.rsqrt(jnp.mean(rope * rope, axis=1, keepdims=True) + EPS) * (g_qr * Q_SCALE)
    x1, x2 = rope[:, :ROPE_HALF], rope[:, ROPE_HALF:]
    cos, sin = cost_ref[...][None], sint_ref[...][None]
    pad = jnp.zeros((N_HEADS, HEAD_PAD - QK_NOPE - QK_ROPE, tm), F32)
    qt_ref[...] = jnp.concatenate([nope, x1 * cos - x2 * sin, x1 * sin + x2 * cos, pad], axis=1).astype(BF16)


def _in_proj(x, tabs, w, tm):
    n = x.shape[0]
    ca, sb, cost, sint = tabs
    pos_tiles = ca.shape[0] // tm
    tok = lambda width: pl.BlockSpec((tm, width), lambda i: (i, 0))
    tok_t = lambda rows: pl.BlockSpec((rows, tm), lambda i: (0, i))
    pos = pl.BlockSpec((tm, HEAD_PAD), lambda i: (i % pos_tiles, 0))
    pos_t = pl.BlockSpec((ROPE_HALF, tm), lambda i: (0, i % pos_tiles))
    weights = [w['g_attn'], w['w_in'], w['g_q'], w['w_qt'], w['g_qn'], w['g_qr'],
               w['g_kv'], w['w_kvt'], w['g_kvt'], w['w_uk'], w['gm'], w['ga'], w['gb']]
    return pl.pallas_call(
        _in_proj_kernel,
        grid=(n // tm,),
        in_specs=[tok(D_MODEL), pos, pos, pos_t, pos_t]
                 + [_full(a.shape) for a in weights],
        out_specs=[pl.BlockSpec((N_HEADS, HEAD_PAD, tm), lambda i: (0, 0, i)),
                   tok(QK_WIDTH), tok(KV_RANK), tok_t(KV_RANK), tok(HEAD_PAD),
                   tok(CG_WIDTH), tok(CG_WIDTH)],
        out_shape=[jax.ShapeDtypeStruct((N_HEADS, HEAD_PAD, n), BF16),
                   jax.ShapeDtypeStruct((n, QK_WIDTH), BF16),
                   jax.ShapeDtypeStruct((n, KV_RANK), F32),
                   jax.ShapeDtypeStruct((KV_RANK, n), BF16),
                   jax.ShapeDtypeStruct((n, HEAD_PAD), F32),
                   jax.ShapeDtypeStruct((n, CG_WIDTH), F32),
                   jax.ShapeDtypeStruct((n, CG_WIDTH), F32)],
        compiler_params=_params(("parallel",)),
        name="in_proj",
    )(x, ca, sb, cost, sint, *weights)


def _prompt_attn_kernel(qt_ref, k_ref, vt_ref, o_ref, m_s, acc_s):
    qi = pl.program_id(1)
    kb = KEY_BLOCK
    per_q = ATT_BLOCK // kb
    m_s[...] = jnp.full(m_s.shape, -jnp.inf, F32)
    acc_s[...] = jnp.zeros(acc_s.shape, F32)
    ones = jnp.ones((ONES_ROWS, kb), BF16)

    def key_block(off, first_q, masked):
        cols = slice(first_q, ATT_BLOCK)
        vals = jnp.concatenate([vt_ref[:, pl.ds(off, kb)], ones], axis=0)
        if masked:
            key = lax.broadcasted_iota(jnp.int32, (kb, ATT_BLOCK - first_q), 0)
            qry = lax.broadcasted_iota(jnp.int32, (kb, ATT_BLOCK - first_q), 1)
            visible = key <= qry
        for h in range(N_HEADS):
            s = _dot(k_ref[pl.ds(off, kb), h * HEAD_PAD:(h + 1) * HEAD_PAD], qt_ref[h, :, cols])
            if masked:
                s = jnp.where(visible, s, -jnp.inf)
            m_old = m_s[h, :, cols]
            m_new = jnp.maximum(m_old, jnp.max(s, axis=0, keepdims=True))
            p = jnp.exp2(s - m_new)
            acc_s[h, :, cols] = jnp.exp2(m_old - m_new) * acc_s[h, :, cols] + _dot(vals, p.astype(BF16))
            m_s[h, :, cols] = m_new

    def body(j, carry):
        key_block(pl.multiple_of(j * kb, kb), 0, False)
        return carry

    lax.fori_loop(0, qi * per_q, body, 0)
    for d in range(per_q):
        key_block(pl.multiple_of(qi * ATT_BLOCK + d * kb, kb), d * kb, True)
    for h in range(N_HEADS):
        acc = acc_s[h]
        o_t = acc[:KV_RANK] / acc[KV_RANK:KV_RANK + 1]
        o_ref[:, h * HEAD_PAD:(h + 1) * HEAD_PAD] = o_t.T.astype(BF16)


def _prompt_attn(qt, k, vt, batch, seq):
    nq = seq // ATT_BLOCK
    return pl.pallas_call(
        _prompt_attn_kernel,
        grid=(batch, nq),
        in_specs=[pl.BlockSpec((N_HEADS, HEAD_PAD, ATT_BLOCK), lambda b, i: (0, 0, b * nq + i)),
                  pl.BlockSpec((seq, QK_WIDTH), lambda b, i: (b, 0)),
                  pl.BlockSpec((KV_RANK, seq), lambda b, i: (0, b))],
        out_specs=pl.BlockSpec((ATT_BLOCK, N_HEADS * KV_RANK), lambda b, i: (b * nq + i, 0)),
        out_shape=jax.ShapeDtypeStruct((batch * seq, N_HEADS * KV_RANK), BF16),
        scratch_shapes=[pltpu.VMEM((N_HEADS, 1, ATT_BLOCK), F32),
                        pltpu.VMEM((N_HEADS, KV_RANK + ONES_ROWS, ATT_BLOCK), F32)],
        compiler_params=_params(("parallel", "parallel")),
        name="prompt_attn",
    )(qt, k, vt)


def _absorb_q_kernel(qn_ref, g_ref, wukt_ref, qa_ref):
    qg = qn_ref[...] * g_ref[...]
    lane = lax.broadcasted_iota(jnp.int32, qg.shape, 1)
    for h in range(N_HEADS):
        q_h = jnp.where((lane >= h * QK_NOPE) & (lane < (h + 1) * QK_NOPE), qg, 0.0)
        qa_ref[:, h * KV_RANK:(h + 1) * KV_RANK] = jnp.dot(
            q_h, wukt_ref[...], precision=lax.Precision.HIGHEST, preferred_element_type=F32)


def _absorb_q(qn, w):
    n_samp = qn.shape[0]
    args = [qn, w['g_kn_row'], w['w_ukt_f32']]
    return pl.pallas_call(
        _absorb_q_kernel,
        grid=(1,),
        in_specs=[_full(a.shape) for a in args],
        out_specs=_full((n_samp, N_HEADS * KV_RANK)),
        out_shape=jax.ShapeDtypeStruct((n_samp, N_HEADS * KV_RANK), F32),
        compiler_params=_params(("arbitrary",)),
        name="absorb_q",
    )(*args)


def _sample_attn_kernel(pt_ref, q_ref, qa_ref, kown_ref, cown_ref, wukb_ref,
                        cache_c, cache_r, o_ref,
                        wext, cbuf, rbuf, sems, cb_all, s_all, *, n_pages):
    b = pl.program_id(0)
    last = pl.num_programs(0) - 1
    slot = lax.rem(b, 2)
    past = n_pages * PAGE_SIZE
    n_tiles = past // KV_TILE
    pages_per_tile = KV_TILE // PAGE_SIZE
    nw = N_HEADS * QK_NOPE
    nxt = jnp.minimum(b + 1, last)

    def start_page(bb, sl, i):
        page = pt_ref[bb * n_pages + i]
        off = pl.multiple_of(i * PAGE_SIZE, PAGE_SIZE)
        pltpu.make_async_copy(cache_c.at[page], cbuf.at[sl, pl.ds(off, PAGE_SIZE)], sems.at[0, sl]).start()
        pltpu.make_async_copy(cache_r.at[page], rbuf.at[sl, :, pl.ds(off, PAGE_SIZE)],
                              sems.at[1, sl]).start(priority=1)

    def wait_pages(sl):
        pltpu.make_async_copy(cbuf.at[sl], cbuf.at[sl], sems.at[0, sl]).wait()
        pltpu.make_async_copy(rbuf.at[sl], rbuf.at[sl], sems.at[1, sl]).wait()

    @pl.when(b == 0)
    def _():
        def issue(i, carry):
            start_page(0, 0, i)
            return carry
        lax.fori_loop(0, n_pages, issue, 0)
        wext[:nw, :] = wukb_ref[...]

    wext[nw:, :] = jnp.concatenate(
        [qa_ref[0], jnp.zeros((wext.shape[0] - nw - N_HEADS, KV_RANK), F32)], axis=0).astype(BF16)
    qr = q_ref[0][:, QK_NOPE:QK_NOPE + QK_ROPE].astype(BF16)

    def scores(c_blk, krt_blk):
        tk = c_blk.shape[0]
        cb = c_blk.astype(BF16)
        knt = _dot_nt(wext[...], cb)
        kn3 = knt[:nw].reshape(N_HEADS, QK_NOPE, tk)
        r = lax.rsqrt(jnp.sum(kn3 * kn3, axis=1) * (1.0 / QK_NOPE) + EPS)
        s_rope = _dot(qr, krt_blk.astype(BF16))
        return cb, knt[nw:nw + N_HEADS] * r + s_rope

    s_own = jnp.sum(q_ref[0] * kown_ref[0].astype(F32), axis=-1, keepdims=True)

    wait_pages(slot)
    for j in range(n_tiles):
        for i in range(pages_per_tile):
            start_page(nxt, 1 - slot, j * pages_per_tile + i)
        keys = slice(j * KV_TILE, (j + 1) * KV_TILE)
        cb, s = scores(cbuf[slot, keys, :], rbuf[slot, :, keys])
        cb_all[keys, :] = cb
        s_all[:, keys] = s

    s = s_all[...]
    m = jnp.maximum(jnp.max(s, axis=-1, keepdims=True), s_own)
    p = jnp.exp2(s - m)
    p_own = jnp.exp2(s_own - m)
    l = jnp.sum(p, axis=-1, keepdims=True) + p_own
    round_bf16 = lambda a: a.astype(BF16).astype(F32)
    acc = _dot(p.astype(BF16), cb_all[...]) + round_bf16(p_own) * round_bf16(cown_ref[0])
    o_ref[0] = acc / l

    @pl.when(b == last)
    def _():
        wait_pages(1 - slot)


def _sample_attn(page_table, q, qa, k_own, c_own, w, cache_c, cache_rt):
    n_samp, n_pages = page_table.shape
    past = n_pages * PAGE_SIZE
    wext_rows = N_HEADS * QK_NOPE + 2 * SUBLANES
    per_sample = lambda a: pl.BlockSpec((1,) + a.shape[1:], lambda b, pt: (b, 0, 0))
    grid_spec = pltpu.PrefetchScalarGridSpec(
        num_scalar_prefetch=1,
        grid=(n_samp,),
        in_specs=[per_sample(q), per_sample(qa), per_sample(k_own), per_sample(c_own),
                  pl.BlockSpec(w['w_ukt'].shape, lambda b, pt: (0, 0)),
                  pl.BlockSpec(memory_space=pl.ANY),
                  pl.BlockSpec(memory_space=pl.ANY)],
        out_specs=pl.BlockSpec((1, N_HEADS, KV_RANK), lambda b, pt: (b, 0, 0)),
        scratch_shapes=[pltpu.VMEM((wext_rows, KV_RANK), BF16),
                        pltpu.VMEM((2, past, KV_RANK), F32),
                        pltpu.VMEM((2, QK_ROPE, past), F32),
                        pltpu.SemaphoreType.DMA((2, 2)),
                        pltpu.VMEM((past, KV_RANK), BF16),
                        pltpu.VMEM((N_HEADS, past), F32)])
    return pl.pallas_call(
        functools.partial(_sample_attn_kernel, n_pages=n_pages),
        grid_spec=grid_spec,
        out_shape=jax.ShapeDtypeStruct((n_samp, N_HEADS, KV_RANK), F32),
        compiler_params=_params(("arbitrary",)),
        name="sample_attn",
    )(page_table.reshape(-1), q, qa, k_own, c_own, w['w_ukt'], cache_c, cache_rt)


def _mix_proj_kernel(x_ref, o_ref, u_ref, v_ref, wuv_ref, ws_ref, bias_ref, wo_ref, g_ref,
                     h_ref, hn_ref, *, single_token):
    tm = x_ref.shape[0]
    attn = _dot(o_ref[...], wuv_ref[...])
    v = v_ref[...]
    if single_token:
        mixed = v * ws_ref[...] + bias_ref[...]
    else:
        vb = v.astype(BF16)
        row = lax.broadcasted_iota(jnp.int32, (CHUNK, CHUNK), 0)
        col = lax.broadcasted_iota(jnp.int32, (CHUNK, CHUNK), 1)
        w_tril = [jnp.where(col <= row, ws_ref[g], 0.0).astype(BF16) for g in range(N_CG)]
        low_half = lax.broadcasted_iota(jnp.int32, (CHUNK, LANES), 1) < CG_DIM
        chunks = []
        for c in range(tm // CHUNK):
            cols = []
            for j in range(CG_WIDTH // LANES):
                vp = vb[c * CHUNK:(c + 1) * CHUNK, j * LANES:(j + 1) * LANES]
                cols.append(jnp.where(low_half, _dot(w_tril[2 * j], vp), _dot(w_tril[2 * j + 1], vp)))
            chunks.append(jnp.concatenate(cols, axis=1) + bias_ref[...])
        mixed = jnp.concatenate(chunks, axis=0)
    sg = u_ref[...] * mixed
    width = attn.shape[1]
    h = x_ref[...] + _dot(attn.astype(BF16), wo_ref[:width, :]) + _dot(sg.astype(BF16), wo_ref[width:, :])
    h_ref[...] = h
    hn_ref[...] = (_rms(h) * g_ref[...]).astype(BF16)


def _conv_ffn_kernel(*refs, single_token):
    if single_token:
        h_ref, hn_ref, win_ref, cw_ref, cb_ref, wout_ref, h0_ref, h1_ref, out_ref, a_ref, act_s = refs
    else:
        h_ref, hn_ref, win_ref, cw_ref, cb_ref, wout_ref, out_ref, last_ref, act_s, carry = refs
    tm = hn_ref.shape[0]
    hn = hn_ref[...]

    def conv(cols):
        a = _dot(hn, win_ref[:, cols])
        if single_token:
            a2, a1 = h0_ref[:, cols], h1_ref[:, cols]
            a_ref[:, cols] = a
        else:
            ext = jnp.concatenate([carry[:, cols], a], axis=0)
            a1 = ext[SUBLANES - 1:SUBLANES - 1 + tm]
            a2 = ext[SUBLANES - 2:SUBLANES - 2 + tm]
            tail = a[tm - SUBLANES:]
            carry[:, cols] = tail
            last_ref[0, :, cols] = tail[SUBLANES - (CONV_W - 1):]
        return cb_ref[:, cols] + cw_ref[0:1, cols] * a2 + cw_ref[1:2, cols] * a1 + cw_ref[2:3, cols] * a

    for j in range(D_FF // FF_TILE):
        gate = slice(j * FF_TILE, (j + 1) * FF_TILE)
        up = slice(D_FF + j * FF_TILE, D_FF + (j + 1) * FF_TILE)
        act_s[:, gate] = (jax.nn.silu(conv(gate)) * conv(up)).astype(BF16)
    out_ref[...] = h_ref[...] + _dot(act_s[...], wout_ref[...])


def _ple_kernel(h_ref, p_ref, g_ref, wgate_ref, wproj_ref, gpost_ref, y_ref):
    h = h_ref[...]
    gate = jax.nn.sigmoid(_dot((_rms(h) * g_ref[...]).astype(BF16), wgate_ref[...]))
    e = _rms(_dot(p_ref[...].astype(BF16), wproj_ref[...])) * gpost_ref[...]
    y_ref[...] = h + gate * e


def _post_attn_kernel(*refs, single_token):
    x_ref, o_ref, u_ref, v_ref, p_ref = refs[:5]
    mix_w, ffn_w, ple_w = refs[5:10], refs[10:14], refs[14:18]
    if single_token:
        h0_ref, h1_ref, y_ref, a_ref, act_s, h_s, hn_s, h2_s = refs[18:]
        _mix_proj_kernel(x_ref, o_ref, u_ref, v_ref, *mix_w, h_s, hn_s, single_token=True)
        _conv_ffn_kernel(h_s, hn_s, *ffn_w, h0_ref, h1_ref, h2_s, a_ref, act_s, single_token=True)
        _ple_kernel(h2_s, p_ref, *ple_w, y_ref)
        return
    y_ref, last_ref, act_s, carry, h_s, hn_s, h2_s = refs[18:]

    @pl.when(pl.program_id(1) == 0)
    def _():
        carry[...] = jnp.zeros(carry.shape, F32)

    _mix_proj_kernel(x_ref, o_ref, u_ref, v_ref, *mix_w, h_s, hn_s, single_token=False)
    _conv_ffn_kernel(h_s, hn_s, *ffn_w, h2_s, last_ref, act_s, carry, single_token=False)
    _ple_kernel(h2_s, p_ref, *ple_w, y_ref)


def _post_attn(x, o_lat, u, v, p, w, batch, seq, tm, hist=None):
    single_token = hist is not None
    nt = seq // tm
    tok = lambda width: pl.BlockSpec((tm, width), lambda b, t: (b * nt + t, 0))
    resident = lambda a: pl.BlockSpec(a.shape, lambda b, t: (0,) * a.ndim, pipeline_mode=pl.Buffered(1))
    ws, bias = (w['ws_diag'], w['bias_first']) if single_token else (w['w_s'], w['bias_tab'])
    weights = [w['w_uv_bd'], ws, bias, w['w_o'], w['g_ffn'],
               w['w_ff_in'], w['conv_w'], w['conv_b'], w['w_ff_out'],
               w['g_ple'], w['w_gate'], w['w_proj'], w['g_post']]
    in_specs = ([tok(D_MODEL), tok(N_HEADS * KV_RANK), tok(CG_WIDTH), tok(CG_WIDTH), tok(PLE_DIM)]
                + [resident(a) for a in weights])
    args = [x, o_lat, u, v, p] + weights
    scratch = [pltpu.VMEM((tm, D_FF), BF16)]
    if single_token:
        in_specs += [resident(hist[0]), resident(hist[1])]
        args += list(hist)
        last_spec = pl.BlockSpec((tm, 2 * D_FF), lambda b, t: (0, 0))
        last_shape = jax.ShapeDtypeStruct((tm, 2 * D_FF), F32)
    else:
        last_spec = pl.BlockSpec((1, CONV_W - 1, 2 * D_FF), lambda b, t: (b * nt + t, 0, 0))
        last_shape = jax.ShapeDtypeStruct((batch * nt, CONV_W - 1, 2 * D_FF), F32)
        scratch.append(pltpu.VMEM((SUBLANES, 2 * D_FF), F32))
    scratch += [pltpu.VMEM((tm, D_MODEL), F32), pltpu.VMEM((tm, D_MODEL), BF16),
                pltpu.VMEM((tm, D_MODEL), F32)]
    return pl.pallas_call(
        functools.partial(_post_attn_kernel, single_token=single_token),
        grid=(batch, nt),
        in_specs=in_specs,
        out_specs=[tok(D_MODEL), last_spec],
        out_shape=[jax.ShapeDtypeStruct((batch * seq, D_MODEL), F32), last_shape],
        scratch_shapes=scratch,
        compiler_params=_params(("arbitrary", "arbitrary")),
        name="post_attn_single" if single_token else "post_attn",
    )(*args)


def _rope_tables(pos):
    inv = ROPE_THETA ** (-jnp.arange(ROPE_HALF, dtype=F32) * (2.0 / QK_ROPE))
    ang = pos.astype(F32)[:, None] * inv[None, :]
    cos, sin = jnp.cos(ang), jnp.sin(ang)
    n = pos.shape[0]
    ones = jnp.ones((n, QK_NOPE), F32)
    zq = jnp.zeros((n, QK_NOPE), F32)
    zp = jnp.zeros((n, HEAD_PAD - QK_NOPE - QK_ROPE), F32)
    ca = jnp.concatenate([ones, cos, cos, zp], axis=1)
    sb = jnp.concatenate([zq, -sin, sin, zp], axis=1)
    return ca, sb, cos.T, sin.T


def _prep_weights(i, attn_norm_w, w_in, q_norm_w, w_uq, q_nope_norm_w, q_rope_norm_w, kv_norm_w,
                  k_rope_norm_w, w_uk, k_nope_norm_w, w_uv, w_s, b_s, w_o, ffn_norm_w, w_ff_in,
                  conv_w, conv_b, w_ff_out, ple_norm_w, w_ple_gate, w_ple_proj, ple_post_norm_w):
    row = lambda g: g[i][None, :]
    swap = lambda a: jnp.concatenate([a[..., ROPE_HALF:], a[..., :ROPE_HALF]], axis=-1)
    win = w_in[i]
    wq = w_uq[i].reshape(Q_RANK, N_HEADS, QK_NOPE + QK_ROPE)
    nope, rope = wq[..., :QK_NOPE], wq[..., QK_NOPE:]
    pad = HEAD_PAD - QK_NOPE - QK_ROPE
    w_qt = jnp.concatenate([nope, rope, jnp.zeros((Q_RANK, N_HEADS, pad), F32)],
                           axis=-1).reshape(Q_RANK, QK_WIDTH).T
    col = lambda g: jnp.broadcast_to(g[:, None], (g.shape[0], LANES))
    kr_cols = win[:, OFF_KR:OFF_U]
    slot_of = lambda c: jnp.concatenate([jnp.zeros((D_MODEL, QK_NOPE), F32), c,
                                         jnp.zeros((D_MODEL, pad), F32)], axis=1)
    w_in_e = jnp.concatenate([win[:, :OFF_KR], win[:, OFF_U:], slot_of(kr_cols), slot_of(swap(kr_cols))],
                             axis=1)
    w_uk_e = jnp.concatenate([w_uk[i], jnp.zeros((KV_RANK, N_HEADS, HEAD_PAD - QK_NOPE), F32)],
                             axis=-1).reshape(KV_RANK, QK_WIDTH)
    gn, gr = k_nope_norm_w[i], k_rope_norm_w[i]
    ga = jnp.tile(jnp.concatenate([gn, gr, jnp.zeros((pad,), F32)]), N_HEADS)[None, :]
    gb = jnp.tile(jnp.concatenate([jnp.zeros((QK_NOPE,), F32), swap(gr), jnp.zeros((pad,), F32)]),
                  N_HEADS)[None, :]
    lane = jnp.arange(2 * HEAD_PAD)
    slot, off = lane // HEAD_PAD, lane % HEAD_PAD
    grp = jnp.where(off < QK_NOPE, 0, jnp.where(off < QK_NOPE + QK_ROPE, 1, 2))
    same = (slot[:, None] == slot[None, :]) & (grp[:, None] == grp[None, :]) & (grp[:, None] < 2)
    gm = jnp.where(same, jnp.where(grp[:, None] == 0, 1.0 / QK_NOPE, 1.0 / QK_ROPE), 0.0)
    w_ukt = w_uk[i].reshape(KV_RANK, N_HEADS * QK_NOPE).T
    eye = jnp.eye(N_HEADS, dtype=F32)
    w_uv_bd = jnp.einsum('rhd,hg->hrgd', w_uv[i], eye).reshape(N_HEADS * KV_RANK, N_HEADS * V_DIM)
    return {
        'g_attn': row(attn_norm_w),
        'w_in': w_in_e.astype(BF16),
        'g_q': row(q_norm_w), 'w_qt': w_qt.astype(BF16),
        'g_qn': col(q_nope_norm_w[i]), 'g_qr': col(q_rope_norm_w[i]),
        'g_kv': row(kv_norm_w), 'w_kvt': win[:, OFF_KV:OFF_KR].T.astype(BF16), 'g_kvt': col(kv_norm_w[i]),
        'w_uk': w_uk_e.astype(BF16), 'gm': gm.astype(BF16), 'ga': ga, 'gb': gb,
        'w_ukt': w_ukt.astype(BF16), 'w_ukt_f32': w_ukt,
        'g_kn_row': jnp.tile(k_nope_norm_w[i], N_HEADS)[None, :],
        'w_uv_bd': w_uv_bd.astype(BF16),
        'w_s': w_s[i],
        'bias_tab': jnp.repeat(b_s[i].T, CG_DIM, axis=1),
        'ws_diag': jnp.repeat(w_s[i][:, 0, 0], CG_DIM)[None, :],
        'bias_first': jnp.repeat(b_s[i][:, 0], CG_DIM)[None, :],
        'w_o': w_o[i].astype(BF16), 'g_ffn': row(ffn_norm_w),
        'w_ff_in': w_ff_in[i].astype(BF16), 'conv_w': conv_w[i], 'conv_b': row(conv_b),
        'w_ff_out': w_ff_out[i].astype(BF16),
        'g_ple': row(ple_norm_w), 'w_gate': w_ple_gate[i].astype(BF16),
        'w_proj': w_ple_proj[i].astype(BF16), 'g_post': row(ple_post_norm_w),
    }


def kernel(x_prompt, x_sample, cache_ckv, cache_krope, state_conv, page_table, p_prompt, p_sample,
           attn_norm_w, w_in, q_norm_w, w_uq, q_nope_norm_w, q_rope_norm_w, kv_norm_w, k_rope_norm_w,
           w_uk, k_nope_norm_w, w_uv, w_s, b_s, w_o, ffn_norm_w, w_ff_in, conv_w, conv_b, w_ff_out,
           ple_norm_w, w_ple_gate, w_ple_proj, ple_post_norm_w):
    batch, seq, _ = x_prompt.shape
    n_samp, seq_s, _ = x_sample.shape
    depth = w_in.shape[0]
    assert depth == 1 and seq_s == 1 and seq % TOK_TILE == 0 and seq % CHUNK == 0
    past_len = page_table.shape[1] * PAGE_SIZE
    assert past_len % KV_TILE == 0
    w = _prep_weights(0, attn_norm_w, w_in, q_norm_w, w_uq, q_nope_norm_w, q_rope_norm_w, kv_norm_w,
                      k_rope_norm_w, w_uk, k_nope_norm_w, w_uv, w_s, b_s, w_o, ffn_norm_w, w_ff_in,
                      conv_w, conv_b, w_ff_out, ple_norm_w, w_ple_gate, w_ple_proj, ple_post_norm_w)

    n_tok = batch * seq
    tabs = _rope_tables(jnp.arange(seq, dtype=jnp.int32))
    xp = x_prompt.reshape(n_tok, D_MODEL)
    qt, k, ckv, vt, kr, u, v = _in_proj(xp, tabs, w, TOK_TILE)
    o_lat = _prompt_attn(qt, k, vt, batch, seq)
    y_prompt, a_last = _post_attn(xp, o_lat, u, v, p_prompt[0].reshape(n_tok, PLE_DIM), w,
                                  batch, seq, TOK_TILE)
    y_prompt = y_prompt.reshape(batch, seq, D_MODEL)
    n_keep = (seq - 1) % CHUNK + 1
    new_ckv_p = ckv.reshape(1, batch, seq, KV_RANK)
    rope_lanes = slice(QK_NOPE, QK_NOPE + QK_ROPE)
    new_kr_p = kr[:, rope_lanes].reshape(1, batch, seq, QK_ROPE)
    new_v_p = v.reshape(batch, seq, CG_WIDTH)[:, seq - n_keep:][None]
    new_conv_p = a_last.reshape(batch, seq // TOK_TILE, CONV_W - 1, 2 * D_FF)[:, -1][None]

    tabs_s = _rope_tables(jnp.full((n_samp,), past_len, dtype=jnp.int32))
    xs = x_sample.reshape(n_samp, D_MODEL)
    qt_s, k_s, ckv_s, _, kr_row_s, u_s, v_s = _in_proj(xs, tabs_s, w, n_samp)
    q3 = jnp.transpose(qt_s, (2, 0, 1)).astype(F32)
    kr_s = kr_row_s[:, rope_lanes]
    qa = _absorb_q(q3[:, :, :QK_NOPE].reshape(n_samp, N_HEADS * QK_NOPE), w)
    cache_rt = jnp.swapaxes(cache_krope[0], 1, 2)
    o_s = _sample_attn(page_table, q3, qa.reshape(n_samp, N_HEADS, KV_RANK),
                       k_s.reshape(n_samp, N_HEADS, HEAD_PAD), ckv_s.reshape(n_samp, 1, KV_RANK),
                       w, cache_ckv[0], cache_rt)
    o_s = o_s.reshape(n_samp, N_HEADS * KV_RANK).astype(BF16)
    hist = (state_conv[0, :, 0, :], state_conv[0, :, 1, :])
    y_sample, a_s = _post_attn(xs, o_s, u_s, v_s, p_sample[0].reshape(n_samp, PLE_DIM), w,
                               1, n_samp, n_samp, hist=hist)
    y_sample = y_sample.reshape(n_samp, 1, D_MODEL)
    new_conv_s = jnp.stack([hist[1], a_s], axis=1)[None]

    return (y_prompt, y_sample, new_ckv_p, new_kr_p,
            ckv_s.reshape(1, n_samp, 1, KV_RANK), kr_s.reshape(1, n_samp, 1, QK_ROPE),
            new_v_p, v_s.reshape(1, n_samp, 1, CG_WIDTH), new_conv_p, new_conv_s)
```

```python
import functools

import jax
import jax.numpy as jnp
from jax import lax
from jax.experimental import pallas as pl
from jax.experimental.pallas import tpu as pltpu

F32 = jnp.float32
BF16 = jnp.bfloat16

D_MODEL = 1024
N_HEADS = 8
QK_NOPE = 64
QK_ROPE = 32
ROPE_HALF = QK_ROPE // 2
V_DIM = 64
Q_RANK = 256
KV_RANK = 128
CHUNK = 128
N_CG = 8
CG_WIDTH = 512
CG_DIM = 64
D_FF = 2816
CONV_W = 3
PLE_DIM = 256
PAGE_SIZE = 128
ROPE_THETA = 10000.0
EPS = 1e-6
SCALE = (QK_NOPE + QK_ROPE) ** -0.5
Q_SCALE = SCALE * 1.4426950408889634
OFF_KV = Q_RANK
OFF_KR = OFF_KV + KV_RANK
OFF_U = OFF_KR + QK_ROPE
OFF_V = OFF_U + CG_WIDTH
IN_WIDTH = OFF_V + CG_WIDTH
HEAD_PAD = 128
QK_WIDTH = N_HEADS * HEAD_PAD
Z_WIDTH = Q_RANK + KV_RANK + 2 * CG_WIDTH

LANES = 128
SUBLANES = 8
VMEM_LIMIT_BYTES = 56 * 1024 * 1024

TOK_TILE = 512
ATT_BLOCK = 1024
KEY_BLOCK = 512
ONES_ROWS = 16
FF_TILE = 256
KV_TILE = 2048

NT_DIMS = (((1,), (1,)), ((), ()))


def _params(semantics):
    return pltpu.CompilerParams(dimension_semantics=semantics,
                                vmem_limit_bytes=VMEM_LIMIT_BYTES)


def _rms(x):
    return x * lax.rsqrt(jnp.mean(x * x, axis=-1, keepdims=True) + EPS)


def _dot(a, b):
    return jnp.dot(a, b, preferred_element_type=F32)


def _dot_nt(a, b):
    return lax.dot_general(a, b, NT_DIMS, preferred_element_type=F32)


def _full(shape):
    zeros = (0,) * len(shape)
    return pl.BlockSpec(shape, lambda *_: zeros)


def _in_proj_kernel(x_ref, ca_ref, sb_ref, cost_ref, sint_ref,
                    g_attn_ref, w_in_ref, g_q_ref, w_qt_ref, g_qn_ref, g_qr_ref,
                    g_kv_ref, w_kvt_ref, g_kvt_ref, w_krt_ref, g_krt_ref, w_uk_ref, gm_ref, ga_ref, gb_ref,
                    qt_ref, k_ref, ckv_ref, vt_ref, krt_ref, u_ref, v_ref):
    tm = x_ref.shape[0]
    reps = tm // LANES
    cos, sin = cost_ref[...], sint_ref[...]
    hb = (_rms(x_ref[...]) * g_attn_ref[...]).astype(BF16)
    z = _dot(hb, w_in_ref[...])
    ckv = _rms(z[:, OFF_KV:OFF_KR]) * g_kv_ref[...]
    ckv_ref[...] = ckv
    kr_slot = z[:, Z_WIDTH:]
    ka = _dot(ckv.astype(BF16), w_uk_ref[...]) + jnp.tile(kr_slot, (1, N_HEADS))
    ct = _dot_nt(w_kvt_ref[...], hb)
    krt = _dot_nt(w_krt_ref[...], hb)
    qln = (_rms(z[:, :Q_RANK]) * g_q_ref[...]).astype(BF16)
    qt = _dot_nt(w_qt_ref[...], qln).reshape(N_HEADS, HEAD_PAD, tm)
    sq = (ka * ka).astype(BF16)
    gm = gm_ref[...]
    pair = 2 * HEAD_PAD
    ms = jnp.concatenate([_dot(sq[:, i * pair:(i + 1) * pair], gm) for i in range(N_HEADS // 2)], axis=1)
    krb = pltpu.roll(kr_slot, HEAD_PAD - QK_ROPE, axis=1)

    u_ref[...] = jax.nn.gelu(z[:, OFF_KR:OFF_KR + CG_WIDTH])
    v_ref[...] = jax.nn.gelu(z[:, OFF_KR + CG_WIDTH:OFF_KR + 2 * CG_WIDTH])
    ct = ct * lax.rsqrt(jnp.mean(ct * ct, axis=0, keepdims=True) + EPS)
    vt_ref[...] = (ct * jnp.tile(g_kvt_ref[...], (1, reps))).astype(BF16)
    krt = krt * lax.rsqrt(jnp.mean(krt * krt, axis=0, keepdims=True) + EPS) * jnp.tile(g_krt_ref[...], (1, reps))
    kx1, kx2 = krt[:ROPE_HALF], krt[ROPE_HALF:]
    krt_ref[0] = jnp.concatenate([kx1 * cos - kx2 * sin, kx1 * sin + kx2 * cos], axis=0)

    nope = qt[:, :QK_NOPE]
    rope = qt[:, QK_NOPE:QK_NOPE + QK_ROPE]
    g_qn = jnp.tile(g_qn_ref[...], (1, reps))[None]
    g_qr = jnp.tile(g_qr_ref[...], (1, reps))[None]
    nope = nope * lax.rsqrt(jnp.mean(nope * nope, axis=1, keepdims=True) + EPS) * (g_qn * Q_SCALE)
    rope = rope * lax.rsqrt(jnp.mean(rope * rope, axis=1, keepdims=True) + EPS) * (g_qr * Q_SCALE)
    x1, x2 = rope[:, :ROPE_HALF], rope[:, ROPE_HALF:]
    cos, sin = cos[None], sin[None]
    pad = jnp.zeros((N_HEADS, HEAD_PAD - QK_NOPE - QK_ROPE, tm), F32)
    qt_ref[...] = jnp.concatenate([nope, x1 * cos - x2 * sin, x1 * sin + x2 * cos, pad], axis=1).astype(BF16)

    ca = jnp.tile(ca_ref[...], (1, N_HEADS))
    sb = jnp.tile(sb_ref[...], (1, N_HEADS))
    k = lax.rsqrt(ms + EPS) * (ka * ga_ref[...] * ca + jnp.tile(krb, (1, N_HEADS)) * gb_ref[...] * sb)
    k_ref[...] = k.astype(BF16)


def _in_proj(x, tabs, w, tm):
    n = x.shape[0]
    ca, sb, cost, sint = tabs
    pos_tiles = ca.shape[0] // tm
    tok = lambda width: pl.BlockSpec((tm, width), lambda i: (i, 0))
    tok_t = lambda rows: pl.BlockSpec((rows, tm), lambda i: (0, i))
    pos = pl.BlockSpec((tm, HEAD_PAD), lambda i: (i % pos_tiles, 0))
    pos_t = pl.BlockSpec((ROPE_HALF, tm), lambda i: (0, i % pos_tiles))
    weights = [w['g_attn'], w['w_in'], w['g_q'], w['w_qt'], w['g_qn'], w['g_qr'],
               w['g_kv'], w['w_kvt'], w['g_kvt'], w['w_krt'], w['g_krt'], w['w_uk'], w['gm'], w['ga'], w['gb']]
    return pl.pallas_call(
        _in_proj_kernel,
        grid=(n // tm,),
        in_specs=[tok(D_MODEL), pos, pos, pos_t, pos_t]
                 + [_full(a.shape) for a in weights],
        out_specs=[pl.BlockSpec((N_HEADS, HEAD_PAD, tm), lambda i: (0, 0, i)),
                   tok(QK_WIDTH), tok(KV_RANK), tok_t(KV_RANK),
                   pl.BlockSpec((1, QK_ROPE, tm), lambda i: (i // pos_tiles, 0, i % pos_tiles)),
                   tok(CG_WIDTH), tok(CG_WIDTH)],
        out_shape=[jax.ShapeDtypeStruct((N_HEADS, HEAD_PAD, n), BF16),
                   jax.ShapeDtypeStruct((n, QK_WIDTH), BF16),
                   jax.ShapeDtypeStruct((n, KV_RANK), F32),
                   jax.ShapeDtypeStruct((KV_RANK, n), BF16),
                   jax.ShapeDtypeStruct((n // (pos_tiles * tm), QK_ROPE, pos_tiles * tm), F32),
                   jax.ShapeDtypeStruct((n, CG_WIDTH), F32),
                   jax.ShapeDtypeStruct((n, CG_WIDTH), F32)],
        compiler_params=_params(("parallel",)),
        name="in_proj",
    )(x, ca, sb, cost, sint, *weights)


def _prompt_attn_kernel(qt_ref, k_ref, vt_ref, o_ref, m_s, acc_s, s_s):
    qi = pl.program_id(1)
    kb = KEY_BLOCK
    per_q = ATT_BLOCK // kb
    m_s[...] = jnp.full(m_s.shape, -jnp.inf, F32)
    acc_s[...] = jnp.zeros(acc_s.shape, F32)
    ones = jnp.ones((ONES_ROWS, kb), BF16)

    def key_block(off, first_q, masked):
        cols = slice(first_q, ATT_BLOCK)
        vals = jnp.concatenate([vt_ref[:, pl.ds(off, kb)], ones], axis=0)
        if masked:
            key = lax.broadcasted_iota(jnp.int32, (kb, ATT_BLOCK - first_q), 0)
            qry = lax.broadcasted_iota(jnp.int32, (kb, ATT_BLOCK - first_q), 1)
            visible = key <= qry

        def score_stage(h):
            s = _dot(k_ref[pl.ds(off, kb), h * HEAD_PAD:(h + 1) * HEAD_PAD], qt_ref[h, :, cols])
            if masked:
                s = jnp.where(visible, s, -jnp.inf)
            m_old = m_s[h, :, cols]
            m_new = jnp.maximum(m_old, jnp.max(s, axis=0, keepdims=True))
            m_s[h, :, cols] = m_new
            s_s[h % 2, :, cols] = s
            return m_old, m_new

        def value_stage(h, m_old, m_new):
            p = jnp.exp2(s_s[h % 2, :, cols] - m_new)
            acc_s[h, :, cols] = jnp.exp2(m_old - m_new) * acc_s[h, :, cols] + _dot(vals, p.astype(BF16))

        pending = score_stage(0)
        for h in range(N_HEADS):
            nxt = score_stage(h + 1) if h + 1 < N_HEADS else None
            value_stage(h, *pending)
            pending = nxt

    def body(j, carry):
        key_block(pl.multiple_of(j * kb, kb), 0, False)
        return carry

    lax.fori_loop(0, qi * per_q, body, 0)
    for d in range(per_q):
        key_block(pl.multiple_of(qi * ATT_BLOCK + d * kb, kb), d * kb, True)
    for h in range(N_HEADS):
        acc = acc_s[h]
        o_t = acc[:KV_RANK] / acc[KV_RANK:KV_RANK + 1]
        o_ref[:, h * HEAD_PAD:(h + 1) * HEAD_PAD] = o_t.T.astype(BF16)


def _prompt_attn(qt, k, vt, batch, seq):
    nq = seq // ATT_BLOCK
    return pl.pallas_call(
        _prompt_attn_kernel,
        grid=(batch, nq),
        in_specs=[pl.BlockSpec((N_HEADS, HEAD_PAD, ATT_BLOCK), lambda b, i: (0, 0, b * nq + i)),
                  pl.BlockSpec((seq, QK_WIDTH), lambda b, i: (b, 0)),
                  pl.BlockSpec((KV_RANK, seq), lambda b, i: (0, b))],
        out_specs=pl.BlockSpec((ATT_BLOCK, N_HEADS * KV_RANK), lambda b, i: (b * nq + i, 0)),
        out_shape=jax.ShapeDtypeStruct((batch * seq, N_HEADS * KV_RANK), BF16),
        scratch_shapes=[pltpu.VMEM((N_HEADS, 1, ATT_BLOCK), F32),
                        pltpu.VMEM((N_HEADS, KV_RANK + ONES_ROWS, ATT_BLOCK), F32),
                        pltpu.VMEM((2, KEY_BLOCK, ATT_BLOCK), F32)],
        compiler_params=_params(("parallel", "parallel")),
        name="prompt_attn",
    )(qt, k, vt)


def _absorb_q_kernel(qn_ref, g_ref, wukt_ref, qa_ref):
    qg = qn_ref[...] * g_ref[...]
    lane = lax.broadcasted_iota(jnp.int32, qg.shape, 1)
    for h in range(N_HEADS):
        q_h = jnp.where((lane >= h * QK_NOPE) & (lane < (h + 1) * QK_NOPE), qg, 0.0)
        qa_ref[:, h * KV_RANK:(h + 1) * KV_RANK] = jnp.dot(
            q_h, wukt_ref[...], precision=lax.Precision.HIGHEST, preferred_element_type=F32)


def _absorb_q(qn, w):
    n_samp = qn.shape[0]
    args = [qn, w['g_kn_row'], w['w_ukt_f32']]
    return pl.pallas_call(
        _absorb_q_kernel,
        grid=(1,),
        in_specs=[_full(a.shape) for a in args],
        out_specs=_full((n_samp, N_HEADS * KV_RANK)),
        out_shape=jax.ShapeDtypeStruct((n_samp, N_HEADS * KV_RANK), F32),
        compiler_params=_params(("arbitrary",)),
        name="absorb_q",
    )(*args)


def _sample_attn_kernel(pt_ref, q_ref, qa_ref, kown_ref, cown_ref, wukb_ref,
                        cache_c, cache_r, o_ref,
                        wext, cbuf, rbuf, sems, cb_all, s_all, *, n_pages):
    b = pl.program_id(0)
    last = pl.num_programs(0) - 1
    slot = lax.rem(b, 2)
    past = n_pages * PAGE_SIZE
    n_tiles = past // KV_TILE
    pages_per_tile = KV_TILE // PAGE_SIZE
    nw = N_HEADS * QK_NOPE
    nxt = jnp.minimum(b + 1, last)

    def start_page(bb, sl, i):
        page = pt_ref[bb * n_pages + i]
        off = pl.multiple_of(i * PAGE_SIZE, PAGE_SIZE)
        pltpu.make_async_copy(cache_c.at[page], cbuf.at[sl, pl.ds(off, PAGE_SIZE)], sems.at[0, sl]).start()
        pltpu.make_async_copy(cache_r.at[page], rbuf.at[sl, :, pl.ds(off, PAGE_SIZE)],
                              sems.at[1, sl]).start(priority=1)

    def wait_pages(sl):
        pltpu.make_async_copy(cbuf.at[sl], cbuf.at[sl], sems.at[0, sl]).wait()
        pltpu.make_async_copy(rbuf.at[sl], rbuf.at[sl], sems.at[1, sl]).wait()

    @pl.when(b == 0)
    def _():
        def issue(i, carry):
            start_page(0, 0, i)
            return carry
        lax.fori_loop(0, n_pages, issue, 0)
        wext[:nw, :] = wukb_ref[...]

    wext[nw:, :] = jnp.concatenate(
        [qa_ref[0], jnp.zeros((wext.shape[0] - nw - N_HEADS, KV_RANK), F32)], axis=0).astype(BF16)
    qr = q_ref[0][:, QK_NOPE:QK_NOPE + QK_ROPE].astype(BF16)

    def scores(c_blk, krt_blk):
        tk = c_blk.shape[0]
        cb = c_blk.astype(BF16)
        knt = _dot_nt(wext[...], cb)
        kn3 = knt[:nw].reshape(N_HEADS, QK_NOPE, tk)
        r = lax.rsqrt(jnp.sum(kn3 * kn3, axis=1) * (1.0 / QK_NOPE) + EPS)
        s_rope = _dot(qr, krt_blk.astype(BF16))
        return cb, knt[nw:nw + N_HEADS] * r + s_rope

    s_own = jnp.sum(q_ref[0] * kown_ref[0].astype(F32), axis=-1, keepdims=True)

    wait_pages(slot)
    for j in range(n_tiles):
        for i in range(pages_per_tile):
            start_page(nxt, 1 - slot, j * pages_per_tile + i)
        keys = slice(j * KV_TILE, (j + 1) * KV_TILE)
        cb, s = scores(cbuf[slot, keys, :], rbuf[slot, :, keys])
        cb_all[keys, :] = cb
        s_all[:, keys] = s

    s = s_all[...]
    m = jnp.maximum(jnp.max(s, axis=-1, keepdims=True), s_own)
    p = jnp.exp2(s - m)
    p_own = jnp.exp2(s_own - m)
    l = jnp.sum(p, axis=-1, keepdims=True) + p_own
    round_bf16 = lambda a: a.astype(BF16).astype(F32)
    acc = _dot(p.astype(BF16), cb_all[...]) + round_bf16(p_own) * round_bf16(cown_ref[0])
    o_ref[0] = acc / l

    @pl.when(b == last)
    def _():
        wait_pages(1 - slot)


def _sample_attn(page_table, q, qa, k_own, c_own, w, cache_c, cache_rt):
    n_samp, n_pages = page_table.shape
    past = n_pages * PAGE_SIZE
    wext_rows = N_HEADS * QK_NOPE + 2 * SUBLANES
    per_sample = lambda a: pl.BlockSpec((1,) + a.shape[1:], lambda b, pt: (b, 0, 0))
    grid_spec = pltpu.PrefetchScalarGridSpec(
        num_scalar_prefetch=1,
        grid=(n_samp,),
        in_specs=[per_sample(q), per_sample(qa), per_sample(k_own), per_sample(c_own),
                  pl.BlockSpec(w['w_ukt'].shape, lambda b, pt: (0, 0)),
                  pl.BlockSpec(memory_space=pl.ANY),
                  pl.BlockSpec(memory_space=pl.ANY)],
        out_specs=pl.BlockSpec((1, N_HEADS, KV_RANK), lambda b, pt: (b, 0, 0)),
        scratch_shapes=[pltpu.VMEM((wext_rows, KV_RANK), BF16),
                        pltpu.VMEM((2, past, KV_RANK), F32),
                        pltpu.VMEM((2, QK_ROPE, past), F32),
                        pltpu.SemaphoreType.DMA((2, 2)),
                        pltpu.VMEM((past, KV_RANK), BF16),
                        pltpu.VMEM((N_HEADS, past), F32)])
    return pl.pallas_call(
        functools.partial(_sample_attn_kernel, n_pages=n_pages),
        grid_spec=grid_spec,
        out_shape=jax.ShapeDtypeStruct((n_samp, N_HEADS, KV_RANK), F32),
        compiler_params=_params(("arbitrary",)),
        name="sample_attn",
    )(page_table.reshape(-1), q, qa, k_own, c_own, w['w_ukt'], cache_c, cache_rt)


def _mix_proj_kernel(x_ref, o_ref, u_ref, v_ref, wuv_ref, ws_ref, bias_ref, wo_ref, g_ref,
                     h_ref, hn_ref, *, single_token):
    tm = x_ref.shape[0]
    attn = _dot(o_ref[...], wuv_ref[...])
    v = v_ref[...]
    if single_token:
        mixed = v * ws_ref[...] + bias_ref[...]
    else:
        vb = v.astype(BF16)
        row = lax.broadcasted_iota(jnp.int32, (CHUNK, CHUNK), 0)
        col = lax.broadcasted_iota(jnp.int32, (CHUNK, CHUNK), 1)
        w_tril = [jnp.where(col <= row, ws_ref[g], 0.0).astype(BF16) for g in range(N_CG)]
        low_half = lax.broadcasted_iota(jnp.int32, (CHUNK, LANES), 1) < CG_DIM
        chunks = []
        for c in range(tm // CHUNK):
            cols = []
            for j in range(CG_WIDTH // LANES):
                vp = vb[c * CHUNK:(c + 1) * CHUNK, j * LANES:(j + 1) * LANES]
                cols.append(jnp.where(low_half, _dot(w_tril[2 * j], vp), _dot(w_tril[2 * j + 1], vp)))
            chunks.append(jnp.concatenate(cols, axis=1) + bias_ref[...])
        mixed = jnp.concatenate(chunks, axis=0)
    sg = u_ref[...] * mixed
    width = attn.shape[1]
    h = x_ref[...] + _dot(attn.astype(BF16), wo_ref[:width, :]) + _dot(sg.astype(BF16), wo_ref[width:, :])
    h_ref[...] = h
    hn_ref[...] = (_rms(h) * g_ref[...]).astype(BF16)


def _conv_ffn_kernel(*refs, single_token):
    if single_token:
        h_ref, hn_ref, win_ref, cw_ref, cb_ref, wout_ref, h0_ref, h1_ref, out_ref, a_ref, act_s = refs
    else:
        h_ref, hn_ref, win_ref, cw_ref, cb_ref, wout_ref, out_ref, last_ref, act_s, carry = refs
    tm = hn_ref.shape[0]
    hn = hn_ref[...]

    def conv(cols):
        a = _dot(hn, win_ref[:, cols])
        if single_token:
            a2, a1 = h0_ref[:, cols], h1_ref[:, cols]
            a_ref[0, :, cols] = a1
            a_ref[1, :, cols] = a
        else:
            ext = jnp.concatenate([carry[:, cols], a], axis=0)
            a1 = ext[SUBLANES - 1:SUBLANES - 1 + tm]
            a2 = ext[SUBLANES - 2:SUBLANES - 2 + tm]
            tail = a[tm - SUBLANES:]
            carry[:, cols] = tail
            last_ref[0, :, cols] = tail[SUBLANES - (CONV_W - 1):]
        return cb_ref[:, cols] + cw_ref[0:1, cols] * a2 + cw_ref[1:2, cols] * a1 + cw_ref[2:3, cols] * a

    for j in range(D_FF // FF_TILE):
        gate = slice(j * FF_TILE, (j + 1) * FF_TILE)
        up = slice(D_FF + j * FF_TILE, D_FF + (j + 1) * FF_TILE)
        act_s[:, gate] = (jax.nn.silu(conv(gate)) * conv(up)).astype(BF16)
    out_ref[...] = h_ref[...] + _dot(act_s[...], wout_ref[...])


def _ple_kernel(h_ref, p_ref, g_ref, wgate_ref, wproj_ref, gpost_ref, y_ref):
    h = h_ref[...]
    gate = jax.nn.sigmoid(_dot((_rms(h) * g_ref[...]).astype(BF16), wgate_ref[...]))
    e = _rms(_dot(p_ref[...].astype(BF16), wproj_ref[...])) * gpost_ref[...]
    y_ref[...] = h + gate * e


def _post_attn_kernel(*refs, single_token):
    x_ref, o_ref, u_ref, v_ref, p_ref = refs[:5]
    mix_w, ffn_w, ple_w = refs[5:10], refs[10:14], refs[14:18]
    if single_token:
        h0_ref, h1_ref, y_ref, a_ref, act_s, h_s, hn_s, h2_s = refs[18:]
        _mix_proj_kernel(x_ref, o_ref, u_ref, v_ref, *mix_w, h_s, hn_s, single_token=True)
        _conv_ffn_kernel(h_s, hn_s, *ffn_w, h0_ref, h1_ref, h2_s, a_ref, act_s, single_token=True)
        _ple_kernel(h2_s, p_ref, *ple_w, y_ref)
        return
    y_ref, last_ref, act_s, carry, h_s, hn_s, h2_s = refs[18:]

    @pl.when(pl.program_id(1) == 0)
    def _():
        carry[...] = jnp.zeros(carry.shape, F32)

    _mix_proj_kernel(x_ref, o_ref, u_ref, v_ref, *mix_w, h_s, hn_s, single_token=False)
    _conv_ffn_kernel(h_s, hn_s, *ffn_w, h2_s, last_ref, act_s, carry, single_token=False)
    _ple_kernel(h2_s, p_ref, *ple_w, y_ref)


def _post_attn(x, o_lat, u, v, p, w, batch, seq, tm, hist=None):
    single_token = hist is not None
    nt = seq // tm
    tok = lambda width: pl.BlockSpec((tm, width), lambda b, t: (b * nt + t, 0))
    resident = lambda a: pl.BlockSpec(a.shape, lambda b, t: (0,) * a.ndim, pipeline_mode=pl.Buffered(1))
    ws, bias = (w['ws_diag'], w['bias_first']) if single_token else (w['w_s'], w['bias_tab'])
    weights = [w['w_uv_bd'], ws, bias, w['w_o'], w['g_ffn'],
               w['w_ff_in'], w['conv_w'], w['conv_b'], w['w_ff_out'],
               w['g_ple'], w['w_gate'], w['w_proj'], w['g_post']]
    in_specs = ([tok(D_MODEL), tok(N_HEADS * KV_RANK), tok(CG_WIDTH), tok(CG_WIDTH), tok(PLE_DIM)]
                + [resident(a) for a in weights])
    args = [x, o_lat, u, v, p] + weights
    scratch = [pltpu.VMEM((tm, D_FF), BF16)]
    if single_token:
        in_specs += [resident(hist[0]), resident(hist[1])]
        args += list(hist)
        last_spec = pl.BlockSpec((CONV_W - 1, tm, 2 * D_FF), lambda b, t: (0, 0, 0))
        last_shape = jax.ShapeDtypeStruct((CONV_W - 1, tm, 2 * D_FF), F32)
    else:
        last_spec = pl.BlockSpec((1, CONV_W - 1, 2 * D_FF), lambda b, t: (b * nt + t, 0, 0))
        last_shape = jax.ShapeDtypeStruct((batch * nt, CONV_W - 1, 2 * D_FF), F32)
        scratch.append(pltpu.VMEM((SUBLANES, 2 * D_FF), F32))
    scratch += [pltpu.VMEM((tm, D_MODEL), F32), pltpu.VMEM((tm, D_MODEL), BF16),
                pltpu.VMEM((tm, D_MODEL), F32)]
    return pl.pallas_call(
        functools.partial(_post_attn_kernel, single_token=single_token),
        grid=(batch, nt),
        in_specs=in_specs,
        out_specs=[tok(D_MODEL), last_spec],
        out_shape=[jax.ShapeDtypeStruct((batch * seq, D_MODEL), F32), last_shape],
        scratch_shapes=scratch,
        compiler_params=_params(("arbitrary", "arbitrary")),
        name="post_attn_single" if single_token else "post_attn",
    )(*args)


def _rope_tables(pos):
    inv = ROPE_THETA ** (-jnp.arange(ROPE_HALF, dtype=F32) * (2.0 / QK_ROPE))
    ang = pos.astype(F32)[:, None] * inv[None, :]
    cos, sin = jnp.cos(ang), jnp.sin(ang)
    n = pos.shape[0]
    ones = jnp.ones((n, QK_NOPE), F32)
    zq = jnp.zeros((n, QK_NOPE), F32)
    zp = jnp.zeros((n, HEAD_PAD - QK_NOPE - QK_ROPE), F32)
    ca = jnp.concatenate([ones, cos, cos, zp], axis=1)
    sb = jnp.concatenate([zq, -sin, sin, zp], axis=1)
    return ca, sb, cos.T, sin.T


def _prep_weights(i, attn_norm_w, w_in, q_norm_w, w_uq, q_nope_norm_w, q_rope_norm_w, kv_norm_w,
                  k_rope_norm_w, w_uk, k_nope_norm_w, w_uv, w_s, b_s, w_o, ffn_norm_w, w_ff_in,
                  conv_w, conv_b, w_ff_out, ple_norm_w, w_ple_gate, w_ple_proj, ple_post_norm_w):
    row = lambda g: g[i][None, :]
    swap = lambda a: jnp.concatenate([a[..., ROPE_HALF:], a[..., :ROPE_HALF]], axis=-1)
    win = w_in[i]
    wq = w_uq[i].reshape(Q_RANK, N_HEADS, QK_NOPE + QK_ROPE)
    nope, rope = wq[..., :QK_NOPE], wq[..., QK_NOPE:]
    pad = HEAD_PAD - QK_NOPE - QK_ROPE
    w_qt = jnp.concatenate([nope, rope, jnp.zeros((Q_RANK, N_HEADS, pad), F32)],
                           axis=-1).reshape(Q_RANK, QK_WIDTH).T
    col = lambda g: jnp.broadcast_to(g[:, None], (g.shape[0], LANES))
    kr_cols = win[:, OFF_KR:OFF_U]
    w_in_e = jnp.concatenate([win[:, :OFF_KR], win[:, OFF_U:], jnp.zeros((D_MODEL, QK_NOPE), F32),
                              kr_cols, swap(kr_cols)], axis=1)
    w_uk_e = jnp.concatenate([w_uk[i], jnp.zeros((KV_RANK, N_HEADS, HEAD_PAD - QK_NOPE), F32)],
                             axis=-1).reshape(KV_RANK, QK_WIDTH)
    gn, gr = k_nope_norm_w[i], k_rope_norm_w[i]
    ga = jnp.tile(jnp.concatenate([gn, gr, jnp.zeros((pad,), F32)]), N_HEADS)[None, :]
    gb = jnp.tile(jnp.concatenate([jnp.zeros((QK_NOPE,), F32), swap(gr), jnp.zeros((pad,), F32)]),
                  N_HEADS)[None, :]
    lane = jnp.arange(2 * HEAD_PAD)
    slot, off = lane // HEAD_PAD, lane % HEAD_PAD
    grp = jnp.where(off < QK_NOPE, 0, jnp.where(off < QK_NOPE + QK_ROPE, 1, 2))
    same = (slot[:, None] == slot[None, :]) & (grp[:, None] == grp[None, :]) & (grp[:, None] < 2)
    gm = jnp.where(same, jnp.where(grp[:, None] == 0, 1.0 / QK_NOPE, 1.0 / QK_ROPE), 0.0)
    w_ukt = w_uk[i].reshape(KV_RANK, N_HEADS * QK_NOPE).T
    eye = jnp.eye(N_HEADS, dtype=F32)
    w_uv_bd = jnp.einsum('rhd,hg->hrgd', w_uv[i], eye).reshape(N_HEADS * KV_RANK, N_HEADS * V_DIM)
    return {
        'g_attn': row(attn_norm_w),
        'w_in': w_in_e.astype(BF16),
        'g_q': row(q_norm_w), 'w_qt': w_qt.astype(BF16),
        'g_qn': col(q_nope_norm_w[i]), 'g_qr': col(q_rope_norm_w[i]),
        'g_kv': row(kv_norm_w), 'w_kvt': win[:, OFF_KV:OFF_KR].T.astype(BF16), 'g_kvt': col(kv_norm_w[i]),
        'w_krt': kr_cols.T.astype(BF16), 'g_krt': col(k_rope_norm_w[i]),
        'w_uk': w_uk_e.astype(BF16), 'gm': gm.astype(BF16), 'ga': ga, 'gb': gb,
        'w_ukt': w_ukt.astype(BF16), 'w_ukt_f32': w_ukt,
        'g_kn_row': jnp.tile(k_nope_norm_w[i], N_HEADS)[None, :],
        'w_uv_bd': w_uv_bd.astype(BF16),
        'w_s': w_s[i],
        'bias_tab': jnp.repeat(b_s[i].T, CG_DIM, axis=1),
        'ws_diag': jnp.repeat(w_s[i][:, 0, 0], CG_DIM)[None, :],
        'bias_first': jnp.repeat(b_s[i][:, 0], CG_DIM)[None, :],
        'w_o': w_o[i].astype(BF16), 'g_ffn': row(ffn_norm_w),
        'w_ff_in': w_ff_in[i].astype(BF16), 'conv_w': conv_w[i], 'conv_b': row(conv_b),
        'w_ff_out': w_ff_out[i].astype(BF16),
        'g_ple': row(ple_norm_w), 'w_gate': w_ple_gate[i].astype(BF16),
        'w_proj': w_ple_proj[i].astype(BF16), 'g_post': row(ple_post_norm_w),
    }


def kernel(x_prompt, x_sample, cache_ckv, cache_krope, state_conv, page_table, p_prompt, p_sample,
           attn_norm_w, w_in, q_norm_w, w_uq, q_nope_norm_w, q_rope_norm_w, kv_norm_w, k_rope_norm_w,
           w_uk, k_nope_norm_w, w_uv, w_s, b_s, w_o, ffn_norm_w, w_ff_in, conv_w, conv_b, w_ff_out,
           ple_norm_w, w_ple_gate, w_ple_proj, ple_post_norm_w):
    batch, seq, _ = x_prompt.shape
    n_samp, seq_s, _ = x_sample.shape
    depth = w_in.shape[0]
    assert depth == 1 and seq_s == 1 and seq % TOK_TILE == 0 and seq % CHUNK == 0
    past_len = page_table.shape[1] * PAGE_SIZE
    assert past_len % KV_TILE == 0
    w = _prep_weights(0, attn_norm_w, w_in, q_norm_w, w_uq, q_nope_norm_w, q_rope_norm_w, kv_norm_w,
                      k_rope_norm_w, w_uk, k_nope_norm_w, w_uv, w_s, b_s, w_o, ffn_norm_w, w_ff_in,
                      conv_w, conv_b, w_ff_out, ple_norm_w, w_ple_gate, w_ple_proj, ple_post_norm_w)

    n_tok = batch * seq
    tabs = _rope_tables(jnp.arange(seq, dtype=jnp.int32))
    xp = x_prompt.reshape(n_tok, D_MODEL)
    qt, k, ckv, vt, krt, u, v = _in_proj(xp, tabs, w, TOK_TILE)
    o_lat = _prompt_attn(qt, k, vt, batch, seq)
    y_prompt, a_last = _post_attn(xp, o_lat, u, v, p_prompt[0].reshape(n_tok, PLE_DIM), w,
                                  batch, seq, TOK_TILE)
    y_prompt = y_prompt.reshape(batch, seq, D_MODEL)
    n_keep = (seq - 1) % CHUNK + 1
    new_ckv_p = ckv.reshape(1, batch, seq, KV_RANK)
    new_kr_p = jnp.swapaxes(krt, 1, 2)[None]
    new_v_p = v.reshape(batch, seq, CG_WIDTH)[:, seq - n_keep:][None]
    new_conv_p = a_last.reshape(batch, seq // TOK_TILE, CONV_W - 1, 2 * D_FF)[:, -1][None]

    tabs_s = _rope_tables(jnp.full((n_samp,), past_len, dtype=jnp.int32))
    xs = x_sample.reshape(n_samp, D_MODEL)
    qt_s, k_s, ckv_s, _, krt_s, u_s, v_s = _in_proj(xs, tabs_s, w, n_samp)
    q3 = jnp.transpose(qt_s, (2, 0, 1)).astype(F32)
    kr_s = krt_s[0].T
    qa = _absorb_q(q3[:, :, :QK_NOPE].reshape(n_samp, N_HEADS * QK_NOPE), w)
    cache_rt = jnp.swapaxes(cache_krope[0], 1, 2)
    o_s = _sample_attn(page_table, q3, qa.reshape(n_samp, N_HEADS, KV_RANK),
                       k_s.reshape(n_samp, N_HEADS, HEAD_PAD), ckv_s.reshape(n_samp, 1, KV_RANK),
                       w, cache_ckv[0], cache_rt)
    o_s = o_s.reshape(n_samp, N_HEADS * KV_RANK).astype(BF16)
    hist = (state_conv[0, :, 0, :], state_conv[0, :, 1, :])
    y_sample, a_s = _post_attn(xs, o_s, u_s, v_s, p_sample[0].reshape(n_samp, PLE_DIM), w,
                               1, n_samp, n_samp, hist=hist)
    y_sample = y_sample.reshape(n_samp, 1, D_MODEL)
    new_conv_s = jnp.swapaxes(a_s, 0, 1)[None]

    return (y_prompt, y_sample, new_ckv_p, new_kr_p,
            ckv_s.reshape(1, n_samp, 1, KV_RANK), kr_s.reshape(1, n_samp, 1, QK_ROPE),
            new_v_p, v_s.reshape(1, n_samp, 1, CG_WIDTH), new_conv_p, new_conv_s)
```

```python
import functools

import jax
import jax.numpy as jnp
from jax import lax
from jax.experimental import pallas as pl
from jax.experimental.pallas import tpu as pltpu

F32 = jnp.float32
BF16 = jnp.bfloat16

D_MODEL = 1024
N_HEADS = 8
QK_NOPE = 64
QK_ROPE = 32
ROPE_HALF = QK_ROPE // 2
V_DIM = 64
Q_RANK = 256
KV_RANK = 128
CHUNK = 128
N_CG = 8
CG_WIDTH = 512
CG_DIM = 64
D_FF = 2816
CONV_W = 3
PLE_DIM = 256
PAGE_SIZE = 128
ROPE_THETA = 10000.0
EPS = 1e-6
SCALE = (QK_NOPE + QK_ROPE) ** -0.5
Q_SCALE = SCALE * 1.4426950408889634
OFF_KV = Q_RANK
OFF_KR = OFF_KV + KV_RANK
OFF_U = OFF_KR + QK_ROPE
OFF_V = OFF_U + CG_WIDTH
IN_WIDTH = OFF_V + CG_WIDTH
HEAD_PAD = 128
QK_WIDTH = N_HEADS * HEAD_PAD
Z_WIDTH = Q_RANK + KV_RANK + 2 * CG_WIDTH

LANES = 128
SUBLANES = 8
VMEM_LIMIT_BYTES = 56 * 1024 * 1024

TOK_TILE = 512
ATT_BLOCK = 1024
KEY_BLOCK = 512
ONES_ROWS = 16
FF_TILE = 256
KV_TILE = 2048

NT_DIMS = (((1,), (1,)), ((), ()))


def _params(semantics):
    return pltpu.CompilerParams(dimension_semantics=semantics,
                                vmem_limit_bytes=VMEM_LIMIT_BYTES)


def _rms(x):
    return x * lax.rsqrt(jnp.mean(x * x, axis=-1, keepdims=True) + EPS)


def _dot(a, b):
    return jnp.dot(a, b, preferred_element_type=F32)


def _dot_nt(a, b):
    return lax.dot_general(a, b, NT_DIMS, preferred_element_type=F32)


def _full(shape):
    zeros = (0,) * len(shape)
    return pl.BlockSpec(shape, lambda *_: zeros)


def _in_proj_kernel(x_ref, ca_ref, sb_ref, cost_ref, sint_ref,
                    g_attn_ref, w_in_ref, g_q_ref, w_qt_ref, g_qn_ref, g_qr_ref,
                    g_kv_ref, w_kvt_ref, g_kvt_ref, w_krt_ref, g_krt_ref, w_uk_ref, gm_ref, ga_ref, gb_ref,
                    qt_ref, k_ref, ckv_ref, vt_ref, krt_ref, u_ref, v_ref):
    tm = x_ref.shape[0]
    reps = tm // LANES
    cos, sin = cost_ref[...], sint_ref[...]
    hb = (_rms(x_ref[...]) * g_attn_ref[...]).astype(BF16)
    z = _dot(hb, w_in_ref[...])
    ckv = _rms(z[:, OFF_KV:OFF_KR]) * g_kv_ref[...]
    ckv_ref[...] = ckv
    kr_slot = z[:, Z_WIDTH:]
    ka = _dot(ckv.astype(BF16), w_uk_ref[...]) + jnp.tile(kr_slot, (1, N_HEADS))
    ct = _dot_nt(w_kvt_ref[...], hb)
    krt = _dot_nt(w_krt_ref[...], hb)
    qln = (_rms(z[:, :Q_RANK]) * g_q_ref[...]).astype(BF16)
    qt = _dot_nt(w_qt_ref[...], qln).reshape(N_HEADS, HEAD_PAD, tm)
    sq = (ka * ka).astype(BF16)
    gm = gm_ref[...]
    pair = 2 * HEAD_PAD
    ms = jnp.concatenate([_dot(sq[:, i * pair:(i + 1) * pair], gm) for i in range(N_HEADS // 2)], axis=1)
    krb = pltpu.roll(kr_slot, HEAD_PAD - QK_ROPE, axis=1)

    u_ref[...] = jax.nn.gelu(z[:, OFF_KR:OFF_KR + CG_WIDTH])
    v_ref[...] = jax.nn.gelu(z[:, OFF_KR + CG_WIDTH:OFF_KR + 2 * CG_WIDTH])
    ct = ct * lax.rsqrt(jnp.mean(ct * ct, axis=0, keepdims=True) + EPS)
    vt_ref[...] = (ct * jnp.tile(g_kvt_ref[...], (1, reps))).astype(BF16)
    krt = krt * lax.rsqrt(jnp.mean(krt * krt, axis=0, keepdims=True) + EPS) * jnp.tile(g_krt_ref[...], (1, reps))
    kx1, kx2 = krt[:ROPE_HALF], krt[ROPE_HALF:]
    krt_ref[0] = jnp.concatenate([kx1 * cos - kx2 * sin, kx1 * sin + kx2 * cos], axis=0)

    nope = qt[:, :QK_NOPE]
    rope = qt[:, QK_NOPE:QK_NOPE + QK_ROPE]
    g_qn = jnp.tile(g_qn_ref[...], (1, reps))[None]
    g_qr = jnp.tile(g_qr_ref[...], (1, reps))[None]
    nope = nope * lax.rsqrt(jnp.mean(nope * nope, axis=1, keepdims=True) + EPS) * (g_qn * Q_SCALE)
    rope = rope * lax.rsqrt(jnp.mean(rope * rope, axis=1, keepdims=True) + EPS) * (g_qr * Q_SCALE)
    x1, x2 = rope[:, :ROPE_HALF], rope[:, ROPE_HALF:]
    cos, sin = cos[None], sin[None]
    pad = jnp.zeros((N_HEADS, HEAD_PAD - QK_NOPE - QK_ROPE, tm), F32)
    qt_ref[...] = jnp.concatenate([nope, x1 * cos - x2 * sin, x1 * sin + x2 * cos, pad], axis=1).astype(BF16)

    ca = jnp.tile(ca_ref[...], (1, N_HEADS))
    sb = jnp.tile(sb_ref[...], (1, N_HEADS))
    k = lax.rsqrt(ms + EPS) * (ka * ga_ref[...] * ca + jnp.tile(krb, (1, N_HEADS)) * gb_ref[...] * sb)
    k_ref[...] = k.astype(BF16)


def _in_proj(x, tabs, w, tm):
    n = x.shape[0]
    ca, sb, cost, sint = tabs
    pos_tiles = ca.shape[0] // tm
    tok = lambda width: pl.BlockSpec((tm, width), lambda i: (i, 0))
    tok_t = lambda rows: pl.BlockSpec((rows, tm), lambda i: (0, i))
    pos = pl.BlockSpec((tm, HEAD_PAD), lambda i: (i % pos_tiles, 0))
    pos_t = pl.BlockSpec((ROPE_HALF, tm), lambda i: (0, i % pos_tiles))
    weights = [w['g_attn'], w['w_in'], w['g_q'], w['w_qt'], w['g_qn'], w['g_qr'],
               w['g_kv'], w['w_kvt'], w['g_kvt'], w['w_krt'], w['g_krt'], w['w_uk'], w['gm'], w['ga'], w['gb']]
    return pl.pallas_call(
        _in_proj_kernel,
        grid=(n // tm,),
        in_specs=[tok(D_MODEL), pos, pos, pos_t, pos_t]
                 + [_full(a.shape) for a in weights],
        out_specs=[pl.BlockSpec((N_HEADS, HEAD_PAD, tm), lambda i: (0, 0, i)),
                   tok(QK_WIDTH), tok(KV_RANK), tok_t(KV_RANK),
                   pl.BlockSpec((1, QK_ROPE, tm), lambda i: (i // pos_tiles, 0, i % pos_tiles)),
                   tok(CG_WIDTH), tok(CG_WIDTH)],
        out_shape=[jax.ShapeDtypeStruct((N_HEADS, HEAD_PAD, n), BF16),
                   jax.ShapeDtypeStruct((n, QK_WIDTH), BF16),
                   jax.ShapeDtypeStruct((n, KV_RANK), F32),
                   jax.ShapeDtypeStruct((KV_RANK, n), BF16),
                   jax.ShapeDtypeStruct((n // (pos_tiles * tm), QK_ROPE, pos_tiles * tm), F32),
                   jax.ShapeDtypeStruct((n, CG_WIDTH), F32),
                   jax.ShapeDtypeStruct((n, CG_WIDTH), F32)],
        compiler_params=_params(("parallel",)),
        name="in_proj",
    )(x, ca, sb, cost, sint, *weights)


def _prompt_attn_kernel(qt_ref, k_ref, vt_ref, o_ref, m_s, acc_s, s_s):
    qi = pl.program_id(1)
    kb = KEY_BLOCK
    per_q = ATT_BLOCK // kb
    m_s[...] = jnp.full(m_s.shape, -jnp.inf, F32)
    acc_s[...] = jnp.zeros(acc_s.shape, F32)
    ones = jnp.ones((ONES_ROWS, kb), BF16)

    def key_block(off, first_q, masked):
        cols = slice(first_q, ATT_BLOCK)
        vals = jnp.concatenate([vt_ref[:, pl.ds(off, kb)], ones], axis=0)
        if masked:
            key = lax.broadcasted_iota(jnp.int32, (kb, ATT_BLOCK - first_q), 0)
            qry = lax.broadcasted_iota(jnp.int32, (kb, ATT_BLOCK - first_q), 1)
            visible = key <= qry

        def score_stage(h):
            s = _dot(k_ref[pl.ds(off, kb), h * HEAD_PAD:(h + 1) * HEAD_PAD], qt_ref[h, :, cols])
            if masked:
                s = jnp.where(visible, s, -jnp.inf)
            m_old = m_s[h, :, cols]
            m_new = jnp.maximum(m_old, jnp.max(s, axis=0, keepdims=True))
            m_s[h, :, cols] = m_new
            s_s[h % 2, :, cols] = s
            return m_old, m_new

        def value_stage(h, m_old, m_new):
            p = jnp.exp2(s_s[h % 2, :, cols] - m_new)
            acc_s[h, :, cols] = jnp.exp2(m_old - m_new) * acc_s[h, :, cols] + _dot(vals, p.astype(BF16))

        pending = score_stage(0)
        for h in range(N_HEADS):
            nxt = score_stage(h + 1) if h + 1 < N_HEADS else None
            value_stage(h, *pending)
            pending = nxt

    def body(j, carry):
        key_block(pl.multiple_of(j * kb, kb), 0, False)
        return carry

    lax.fori_loop(0, qi * per_q, body, 0)
    for d in range(per_q):
        key_block(pl.multiple_of(qi * ATT_BLOCK + d * kb, kb), d * kb, True)
    for h in range(N_HEADS):
        acc = acc_s[h]
        o_t = acc[:KV_RANK] / acc[KV_RANK:KV_RANK + 1]
        o_ref[:, h * HEAD_PAD:(h + 1) * HEAD_PAD] = o_t.T.astype(BF16)


def _prompt_attn(qt, k, vt, batch, seq):
    nq = seq // ATT_BLOCK
    return pl.pallas_call(
        _prompt_attn_kernel,
        grid=(batch, nq),
        in_specs=[pl.BlockSpec((N_HEADS, HEAD_PAD, ATT_BLOCK), lambda b, i: (0, 0, b * nq + i)),
                  pl.BlockSpec((seq, QK_WIDTH), lambda b, i: (b, 0)),
                  pl.BlockSpec((KV_RANK, seq), lambda b, i: (0, b))],
        out_specs=pl.BlockSpec((ATT_BLOCK, N_HEADS * KV_RANK), lambda b, i: (b * nq + i, 0)),
        out_shape=jax.ShapeDtypeStruct((batch * seq, N_HEADS * KV_RANK), BF16),
        scratch_shapes=[pltpu.VMEM((N_HEADS, 1, ATT_BLOCK), F32),
                        pltpu.VMEM((N_HEADS, KV_RANK + ONES_ROWS, ATT_BLOCK), F32),
                        pltpu.VMEM((2, KEY_BLOCK, ATT_BLOCK), F32)],
        compiler_params=_params(("parallel", "parallel")),
        name="prompt_attn",
    )(qt, k, vt)


def _absorb_q_kernel(qn_ref, g_ref, wukt_ref, qa_ref):
    qg = qn_ref[...] * g_ref[...]
    lane = lax.broadcasted_iota(jnp.int32, qg.shape, 1)
    for h in range(N_HEADS):
        q_h = jnp.where((lane >= h * QK_NOPE) & (lane < (h + 1) * QK_NOPE), qg, 0.0)
        qa_ref[:, h * KV_RANK:(h + 1) * KV_RANK] = jnp.dot(
            q_h, wukt_ref[...], precision=lax.Precision.HIGHEST, preferred_element_type=F32)


def _absorb_q(qn, w):
    n_samp = qn.shape[0]
    args = [qn, w['g_kn_row'], w['w_ukt_f32']]
    return pl.pallas_call(
        _absorb_q_kernel,
        grid=(1,),
        in_specs=[_full(a.shape) for a in args],
        out_specs=_full((n_samp, N_HEADS * KV_RANK)),
        out_shape=jax.ShapeDtypeStruct((n_samp, N_HEADS * KV_RANK), F32),
        compiler_params=_params(("arbitrary",)),
        name="absorb_q",
    )(*args)


def _sample_attn_kernel(pt_ref, q_ref, qa_ref, kown_ref, cown_ref, wukb_ref,
                        cache_c, cache_r, o_ref,
                        wext, cbuf, rbuf, sems, *, n_pages):
    b = pl.program_id(0)
    last = pl.num_programs(0) - 1
    slot = lax.rem(b, 2)
    past = n_pages * PAGE_SIZE
    n_tiles = past // KV_TILE
    pages_per_tile = KV_TILE // PAGE_SIZE
    nw = N_HEADS * QK_NOPE
    nxt = jnp.minimum(b + 1, last)

    def start_page(bb, sl, i):
        page = pt_ref[bb * n_pages + i]
        off = pl.multiple_of(i * PAGE_SIZE, PAGE_SIZE)
        pltpu.make_async_copy(cache_c.at[page], cbuf.at[sl, pl.ds(off, PAGE_SIZE)], sems.at[0, sl]).start()
        pltpu.make_async_copy(cache_r.at[page], rbuf.at[sl, :, pl.ds(off, PAGE_SIZE)],
                              sems.at[1, sl]).start(priority=1)

    def wait_pages(sl):
        pltpu.make_async_copy(cbuf.at[sl], cbuf.at[sl], sems.at[0, sl]).wait()
        pltpu.make_async_copy(rbuf.at[sl], rbuf.at[sl], sems.at[1, sl]).wait()

    @pl.when(b == 0)
    def _():
        def issue(i, carry):
            start_page(0, 0, i)
            return carry
        lax.fori_loop(0, n_pages, issue, 0)
        wext[:nw, :] = wukb_ref[...]

    wext[nw:, :] = jnp.concatenate(
        [qa_ref[0], jnp.zeros((wext.shape[0] - nw - N_HEADS, KV_RANK), F32)], axis=0).astype(BF16)
    qr = q_ref[0][:, QK_NOPE:QK_NOPE + QK_ROPE].astype(BF16)

    def scores(c_blk, krt_blk):
        tk = c_blk.shape[0]
        cb = c_blk.astype(BF16)
        knt = _dot_nt(wext[...], cb)
        kn3 = knt[:nw].reshape(N_HEADS, QK_NOPE, tk)
        r = lax.rsqrt(jnp.sum(kn3 * kn3, axis=1) * (1.0 / QK_NOPE) + EPS)
        s_rope = _dot(qr, krt_blk.astype(BF16))
        return cb, knt[nw:nw + N_HEADS] * r + s_rope

    s_own = jnp.sum(q_ref[0] * kown_ref[0].astype(F32), axis=-1, keepdims=True)

    def value_part(m_j, p_j, cb_j):
        return m_j, jnp.sum(p_j, axis=-1, keepdims=True), _dot(p_j.astype(BF16), cb_j)

    wait_pages(slot)
    parts, pending = [], None
    for j in range(n_tiles):
        for i in range(pages_per_tile):
            start_page(nxt, 1 - slot, j * pages_per_tile + i)
        keys = slice(j * KV_TILE, (j + 1) * KV_TILE)
        cb, s = scores(cbuf[slot, keys, :], rbuf[slot, :, keys])
        m_j = jnp.max(s, axis=-1, keepdims=True)
        if pending is not None:
            parts.append(value_part(*pending))
        pending = (m_j, jnp.exp2(s - m_j), cb)
    parts.append(value_part(*pending))

    m = s_own
    for m_j, _, _ in parts:
        m = jnp.maximum(m, m_j)
    p_own = jnp.exp2(s_own - m)
    round_bf16 = lambda a: a.astype(BF16).astype(F32)
    l = p_own
    acc = round_bf16(p_own) * round_bf16(cown_ref[0])
    for m_j, l_j, acc_j in parts:
        w_j = jnp.exp2(m_j - m)
        l = l + w_j * l_j
        acc = acc + w_j * acc_j
    o_ref[0] = acc / l

    @pl.when(b == last)
    def _():
        wait_pages(1 - slot)


def _sample_attn(page_table, q, qa, k_own, c_own, w, cache_c, cache_rt):
    n_samp, n_pages = page_table.shape
    past = n_pages * PAGE_SIZE
    wext_rows = N_HEADS * QK_NOPE + 2 * SUBLANES
    per_sample = lambda a: pl.BlockSpec((1,) + a.shape[1:], lambda b, pt: (b, 0, 0))
    grid_spec = pltpu.PrefetchScalarGridSpec(
        num_scalar_prefetch=1,
        grid=(n_samp,),
        in_specs=[per_sample(q), per_sample(qa), per_sample(k_own), per_sample(c_own),
                  pl.BlockSpec(w['w_ukt'].shape, lambda b, pt: (0, 0)),
                  pl.BlockSpec(memory_space=pl.ANY),
                  pl.BlockSpec(memory_space=pl.ANY)],
        out_specs=pl.BlockSpec((1, N_HEADS, KV_RANK), lambda b, pt: (b, 0, 0)),
        scratch_shapes=[pltpu.VMEM((wext_rows, KV_RANK), BF16),
                        pltpu.VMEM((2, past, KV_RANK), F32),
                        pltpu.VMEM((2, QK_ROPE, past), F32),
                        pltpu.SemaphoreType.DMA((2, 2))])
    return pl.pallas_call(
        functools.partial(_sample_attn_kernel, n_pages=n_pages),
        grid_spec=grid_spec,
        out_shape=jax.ShapeDtypeStruct((n_samp, N_HEADS, KV_RANK), F32),
        compiler_params=_params(("arbitrary",)),
        name="sample_attn",
    )(page_table.reshape(-1), q, qa, k_own, c_own, w['w_ukt'], cache_c, cache_rt)


def _mix_proj_kernel(x_ref, o_ref, u_ref, v_ref, wuv_ref, ws_ref, bias_ref, wo_ref, g_ref,
                     h_ref, hn_ref, *, single_token):
    tm = x_ref.shape[0]
    attn = _dot(o_ref[...], wuv_ref[...])
    v = v_ref[...]
    if single_token:
        mixed = v * ws_ref[...] + bias_ref[...]
    else:
        vb = v.astype(BF16)
        row = lax.broadcasted_iota(jnp.int32, (CHUNK, CHUNK), 0)
        col = lax.broadcasted_iota(jnp.int32, (CHUNK, CHUNK), 1)
        w_tril = [jnp.where(col <= row, ws_ref[g], 0.0).astype(BF16) for g in range(N_CG)]
        low_half = lax.broadcasted_iota(jnp.int32, (CHUNK, LANES), 1) < CG_DIM
        chunks = []
        for c in range(tm // CHUNK):
            cols = []
            for j in range(CG_WIDTH // LANES):
                vp = vb[c * CHUNK:(c + 1) * CHUNK, j * LANES:(j + 1) * LANES]
                cols.append(jnp.where(low_half, _dot(w_tril[2 * j], vp), _dot(w_tril[2 * j + 1], vp)))
            chunks.append(jnp.concatenate(cols, axis=1) + bias_ref[...])
        mixed = jnp.concatenate(chunks, axis=0)
    sg = u_ref[...] * mixed
    width = attn.shape[1]
    h = x_ref[...] + _dot(attn.astype(BF16), wo_ref[:width, :]) + _dot(sg.astype(BF16), wo_ref[width:, :])
    h_ref[...] = h
    hn_ref[...] = (_rms(h) * g_ref[...]).astype(BF16)


def _conv_ffn_kernel(*refs, single_token):
    if single_token:
        h_ref, hn_ref, win_ref, cw_ref, cb_ref, wout_ref, h0_ref, h1_ref, out_ref, a_ref, act_s = refs
    else:
        h_ref, hn_ref, win_ref, cw_ref, cb_ref, wout_ref, out_ref, last_ref, act_s, carry = refs
    tm = hn_ref.shape[0]
    hn = hn_ref[...]

    def conv(cols):
        a = _dot(hn, win_ref[:, cols])
        if single_token:
            a2, a1 = h0_ref[:, cols], h1_ref[:, cols]
            a_ref[0, :, cols] = a1
            a_ref[1, :, cols] = a
        else:
            ext = jnp.concatenate([carry[:, cols], a], axis=0)
            a1 = ext[SUBLANES - 1:SUBLANES - 1 + tm]
            a2 = ext[SUBLANES - 2:SUBLANES - 2 + tm]
            tail = a[tm - SUBLANES:]
            carry[:, cols] = tail
            last_ref[0, :, cols] = tail[SUBLANES - (CONV_W - 1):]
        return cb_ref[:, cols] + cw_ref[0:1, cols] * a2 + cw_ref[1:2, cols] * a1 + cw_ref[2:3, cols] * a

    for j in range(D_FF // FF_TILE):
        gate = slice(j * FF_TILE, (j + 1) * FF_TILE)
        up = slice(D_FF + j * FF_TILE, D_FF + (j + 1) * FF_TILE)
        act_s[:, gate] = (jax.nn.silu(conv(gate)) * conv(up)).astype(BF16)
    out_ref[...] = h_ref[...] + _dot(act_s[...], wout_ref[...])


def _ple_kernel(h_ref, p_ref, g_ref, wgate_ref, wproj_ref, gpost_ref, y_ref):
    h = h_ref[...]
    gate = jax.nn.sigmoid(_dot((_rms(h) * g_ref[...]).astype(BF16), wgate_ref[...]))
    e = _rms(_dot(p_ref[...].astype(BF16), wproj_ref[...])) * gpost_ref[...]
    y_ref[...] = h + gate * e


def _post_attn_kernel(*refs, single_token):
    x_ref, o_ref, u_ref, v_ref, p_ref = refs[:5]
    mix_w, ffn_w, ple_w = refs[5:10], refs[10:14], refs[14:18]
    if single_token:
        h0_ref, h1_ref, y_ref, a_ref, act_s, h_s, hn_s, h2_s = refs[18:]
        _mix_proj_kernel(x_ref, o_ref, u_ref, v_ref, *mix_w, h_s, hn_s, single_token=True)
        _conv_ffn_kernel(h_s, hn_s, *ffn_w, h0_ref, h1_ref, h2_s, a_ref, act_s, single_token=True)
        _ple_kernel(h2_s, p_ref, *ple_w, y_ref)
        return
    y_ref, last_ref, act_s, carry, h_s, hn_s, h2_s = refs[18:]

    @pl.when(pl.program_id(1) == 0)
    def _():
        carry[...] = jnp.zeros(carry.shape, F32)

    _mix_proj_kernel(x_ref, o_ref, u_ref, v_ref, *mix_w, h_s, hn_s, single_token=False)
    _conv_ffn_kernel(h_s, hn_s, *ffn_w, h2_s, last_ref, act_s, carry, single_token=False)
    _ple_kernel(h2_s, p_ref, *ple_w, y_ref)


def _post_attn(x, o_lat, u, v, p, w, batch, seq, tm, hist=None):
    single_token = hist is not None
    nt = seq // tm
    tok = lambda width: pl.BlockSpec((tm, width), lambda b, t: (b * nt + t, 0))
    resident = lambda a: pl.BlockSpec(a.shape, lambda b, t: (0,) * a.ndim, pipeline_mode=pl.Buffered(1))
    ws, bias = (w['ws_diag'], w['bias_first']) if single_token else (w['w_s'], w['bias_tab'])
    weights = [w['w_uv_bd'], ws, bias, w['w_o'], w['g_ffn'],
               w['w_ff_in'], w['conv_w'], w['conv_b'], w['w_ff_out'],
               w['g_ple'], w['w_gate'], w['w_proj'], w['g_post']]
    in_specs = ([tok(D_MODEL), tok(N_HEADS * KV_RANK), tok(CG_WIDTH), tok(CG_WIDTH), tok(PLE_DIM)]
                + [resident(a) for a in weights])
    args = [x, o_lat, u, v, p] + weights
    scratch = [pltpu.VMEM((tm, D_FF), BF16)]
    if single_token:
        in_specs += [resident(hist[0]), resident(hist[1])]
        args += list(hist)
        last_spec = pl.BlockSpec((CONV_W - 1, tm, 2 * D_FF), lambda b, t: (0, 0, 0))
        last_shape = jax.ShapeDtypeStruct((CONV_W - 1, tm, 2 * D_FF), F32)
    else:
        last_spec = pl.BlockSpec((1, CONV_W - 1, 2 * D_FF), lambda b, t: (b * nt + t, 0, 0))
        last_shape = jax.ShapeDtypeStruct((batch * nt, CONV_W - 1, 2 * D_FF), F32)
        scratch.append(pltpu.VMEM((SUBLANES, 2 * D_FF), F32))
    scratch += [pltpu.VMEM((tm, D_MODEL), F32), pltpu.VMEM((tm, D_MODEL), BF16),
                pltpu.VMEM((tm, D_MODEL), F32)]
    return pl.pallas_call(
        functools.partial(_post_attn_kernel, single_token=single_token),
        grid=(batch, nt),
        in_specs=in_specs,
        out_specs=[tok(D_MODEL), last_spec],
        out_shape=[jax.ShapeDtypeStruct((batch * seq, D_MODEL), F32), last_shape],
        scratch_shapes=scratch,
        compiler_params=_params(("arbitrary", "arbitrary")),
        name="post_attn_single" if single_token else "post_attn",
    )(*args)


def _rope_tables(pos):
    inv = ROPE_THETA ** (-jnp.arange(ROPE_HALF, dtype=F32) * (2.0 / QK_ROPE))
    ang = pos.astype(F32)[:, None] * inv[None, :]
    cos, sin = jnp.cos(ang), jnp.sin(ang)
    n = pos.shape[0]
    ones = jnp.ones((n, QK_NOPE), F32)
    zq = jnp.zeros((n, QK_NOPE), F32)
    zp = jnp.zeros((n, HEAD_PAD - QK_NOPE - QK_ROPE), F32)
    ca = jnp.concatenate([ones, cos, cos, zp], axis=1)
    sb = jnp.concatenate([zq, -sin, sin, zp], axis=1)
    return ca, sb, cos.T, sin.T


def _prep_weights(i, attn_norm_w, w_in, q_norm_w, w_uq, q_nope_norm_w, q_rope_norm_w, kv_norm_w,
                  k_rope_norm_w, w_uk, k_nope_norm_w, w_uv, w_s, b_s, w_o, ffn_norm_w, w_ff_in,
                  conv_w, conv_b, w_ff_out, ple_norm_w, w_ple_gate, w_ple_proj, ple_post_norm_w):
    row = lambda g: g[i][None, :]
    swap = lambda a: jnp.concatenate([a[..., ROPE_HALF:], a[..., :ROPE_HALF]], axis=-1)
    win = w_in[i]
    wq = w_uq[i].reshape(Q_RANK, N_HEADS, QK_NOPE + QK_ROPE)
    nope, rope = wq[..., :QK_NOPE], wq[..., QK_NOPE:]
    pad = HEAD_PAD - QK_NOPE - QK_ROPE
    w_qt = jnp.concatenate([nope, rope, jnp.zeros((Q_RANK, N_HEADS, pad), F32)],
                           axis=-1).reshape(Q_RANK, QK_WIDTH).T
    col = lambda g: jnp.broadcast_to(g[:, None], (g.shape[0], LANES))
    kr_cols = win[:, OFF_KR:OFF_U]
    w_in_e = jnp.concatenate([win[:, :OFF_KR], win[:, OFF_U:], jnp.zeros((D_MODEL, QK_NOPE), F32),
                              kr_cols, swap(kr_cols)], axis=1)
    w_uk_e = jnp.concatenate([w_uk[i], jnp.zeros((KV_RANK, N_HEADS, HEAD_PAD - QK_NOPE), F32)],
                             axis=-1).reshape(KV_RANK, QK_WIDTH)
    gn, gr = k_nope_norm_w[i], k_rope_norm_w[i]
    ga = jnp.tile(jnp.concatenate([gn, gr, jnp.zeros((pad,), F32)]), N_HEADS)[None, :]
    gb = jnp.tile(jnp.concatenate([jnp.zeros((QK_NOPE,), F32), swap(gr), jnp.zeros((pad,), F32)]),
                  N_HEADS)[None, :]
    lane = jnp.arange(2 * HEAD_PAD)
    slot, off = lane // HEAD_PAD, lane % HEAD_PAD
    grp = jnp.where(off < QK_NOPE, 0, jnp.where(off < QK_NOPE + QK_ROPE, 1, 2))
    same = (slot[:, None] == slot[None, :]) & (grp[:, None] == grp[None, :]) & (grp[:, None] < 2)
    gm = jnp.where(same, jnp.where(grp[:, None] == 0, 1.0 / QK_NOPE, 1.0 / QK_ROPE), 0.0)
    w_ukt = w_uk[i].reshape(KV_RANK, N_HEADS * QK_NOPE).T
    eye = jnp.eye(N_HEADS, dtype=F32)
    w_uv_bd = jnp.einsum('rhd,hg->hrgd', w_uv[i], eye).reshape(N_HEADS * KV_RANK, N_HEADS * V_DIM)
    return {
        'g_attn': row(attn_norm_w),
        'w_in': w_in_e.astype(BF16),
        'g_q': row(q_norm_w), 'w_qt': w_qt.astype(BF16),
        'g_qn': col(q_nope_norm_w[i]), 'g_qr': col(q_rope_norm_w[i]),
        'g_kv': row(kv_norm_w), 'w_kvt': win[:, OFF_KV:OFF_KR].T.astype(BF16), 'g_kvt': col(kv_norm_w[i]),
        'w_krt': kr_cols.T.astype(BF16), 'g_krt': col(k_rope_norm_w[i]),
        'w_uk': w_uk_e.astype(BF16), 'gm': gm.astype(BF16), 'ga': ga, 'gb': gb,
        'w_ukt': w_ukt.astype(BF16), 'w_ukt_f32': w_ukt,
        'g_kn_row': jnp.tile(k_nope_norm_w[i], N_HEADS)[None, :],
        'w_uv_bd': w_uv_bd.astype(BF16),
        'w_s': w_s[i],
        'bias_tab': jnp.repeat(b_s[i].T, CG_DIM, axis=1),
        'ws_diag': jnp.repeat(w_s[i][:, 0, 0], CG_DIM)[None, :],
        'bias_first': jnp.repeat(b_s[i][:, 0], CG_DIM)[None, :],
        'w_o': w_o[i].astype(BF16), 'g_ffn': row(ffn_norm_w),
        'w_ff_in': w_ff_in[i].astype(BF16), 'conv_w': conv_w[i], 'conv_b': row(conv_b),
        'w_ff_out': w_ff_out[i].astype(BF16),
        'g_ple': row(ple_norm_w), 'w_gate': w_ple_gate[i].astype(BF16),
        'w_proj': w_ple_proj[i].astype(BF16), 'g_post': row(ple_post_norm_w),
    }


def kernel(x_prompt, x_sample, cache_ckv, cache_krope, state_conv, page_table, p_prompt, p_sample,
           attn_norm_w, w_in, q_norm_w, w_uq, q_nope_norm_w, q_rope_norm_w, kv_norm_w, k_rope_norm_w,
           w_uk, k_nope_norm_w, w_uv, w_s, b_s, w_o, ffn_norm_w, w_ff_in, conv_w, conv_b, w_ff_out,
           ple_norm_w, w_ple_gate, w_ple_proj, ple_post_norm_w):
    batch, seq, _ = x_prompt.shape
    n_samp, seq_s, _ = x_sample.shape
    depth = w_in.shape[0]
    assert depth == 1 and seq_s == 1 and seq % TOK_TILE == 0 and seq % CHUNK == 0
    past_len = page_table.shape[1] * PAGE_SIZE
    assert past_len % KV_TILE == 0
    w = _prep_weights(0, attn_norm_w, w_in, q_norm_w, w_uq, q_nope_norm_w, q_rope_norm_w, kv_norm_w,
                      k_rope_norm_w, w_uk, k_nope_norm_w, w_uv, w_s, b_s, w_o, ffn_norm_w, w_ff_in,
                      conv_w, conv_b, w_ff_out, ple_norm_w, w_ple_gate, w_ple_proj, ple_post_norm_w)

    n_tok = batch * seq
    tabs = _rope_tables(jnp.arange(seq, dtype=jnp.int32))
    xp = x_prompt.reshape(n_tok, D_MODEL)
    qt, k, ckv, vt, krt, u, v = _in_proj(xp, tabs, w, 2 * TOK_TILE)
    o_lat = _prompt_attn(qt, k, vt, batch, seq)
    y_prompt, a_last = _post_attn(xp, o_lat, u, v, p_prompt[0].reshape(n_tok, PLE_DIM), w,
                                  batch, seq, TOK_TILE)
    y_prompt = y_prompt.reshape(batch, seq, D_MODEL)
    n_keep = (seq - 1) % CHUNK + 1
    new_ckv_p = ckv.reshape(1, batch, seq, KV_RANK)
    new_kr_p = jnp.swapaxes(krt, 1, 2)[None]
    new_v_p = v.reshape(batch, seq, CG_WIDTH)[:, seq - n_keep:][None]
    new_conv_p = a_last.reshape(batch, seq // TOK_TILE, CONV_W - 1, 2 * D_FF)[:, -1][None]

    tabs_s = _rope_tables(jnp.full((n_samp,), past_len, dtype=jnp.int32))
    xs = x_sample.reshape(n_samp, D_MODEL)
    qt_s, k_s, ckv_s, _, krt_s, u_s, v_s = _in_proj(xs, tabs_s, w, n_samp)
    q3 = jnp.transpose(qt_s, (2, 0, 1)).astype(F32)
    kr_s = krt_s[0].T
    qa = _absorb_q(q3[:, :, :QK_NOPE].reshape(n_samp, N_HEADS * QK_NOPE), w)
    cache_rt = jnp.swapaxes(cache_krope[0], 1, 2)
    o_s = _sample_attn(page_table, q3, qa.reshape(n_samp, N_HEADS, KV_RANK),
                       k_s.reshape(n_samp, N_HEADS, HEAD_PAD), ckv_s.reshape(n_samp, 1, KV_RANK),
                       w, cache_ckv[0], cache_rt)
    o_s = o_s.reshape(n_samp, N_HEADS * KV_RANK).astype(BF16)
    hist = (state_conv[0, :, 0, :], state_conv[0, :, 1, :])
    y_sample, a_s = _post_attn(xs, o_s, u_s, v_s, p_sample[0].reshape(n_samp, PLE_DIM), w,
                               1, n_samp, n_samp, hist=hist)
    y_sample = y_sample.reshape(n_samp, 1, D_MODEL)
    new_conv_s = jnp.swapaxes(a_s, 0, 1)[None]

    return (y_prompt, y_sample, new_ckv_p, new_kr_p,
            ckv_s.reshape(1, n_samp, 1, KV_RANK), kr_s.reshape(1, n_samp, 1, QK_ROPE),
            new_v_p, v_s.reshape(1, n_samp, 1, CG_WIDTH), new_conv_p, new_conv_s)
```

```python
import functools

import jax
import jax.numpy as jnp
from jax import lax
from jax.experimental import pallas as pl
from jax.experimental.pallas import tpu as pltpu

F32 = jnp.float32
BF16 = jnp.bfloat16

D_MODEL = 1024
N_HEADS = 8
QK_NOPE = 64
QK_ROPE = 32
ROPE_HALF = QK_ROPE // 2
V_DIM = 64
Q_RANK = 256
KV_RANK = 128
CHUNK = 128
N_CG = 8
CG_WIDTH = 512
CG_DIM = 64
D_FF = 2816
CONV_W = 3
PLE_DIM = 256
PAGE_SIZE = 128
ROPE_THETA = 10000.0
EPS = 1e-6
SCALE = (QK_NOPE + QK_ROPE) ** -0.5
Q_SCALE = SCALE * 1.4426950408889634
OFF_KV = Q_RANK
OFF_KR = OFF_KV + KV_RANK
OFF_U = OFF_KR + QK_ROPE
OFF_V = OFF_U + CG_WIDTH
IN_WIDTH = OFF_V + CG_WIDTH
HEAD_PAD = 128
QK_WIDTH = N_HEADS * HEAD_PAD
Z_WIDTH = Q_RANK + KV_RANK + 2 * CG_WIDTH

LANES = 128
SUBLANES = 8
VMEM_LIMIT_BYTES = 56 * 1024 * 1024

TOK_TILE = 512
ATT_BLOCK = 1024
KEY_BLOCK = 512
ONES_ROWS = 16
FF_TILE = 256
KV_TILE = 2048

NT_DIMS = (((1,), (1,)), ((), ()))


def _params(semantics):
    return pltpu.CompilerParams(dimension_semantics=semantics,
                                vmem_limit_bytes=VMEM_LIMIT_BYTES)


def _rms(x):
    return x * lax.rsqrt(jnp.mean(x * x, axis=-1, keepdims=True) + EPS)


def _dot(a, b):
    return jnp.dot(a, b, preferred_element_type=F32)


def _dot_nt(a, b):
    return lax.dot_general(a, b, NT_DIMS, preferred_element_type=F32)


def _full(shape):
    zeros = (0,) * len(shape)
    return pl.BlockSpec(shape, lambda *_: zeros)


def _in_proj_kernel(x_ref, ca_ref, sb_ref, cost_ref, sint_ref,
                    g_attn_ref, w_in_ref, g_q_ref, w_qt_ref, g_qn_ref, g_qr_ref,
                    g_kv_ref, w_kvt_ref, g_kvt_ref, w_krt_ref, g_krt_ref, w_uk_ref, gm_ref, ga_ref, gb_ref,
                    qt_ref, k_ref, ckv_ref, vt_ref, krt_ref, u_ref, v_ref):
    tm = x_ref.shape[0]
    reps = tm // LANES
    cos, sin = cost_ref[...], sint_ref[...]
    hb = (_rms(x_ref[...]) * g_attn_ref[...]).astype(BF16)
    z = _dot(hb, w_in_ref[...])
    ckv = _rms(z[:, OFF_KV:OFF_KR]) * g_kv_ref[...]
    ckv_ref[...] = ckv
    kr_slot = z[:, Z_WIDTH:]
    ka = _dot(ckv.astype(BF16), w_uk_ref[...]) + jnp.tile(kr_slot, (1, N_HEADS))
    ct = _dot_nt(w_kvt_ref[...], hb)
    krt = _dot_nt(w_krt_ref[...], hb)
    qln = (_rms(z[:, :Q_RANK]) * g_q_ref[...]).astype(BF16)
    qt = _dot_nt(w_qt_ref[...], qln).reshape(N_HEADS, HEAD_PAD, tm)
    sq = (ka * ka).astype(BF16)
    gm = gm_ref[...]
    pair = 2 * HEAD_PAD
    ms = jnp.concatenate([_dot(sq[:, i * pair:(i + 1) * pair], gm) for i in range(N_HEADS // 2)], axis=1)
    krb = pltpu.roll(kr_slot, HEAD_PAD - QK_ROPE, axis=1)

    u_ref[...] = jax.nn.gelu(z[:, OFF_KR:OFF_KR + CG_WIDTH])
    v_ref[...] = jax.nn.gelu(z[:, OFF_KR + CG_WIDTH:OFF_KR + 2 * CG_WIDTH])
    ct = ct * lax.rsqrt(jnp.mean(ct * ct, axis=0, keepdims=True) + EPS)
    vt_ref[...] = (ct * jnp.tile(g_kvt_ref[...], (1, reps))).astype(BF16)
    krt = krt * lax.rsqrt(jnp.mean(krt * krt, axis=0, keepdims=True) + EPS) * jnp.tile(g_krt_ref[...], (1, reps))
    kx1, kx2 = krt[:ROPE_HALF], krt[ROPE_HALF:]
    krt_ref[0] = jnp.concatenate([kx1 * cos - kx2 * sin, kx1 * sin + kx2 * cos], axis=0)

    nope = qt[:, :QK_NOPE]
    rope = qt[:, QK_NOPE:QK_NOPE + QK_ROPE]
    g_qn = jnp.tile(g_qn_ref[...], (1, reps))[None]
    g_qr = jnp.tile(g_qr_ref[...], (1, reps))[None]
    nope = nope * lax.rsqrt(jnp.mean(nope * nope, axis=1, keepdims=True) + EPS) * (g_qn * Q_SCALE)
    rope = rope * lax.rsqrt(jnp.mean(rope * rope, axis=1, keepdims=True) + EPS) * (g_qr * Q_SCALE)
    x1, x2 = rope[:, :ROPE_HALF], rope[:, ROPE_HALF:]
    cos, sin = cos[None], sin[None]
    pad = jnp.zeros((N_HEADS, HEAD_PAD - QK_NOPE - QK_ROPE, tm), F32)
    qt_ref[...] = jnp.concatenate([nope, x1 * cos - x2 * sin, x1 * sin + x2 * cos, pad], axis=1).astype(BF16)

    ca = jnp.tile(ca_ref[...], (1, N_HEADS))
    sb = jnp.tile(sb_ref[...], (1, N_HEADS))
    k = lax.rsqrt(ms + EPS) * (ka * ga_ref[...] * ca + jnp.tile(krb, (1, N_HEADS)) * gb_ref[...] * sb)
    k_ref[...] = k.astype(BF16)


def _in_proj(x, tabs, w, tm):
    n = x.shape[0]
    ca, sb, cost, sint = tabs
    pos_tiles = ca.shape[0] // tm
    tok = lambda width: pl.BlockSpec((tm, width), lambda i: (i, 0))
    tok_t = lambda rows: pl.BlockSpec((rows, tm), lambda i: (0, i))
    pos = pl.BlockSpec((tm, HEAD_PAD), lambda i: (i % pos_tiles, 0))
    pos_t = pl.BlockSpec((ROPE_HALF, tm), lambda i: (0, i % pos_tiles))
    weights = [w['g_attn'], w['w_in'], w['g_q'], w['w_qt'], w['g_qn'], w['g_qr'],
               w['g_kv'], w['w_kvt'], w['g_kvt'], w['w_krt'], w['g_krt'], w['w_uk'], w['gm'], w['ga'], w['gb']]
    return pl.pallas_call(
        _in_proj_kernel,
        grid=(n // tm,),
        in_specs=[tok(D_MODEL), pos, pos, pos_t, pos_t]
                 + [_full(a.shape) for a in weights],
        out_specs=[pl.BlockSpec((N_HEADS, HEAD_PAD, tm), lambda i: (0, 0, i)),
                   tok(QK_WIDTH), tok(KV_RANK), tok_t(KV_RANK),
                   pl.BlockSpec((1, QK_ROPE, tm), lambda i: (i // pos_tiles, 0, i % pos_tiles)),
                   tok(CG_WIDTH), tok(CG_WIDTH)],
        out_shape=[jax.ShapeDtypeStruct((N_HEADS, HEAD_PAD, n), BF16),
                   jax.ShapeDtypeStruct((n, QK_WIDTH), BF16),
                   jax.ShapeDtypeStruct((n, KV_RANK), F32),
                   jax.ShapeDtypeStruct((KV_RANK, n), BF16),
                   jax.ShapeDtypeStruct((n // (pos_tiles * tm), QK_ROPE, pos_tiles * tm), F32),
                   jax.ShapeDtypeStruct((n, CG_WIDTH), F32),
                   jax.ShapeDtypeStruct((n, CG_WIDTH), F32)],
        compiler_params=_params(("parallel",)),
        name="in_proj",
    )(x, ca, sb, cost, sint, *weights)


def _prompt_attn_kernel(qt_ref, k_ref, vt_ref, o_ref, m_s, acc_s, s_s):
    qi = pl.program_id(1)
    kb = KEY_BLOCK
    per_q = ATT_BLOCK // kb
    m_s[...] = jnp.full(m_s.shape, -jnp.inf, F32)
    acc_s[...] = jnp.zeros(acc_s.shape, F32)
    ones = jnp.ones((ONES_ROWS, kb), BF16)

    def key_block(off, first_q, masked):
        cols = slice(first_q, ATT_BLOCK)
        vals = jnp.concatenate([vt_ref[:, pl.ds(off, kb)], ones], axis=0)
        if masked:
            key = lax.broadcasted_iota(jnp.int32, (kb, ATT_BLOCK - first_q), 0)
            qry = lax.broadcasted_iota(jnp.int32, (kb, ATT_BLOCK - first_q), 1)
            visible = key <= qry

        def score_stage(h):
            s = _dot(k_ref[pl.ds(off, kb), h * HEAD_PAD:(h + 1) * HEAD_PAD], qt_ref[h, :, cols])
            if masked:
                s = jnp.where(visible, s, -jnp.inf)
            m_old = m_s[h, :, cols]
            m_new = jnp.maximum(m_old, jnp.max(s, axis=0, keepdims=True))
            m_s[h, :, cols] = m_new
            s_s[h % 2, :, cols] = s
            return m_old, m_new

        def value_stage(h, m_old, m_new):
            p = jnp.exp2(s_s[h % 2, :, cols] - m_new)
            acc_s[h, :, cols] = jnp.exp2(m_old - m_new) * acc_s[h, :, cols] + _dot(vals, p.astype(BF16))

        pending = score_stage(0)
        for h in range(N_HEADS):
            nxt = score_stage(h + 1) if h + 1 < N_HEADS else None
            value_stage(h, *pending)
            pending = nxt

    def body(j, carry):
        key_block(pl.multiple_of(j * kb, kb), 0, False)
        return carry

    lax.fori_loop(0, qi * per_q, body, 0)
    for d in range(per_q):
        key_block(pl.multiple_of(qi * ATT_BLOCK + d * kb, kb), d * kb, True)
    for h in range(N_HEADS):
        acc = acc_s[h]
        o_t = acc[:KV_RANK] / acc[KV_RANK:KV_RANK + 1]
        o_ref[:, h * HEAD_PAD:(h + 1) * HEAD_PAD] = o_t.T.astype(BF16)


def _prompt_attn(qt, k, vt, batch, seq):
    nq = seq // ATT_BLOCK
    return pl.pallas_call(
        _prompt_attn_kernel,
        grid=(batch, nq),
        in_specs=[pl.BlockSpec((N_HEADS, HEAD_PAD, ATT_BLOCK), lambda b, i: (0, 0, b * nq + i)),
                  pl.BlockSpec((seq, QK_WIDTH), lambda b, i: (b, 0)),
                  pl.BlockSpec((KV_RANK, seq), lambda b, i: (0, b))],
        out_specs=pl.BlockSpec((ATT_BLOCK, N_HEADS * KV_RANK), lambda b, i: (b * nq + i, 0)),
        out_shape=jax.ShapeDtypeStruct((batch * seq, N_HEADS * KV_RANK), BF16),
        scratch_shapes=[pltpu.VMEM((N_HEADS, 1, ATT_BLOCK), F32),
                        pltpu.VMEM((N_HEADS, KV_RANK + ONES_ROWS, ATT_BLOCK), F32),
                        pltpu.VMEM((2, KEY_BLOCK, ATT_BLOCK), F32)],
        compiler_params=_params(("parallel", "parallel")),
        name="prompt_attn",
    )(qt, k, vt)


def _absorb_q_kernel(qn_ref, g_ref, wukt_ref, qa_ref):
    qg = qn_ref[...] * g_ref[...]
    lane = lax.broadcasted_iota(jnp.int32, qg.shape, 1)
    for h in range(N_HEADS):
        q_h = jnp.where((lane >= h * QK_NOPE) & (lane < (h + 1) * QK_NOPE), qg, 0.0)
        qa_ref[:, h * KV_RANK:(h + 1) * KV_RANK] = jnp.dot(
            q_h, wukt_ref[...], precision=lax.Precision.HIGHEST, preferred_element_type=F32)


def _absorb_q(qn, w):
    n_samp = qn.shape[0]
    args = [qn, w['g_kn_row'], w['w_ukt_f32']]
    return pl.pallas_call(
        _absorb_q_kernel,
        grid=(1,),
        in_specs=[_full(a.shape) for a in args],
        out_specs=_full((n_samp, N_HEADS * KV_RANK)),
        out_shape=jax.ShapeDtypeStruct((n_samp, N_HEADS * KV_RANK), F32),
        compiler_params=_params(("arbitrary",)),
        name="absorb_q",
    )(*args)


def _sample_attn_kernel(pt_ref, q_ref, qa_ref, kown_ref, cown_ref, wukb_ref,
                        cache_c, cache_r, o_ref,
                        wext, cbuf, rbuf, sems, *, n_pages):
    b = pl.program_id(0)
    last = pl.num_programs(0) - 1
    slot = lax.rem(b, 2)
    past = n_pages * PAGE_SIZE
    n_tiles = past // KV_TILE
    pages_per_tile = KV_TILE // PAGE_SIZE
    nw = N_HEADS * QK_NOPE
    nxt = jnp.minimum(b + 1, last)

    def start_page(bb, sl, i, prio=0):
        page = pt_ref[bb * n_pages + i]
        off = pl.multiple_of(i * PAGE_SIZE, PAGE_SIZE)
        pltpu.make_async_copy(cache_c.at[page], cbuf.at[sl, pl.ds(off, PAGE_SIZE)],
                              sems.at[0, sl]).start(priority=prio)
        pltpu.make_async_copy(cache_r.at[page], rbuf.at[sl, i], sems.at[1, sl]).start(priority=1 - prio)

    def wait_pages(sl):
        pltpu.make_async_copy(cbuf.at[sl], cbuf.at[sl], sems.at[0, sl]).wait()
        pltpu.make_async_copy(rbuf.at[sl], rbuf.at[sl], sems.at[1, sl]).wait()

    @pl.when(b == 0)
    def _():
        def issue(i, carry):
            start_page(0, 0, i)
            return carry
        lax.fori_loop(0, n_pages, issue, 0)
        wext[:nw, :] = wukb_ref[...]

    wext[nw:, :] = jnp.concatenate(
        [qa_ref[0], jnp.zeros((wext.shape[0] - nw - N_HEADS, KV_RANK), F32)], axis=0).astype(BF16)
    qr = q_ref[0][:, QK_NOPE:QK_NOPE + QK_ROPE].astype(BF16)

    def scores(c_blk, krt_blk):
        tk = c_blk.shape[0]
        cb = c_blk.astype(BF16)
        knt = _dot_nt(wext[...], cb)
        kn3 = knt[:nw].reshape(N_HEADS, QK_NOPE, tk)
        r = lax.rsqrt(jnp.sum(kn3 * kn3, axis=1) * (1.0 / QK_NOPE) + EPS)
        s_rope = _dot(qr, krt_blk.astype(BF16))
        return cb, knt[nw:nw + N_HEADS] * r + s_rope

    s_own = jnp.sum(q_ref[0] * kown_ref[0].astype(F32), axis=-1, keepdims=True)

    def value_part(m_j, p_j, cb_j):
        return m_j, jnp.sum(p_j, axis=-1, keepdims=True), _dot(p_j.astype(BF16), cb_j)

    wait_pages(slot)
    parts, pending = [], None
    for j in range(n_tiles):
        for i in range(pages_per_tile):
            start_page(nxt, 1 - slot, j * pages_per_tile + i, prio=i % 2)
        keys = slice(j * KV_TILE, (j + 1) * KV_TILE)
        krt = jnp.concatenate([rbuf[slot, j * pages_per_tile + i] for i in range(pages_per_tile)], axis=1)
        cb, s = scores(cbuf[slot, keys, :], krt)
        m_j = jnp.max(s, axis=-1, keepdims=True)
        if pending is not None:
            parts.append(value_part(*pending))
        pending = (m_j, jnp.exp2(s - m_j), cb)
    parts.append(value_part(*pending))

    m = s_own
    for m_j, _, _ in parts:
        m = jnp.maximum(m, m_j)
    p_own = jnp.exp2(s_own - m)
    round_bf16 = lambda a: a.astype(BF16).astype(F32)
    l = p_own
    acc = round_bf16(p_own) * round_bf16(cown_ref[0])
    for m_j, l_j, acc_j in parts:
        w_j = jnp.exp2(m_j - m)
        l = l + w_j * l_j
        acc = acc + w_j * acc_j
    o_ref[0] = acc / l

    @pl.when(b == last)
    def _():
        wait_pages(1 - slot)


def _sample_attn(page_table, q, qa, k_own, c_own, w, cache_c, cache_rt):
    n_samp, n_pages = page_table.shape
    past = n_pages * PAGE_SIZE
    wext_rows = N_HEADS * QK_NOPE + 2 * SUBLANES
    per_sample = lambda a: pl.BlockSpec((1,) + a.shape[1:], lambda b, pt: (b, 0, 0))
    grid_spec = pltpu.PrefetchScalarGridSpec(
        num_scalar_prefetch=1,
        grid=(n_samp,),
        in_specs=[per_sample(q), per_sample(qa), per_sample(k_own), per_sample(c_own),
                  pl.BlockSpec(w['w_ukt'].shape, lambda b, pt: (0, 0)),
                  pl.BlockSpec(memory_space=pl.ANY),
                  pl.BlockSpec(memory_space=pl.ANY)],
        out_specs=pl.BlockSpec((1, N_HEADS, KV_RANK), lambda b, pt: (b, 0, 0)),
        scratch_shapes=[pltpu.VMEM((wext_rows, KV_RANK), BF16),
                        pltpu.VMEM((2, past, KV_RANK), F32),
                        pltpu.VMEM((2, n_pages, QK_ROPE, PAGE_SIZE), F32),
                        pltpu.SemaphoreType.DMA((2, 2))])
    return pl.pallas_call(
        functools.partial(_sample_attn_kernel, n_pages=n_pages),
        grid_spec=grid_spec,
        out_shape=jax.ShapeDtypeStruct((n_samp, N_HEADS, KV_RANK), F32),
        compiler_params=_params(("arbitrary",)),
        name="sample_attn",
    )(page_table.reshape(-1), q, qa, k_own, c_own, w['w_ukt'], cache_c, cache_rt)


def _mix_proj_kernel(x_ref, o_ref, u_ref, v_ref, wuv_ref, ws_ref, bias_ref, wo_ref, g_ref,
                     h_ref, hn_ref, *, single_token):
    tm = x_ref.shape[0]
    attn = _dot(o_ref[...], wuv_ref[...])
    v = v_ref[...]
    if single_token:
        mixed = v * ws_ref[...] + bias_ref[...]
    else:
        vb = v.astype(BF16)
        row = lax.broadcasted_iota(jnp.int32, (CHUNK, CHUNK), 0)
        col = lax.broadcasted_iota(jnp.int32, (CHUNK, CHUNK), 1)
        w_tril = [jnp.where(col <= row, ws_ref[g], 0.0).astype(BF16) for g in range(N_CG)]
        low_half = lax.broadcasted_iota(jnp.int32, (CHUNK, LANES), 1) < CG_DIM
        chunks = []
        for c in range(tm // CHUNK):
            cols = []
            for j in range(CG_WIDTH // LANES):
                vp = vb[c * CHUNK:(c + 1) * CHUNK, j * LANES:(j + 1) * LANES]
                cols.append(jnp.where(low_half, _dot(w_tril[2 * j], vp), _dot(w_tril[2 * j + 1], vp)))
            chunks.append(jnp.concatenate(cols, axis=1) + bias_ref[...])
        mixed = jnp.concatenate(chunks, axis=0)
    sg = u_ref[...] * mixed
    width = attn.shape[1]
    h = x_ref[...] + _dot(attn.astype(BF16), wo_ref[:width, :]) + _dot(sg.astype(BF16), wo_ref[width:, :])
    h_ref[...] = h
    hn_ref[...] = (_rms(h) * g_ref[...]).astype(BF16)


def _conv_ffn_kernel(*refs, single_token):
    if single_token:
        h_ref, hn_ref, win_ref, cw_ref, cb_ref, wout_ref, h0_ref, h1_ref, out_ref, a_ref, act_s = refs
    else:
        h_ref, hn_ref, win_ref, cw_ref, cb_ref, wout_ref, out_ref, last_ref, act_s, carry = refs
    tm = hn_ref.shape[0]
    hn = hn_ref[...]

    def conv(cols):
        a = _dot(hn, win_ref[:, cols])
        if single_token:
            a2, a1 = h0_ref[:, cols], h1_ref[:, cols]
            a_ref[0, :, cols] = a1
            a_ref[1, :, cols] = a
        else:
            ext = jnp.concatenate([carry[:, cols], a], axis=0)
            a1 = ext[SUBLANES - 1:SUBLANES - 1 + tm]
            a2 = ext[SUBLANES - 2:SUBLANES - 2 + tm]
            tail = a[tm - SUBLANES:]
            carry[:, cols] = tail
            last_ref[0, :, cols] = tail[SUBLANES - (CONV_W - 1):]
        return cb_ref[:, cols] + cw_ref[0:1, cols] * a2 + cw_ref[1:2, cols] * a1 + cw_ref[2:3, cols] * a

    for j in range(D_FF // FF_TILE):
        gate = slice(j * FF_TILE, (j + 1) * FF_TILE)
        up = slice(D_FF + j * FF_TILE, D_FF + (j + 1) * FF_TILE)
        act_s[:, gate] = (jax.nn.silu(conv(gate)) * conv(up)).astype(BF16)
    out_ref[...] = h_ref[...] + _dot(act_s[...], wout_ref[...])


def _ple_kernel(h_ref, p_ref, g_ref, wgate_ref, wproj_ref, gpost_ref, y_ref):
    h = h_ref[...]
    gate = jax.nn.sigmoid(_dot((_rms(h) * g_ref[...]).astype(BF16), wgate_ref[...]))
    e = _rms(_dot(p_ref[...].astype(BF16), wproj_ref[...])) * gpost_ref[...]
    y_ref[...] = h + gate * e


def _post_attn_kernel(*refs, single_token):
    x_ref, o_ref, u_ref, v_ref, p_ref = refs[:5]
    mix_w, ffn_w, ple_w = refs[5:10], refs[10:14], refs[14:18]
    if single_token:
        h0_ref, h1_ref, y_ref, a_ref, act_s, h_s, hn_s, h2_s = refs[18:]
        _mix_proj_kernel(x_ref, o_ref, u_ref, v_ref, *mix_w, h_s, hn_s, single_token=True)
        _conv_ffn_kernel(h_s, hn_s, *ffn_w, h0_ref, h1_ref, h2_s, a_ref, act_s, single_token=True)
        _ple_kernel(h2_s, p_ref, *ple_w, y_ref)
        return
    y_ref, last_ref, act_s, carry, h_s, hn_s, h2_s = refs[18:]

    @pl.when(pl.program_id(1) == 0)
    def _():
        carry[...] = jnp.zeros(carry.shape, F32)

    _mix_proj_kernel(x_ref, o_ref, u_ref, v_ref, *mix_w, h_s, hn_s, single_token=False)
    _conv_ffn_kernel(h_s, hn_s, *ffn_w, h2_s, last_ref, act_s, carry, single_token=False)
    _ple_kernel(h2_s, p_ref, *ple_w, y_ref)


def _post_attn(x, o_lat, u, v, p, w, batch, seq, tm, hist=None):
    single_token = hist is not None
    nt = seq // tm
    tok = lambda width: pl.BlockSpec((tm, width), lambda b, t: (b * nt + t, 0))
    resident = lambda a: pl.BlockSpec(a.shape, lambda b, t: (0,) * a.ndim, pipeline_mode=pl.Buffered(1))
    ws, bias = (w['ws_diag'], w['bias_first']) if single_token else (w['w_s'], w['bias_tab'])
    weights = [w['w_uv_bd'], ws, bias, w['w_o'], w['g_ffn'],
               w['w_ff_in'], w['conv_w'], w['conv_b'], w['w_ff_out'],
               w['g_ple'], w['w_gate'], w['w_proj'], w['g_post']]
    in_specs = ([tok(D_MODEL), tok(N_HEADS * KV_RANK), tok(CG_WIDTH), tok(CG_WIDTH), tok(PLE_DIM)]
                + [resident(a) for a in weights])
    args = [x, o_lat, u, v, p] + weights
    scratch = [pltpu.VMEM((tm, D_FF), BF16)]
    if single_token:
        in_specs += [resident(hist[0]), resident(hist[1])]
        args += list(hist)
        last_spec = pl.BlockSpec((CONV_W - 1, tm, 2 * D_FF), lambda b, t: (0, 0, 0))
        last_shape = jax.ShapeDtypeStruct((CONV_W - 1, tm, 2 * D_FF), F32)
    else:
        last_spec = pl.BlockSpec((1, CONV_W - 1, 2 * D_FF), lambda b, t: (b * nt + t, 0, 0))
        last_shape = jax.ShapeDtypeStruct((batch * nt, CONV_W - 1, 2 * D_FF), F32)
        scratch.append(pltpu.VMEM((SUBLANES, 2 * D_FF), F32))
    scratch += [pltpu.VMEM((tm, D_MODEL), F32), pltpu.VMEM((tm, D_MODEL), BF16),
                pltpu.VMEM((tm, D_MODEL), F32)]
    return pl.pallas_call(
        functools.partial(_post_attn_kernel, single_token=single_token),
        grid=(batch, nt),
        in_specs=in_specs,
        out_specs=[tok(D_MODEL), last_spec],
        out_shape=[jax.ShapeDtypeStruct((batch * seq, D_MODEL), F32), last_shape],
        scratch_shapes=scratch,
        compiler_params=_params(("arbitrary", "arbitrary")),
        name="post_attn_single" if single_token else "post_attn",
    )(*args)


def _rope_tables(pos):
    inv = ROPE_THETA ** (-jnp.arange(ROPE_HALF, dtype=F32) * (2.0 / QK_ROPE))
    ang = pos.astype(F32)[:, None] * inv[None, :]
    cos, sin = jnp.cos(ang), jnp.sin(ang)
    n = pos.shape[0]
    ones = jnp.ones((n, QK_NOPE), F32)
    zq = jnp.zeros((n, QK_NOPE), F32)
    zp = jnp.zeros((n, HEAD_PAD - QK_NOPE - QK_ROPE), F32)
    ca = jnp.concatenate([ones, cos, cos, zp], axis=1)
    sb = jnp.concatenate([zq, -sin, sin, zp], axis=1)
    return ca, sb, cos.T, sin.T


def _prep_weights(i, attn_norm_w, w_in, q_norm_w, w_uq, q_nope_norm_w, q_rope_norm_w, kv_norm_w,
                  k_rope_norm_w, w_uk, k_nope_norm_w, w_uv, w_s, b_s, w_o, ffn_norm_w, w_ff_in,
                  conv_w, conv_b, w_ff_out, ple_norm_w, w_ple_gate, w_ple_proj, ple_post_norm_w):
    row = lambda g: g[i][None, :]
    swap = lambda a: jnp.concatenate([a[..., ROPE_HALF:], a[..., :ROPE_HALF]], axis=-1)
    win = w_in[i]
    wq = w_uq[i].reshape(Q_RANK, N_HEADS, QK_NOPE + QK_ROPE)
    nope, rope = wq[..., :QK_NOPE], wq[..., QK_NOPE:]
    pad = HEAD_PAD - QK_NOPE - QK_ROPE
    w_qt = jnp.concatenate([nope, rope, jnp.zeros((Q_RANK, N_HEADS, pad), F32)],
                           axis=-1).reshape(Q_RANK, QK_WIDTH).T
    col = lambda g: jnp.broadcast_to(g[:, None], (g.shape[0], LANES))
    kr_cols = win[:, OFF_KR:OFF_U]
    w_in_e = jnp.concatenate([win[:, :OFF_KR], win[:, OFF_U:], jnp.zeros((D_MODEL, QK_NOPE), F32),
                              kr_cols, swap(kr_cols)], axis=1)
    w_uk_e = jnp.concatenate([w_uk[i], jnp.zeros((KV_RANK, N_HEADS, HEAD_PAD - QK_NOPE), F32)],
                             axis=-1).reshape(KV_RANK, QK_WIDTH)
    gn, gr = k_nope_norm_w[i], k_rope_norm_w[i]
    ga = jnp.tile(jnp.concatenate([gn, gr, jnp.zeros((pad,), F32)]), N_HEADS)[None, :]
    gb = jnp.tile(jnp.concatenate([jnp.zeros((QK_NOPE,), F32), swap(gr), jnp.zeros((pad,), F32)]),
                  N_HEADS)[None, :]
    lane = jnp.arange(2 * HEAD_PAD)
    slot, off = lane // HEAD_PAD, lane % HEAD_PAD
    grp = jnp.where(off < QK_NOPE, 0, jnp.where(off < QK_NOPE + QK_ROPE, 1, 2))
    same = (slot[:, None] == slot[None, :]) & (grp[:, None] == grp[None, :]) & (grp[:, None] < 2)
    gm = jnp.where(same, jnp.where(grp[:, None] == 0, 1.0 / QK_NOPE, 1.0 / QK_ROPE), 0.0)
    w_ukt = w_uk[i].reshape(KV_RANK, N_HEADS * QK_NOPE).T
    eye = jnp.eye(N_HEADS, dtype=F32)
    w_uv_bd = jnp.einsum('rhd,hg->hrgd', w_uv[i], eye).reshape(N_HEADS * KV_RANK, N_HEADS * V_DIM)
    return {
        'g_attn': row(attn_norm_w),
        'w_in': w_in_e.astype(BF16),
        'g_q': row(q_norm_w), 'w_qt': w_qt.astype(BF16),
        'g_qn': col(q_nope_norm_w[i]), 'g_qr': col(q_rope_norm_w[i]),
        'g_kv': row(kv_norm_w), 'w_kvt': win[:, OFF_KV:OFF_KR].T.astype(BF16), 'g_kvt': col(kv_norm_w[i]),
        'w_krt': kr_cols.T.astype(BF16), 'g_krt': col(k_rope_norm_w[i]),
        'w_uk': w_uk_e.astype(BF16), 'gm': gm.astype(BF16), 'ga': ga, 'gb': gb,
        'w_ukt': w_ukt.astype(BF16), 'w_ukt_f32': w_ukt,
        'g_kn_row': jnp.tile(k_nope_norm_w[i], N_HEADS)[None, :],
        'w_uv_bd': w_uv_bd.astype(BF16),
        'w_s': w_s[i],
        'bias_tab': jnp.repeat(b_s[i].T, CG_DIM, axis=1),
        'ws_diag': jnp.repeat(w_s[i][:, 0, 0], CG_DIM)[None, :],
        'bias_first': jnp.repeat(b_s[i][:, 0], CG_DIM)[None, :],
        'w_o': w_o[i].astype(BF16), 'g_ffn': row(ffn_norm_w),
        'w_ff_in': w_ff_in[i].astype(BF16), 'conv_w': conv_w[i], 'conv_b': row(conv_b),
        'w_ff_out': w_ff_out[i].astype(BF16),
        'g_ple': row(ple_norm_w), 'w_gate': w_ple_gate[i].astype(BF16),
        'w_proj': w_ple_proj[i].astype(BF16), 'g_post': row(ple_post_norm_w),
    }


def kernel(x_prompt, x_sample, cache_ckv, cache_krope, state_conv, page_table, p_prompt, p_sample,
           attn_norm_w, w_in, q_norm_w, w_uq, q_nope_norm_w, q_rope_norm_w, kv_norm_w, k_rope_norm_w,
           w_uk, k_nope_norm_w, w_uv, w_s, b_s, w_o, ffn_norm_w, w_ff_in, conv_w, conv_b, w_ff_out,
           ple_norm_w, w_ple_gate, w_ple_proj, ple_post_norm_w):
    batch, seq, _ = x_prompt.shape
    n_samp, seq_s, _ = x_sample.shape
    depth = w_in.shape[0]
    assert depth == 1 and seq_s == 1 and seq % TOK_TILE == 0 and seq % CHUNK == 0
    past_len = page_table.shape[1] * PAGE_SIZE
    assert past_len % KV_TILE == 0
    w = _prep_weights(0, attn_norm_w, w_in, q_norm_w, w_uq, q_nope_norm_w, q_rope_norm_w, kv_norm_w,
                      k_rope_norm_w, w_uk, k_nope_norm_w, w_uv, w_s, b_s, w_o, ffn_norm_w, w_ff_in,
                      conv_w, conv_b, w_ff_out, ple_norm_w, w_ple_gate, w_ple_proj, ple_post_norm_w)

    n_tok = batch * seq
    tabs = _rope_tables(jnp.arange(seq, dtype=jnp.int32))
    xp = x_prompt.reshape(n_tok, D_MODEL)
    qt, k, ckv, vt, krt, u, v = _in_proj(xp, tabs, w, 2 * TOK_TILE)
    o_lat = _prompt_attn(qt, k, vt, batch, seq)
    y_prompt, a_last = _post_attn(xp, o_lat, u, v, p_prompt[0].reshape(n_tok, PLE_DIM), w,
                                  batch, seq, TOK_TILE)
    y_prompt = y_prompt.reshape(batch, seq, D_MODEL)
    n_keep = (seq - 1) % CHUNK + 1
    new_ckv_p = ckv.reshape(1, batch, seq, KV_RANK)
    new_kr_p = jnp.swapaxes(krt, 1, 2)[None]
    new_v_p = v.reshape(batch, seq, CG_WIDTH)[:, seq - n_keep:][None]
    new_conv_p = a_last.reshape(batch, seq // TOK_TILE, CONV_W - 1, 2 * D_FF)[:, -1][None]

    tabs_s = _rope_tables(jnp.full((n_samp,), past_len, dtype=jnp.int32))
    xs = x_sample.reshape(n_samp, D_MODEL)
    qt_s, k_s, ckv_s, _, krt_s, u_s, v_s = _in_proj(xs, tabs_s, w, n_samp)
    q3 = jnp.transpose(qt_s, (2, 0, 1)).astype(F32)
    kr_s = krt_s[0].T
    qa = _absorb_q(q3[:, :, :QK_NOPE].reshape(n_samp, N_HEADS * QK_NOPE), w)
    cache_rt = jnp.swapaxes(cache_krope[0], 1, 2)
    o_s = _sample_attn(page_table, q3, qa.reshape(n_samp, N_HEADS, KV_RANK),
                       k_s.reshape(n_samp, N_HEADS, HEAD_PAD), ckv_s.reshape(n_samp, 1, KV_RANK),
                       w, cache_ckv[0], cache_rt)
    o_s = o_s.reshape(n_samp, N_HEADS * KV_RANK).astype(BF16)
    hist = (state_conv[0, :, 0, :], state_conv[0, :, 1, :])
    y_sample, a_s = _post_attn(xs, o_s, u_s, v_s, p_sample[0].reshape(n_samp, PLE_DIM), w,
                               1, n_samp, n_samp, hist=hist)
    y_sample = y_sample.reshape(n_samp, 1, D_MODEL)
    new_conv_s = jnp.swapaxes(a_s, 0, 1)[None]

    return (y_prompt, y_sample, new_ckv_p, new_kr_p,
            ckv_s.reshape(1, n_samp, 1, KV_RANK), kr_s.reshape(1, n_samp, 1, QK_ROPE),
            new_v_p, v_s.reshape(1, n_samp, 1, CG_WIDTH), new_conv_p, new_conv_s)
```

```python
import functools

import jax
import jax.numpy as jnp
from jax import lax
from jax.experimental import pallas as pl
from jax.experimental.pallas import tpu as pltpu

F32 = jnp.float32
BF16 = jnp.bfloat16

D_MODEL = 1024
N_HEADS = 8
QK_NOPE = 64
QK_ROPE = 32
ROPE_HALF = QK_ROPE // 2
V_DIM = 64
Q_RANK = 256
KV_RANK = 128
CHUNK = 128
N_CG = 8
CG_WIDTH = 512
CG_DIM = 64
D_FF = 2816
CONV_W = 3
PLE_DIM = 256
PAGE_SIZE = 128
ROPE_THETA = 10000.0
EPS = 1e-6
SCALE = (QK_NOPE + QK_ROPE) ** -0.5
Q_SCALE = SCALE * 1.4426950408889634
OFF_KV = Q_RANK
OFF_KR = OFF_KV + KV_RANK
OFF_U = OFF_KR + QK_ROPE
OFF_V = OFF_U + CG_WIDTH
IN_WIDTH = OFF_V + CG_WIDTH
HEAD_PAD = 128
QK_WIDTH = N_HEADS * HEAD_PAD
Z_WIDTH = Q_RANK + KV_RANK + 2 * CG_WIDTH

LANES = 128
SUBLANES = 8
VMEM_LIMIT_BYTES = 56 * 1024 * 1024

TOK_TILE = 512
ATT_BLOCK = 1024
KEY_BLOCK = 512
ONES_ROWS = 16
FF_TILE = 256
KV_TILE = 2048

NT_DIMS = (((1,), (1,)), ((), ()))


def _params(semantics):
    return pltpu.CompilerParams(dimension_semantics=semantics,
                                vmem_limit_bytes=VMEM_LIMIT_BYTES)


def _rms(x):
    return x * lax.rsqrt(jnp.mean(x * x, axis=-1, keepdims=True) + EPS)


def _dot(a, b):
    return jnp.dot(a, b, preferred_element_type=F32)


def _dot_nt(a, b):
    return lax.dot_general(a, b, NT_DIMS, preferred_element_type=F32)


def _full(shape):
    zeros = (0,) * len(shape)
    return pl.BlockSpec(shape, lambda *_: zeros)


def _in_proj_kernel(x_ref, ca_ref, sb_ref, cost_ref, sint_ref,
                    g_attn_ref, w_in_ref, g_q_ref, w_qt_ref, g_qn_ref, g_qr_ref,
                    g_kv_ref, w_kvt_ref, g_kvt_ref, w_krt_ref, g_krt_ref, w_uk_ref, gm_ref, ga_ref, gb_ref,
                    qt_ref, k_ref, ckv_ref, vt_ref, krt_ref, u_ref, v_ref):
    tm = x_ref.shape[0]
    reps = tm // LANES
    cos, sin = cost_ref[...], sint_ref[...]
    hb = (_rms(x_ref[...]) * g_attn_ref[...]).astype(BF16)
    z = _dot(hb, w_in_ref[...])
    ckv = _rms(z[:, OFF_KV:OFF_KR]) * g_kv_ref[...]
    ckv_ref[...] = ckv
    kr_slot = z[:, Z_WIDTH:]
    ka = _dot(ckv.astype(BF16), w_uk_ref[...]) + jnp.tile(kr_slot, (1, N_HEADS))
    ct = _dot_nt(w_kvt_ref[...], hb)
    krt = _dot_nt(w_krt_ref[...], hb)
    qln = (_rms(z[:, :Q_RANK]) * g_q_ref[...]).astype(BF16)
    qt = _dot_nt(w_qt_ref[...], qln).reshape(N_HEADS, HEAD_PAD, tm)
    sq = (ka * ka).astype(BF16)
    gm = gm_ref[...]
    pair = 2 * HEAD_PAD
    ms = jnp.concatenate([_dot(sq[:, i * pair:(i + 1) * pair], gm) for i in range(N_HEADS // 2)], axis=1)
    krb = pltpu.roll(kr_slot, HEAD_PAD - QK_ROPE, axis=1)

    u_ref[...] = jax.nn.gelu(z[:, OFF_KR:OFF_KR + CG_WIDTH])
    v_ref[...] = jax.nn.gelu(z[:, OFF_KR + CG_WIDTH:OFF_KR + 2 * CG_WIDTH])
    ct = ct * lax.rsqrt(jnp.mean(ct * ct, axis=0, keepdims=True) + EPS)
    vt_ref[...] = (ct * jnp.tile(g_kvt_ref[...], (1, reps))).astype(BF16)
    krt = krt * lax.rsqrt(jnp.mean(krt * krt, axis=0, keepdims=True) + EPS) * jnp.tile(g_krt_ref[...], (1, reps))
    kx1, kx2 = krt[:ROPE_HALF], krt[ROPE_HALF:]
    krt_ref[0] = jnp.concatenate([kx1 * cos - kx2 * sin, kx1 * sin + kx2 * cos], axis=0)

    nope = qt[:, :QK_NOPE]
    rope = qt[:, QK_NOPE:QK_NOPE + QK_ROPE]
    g_qn = jnp.tile(g_qn_ref[...], (1, reps))[None]
    g_qr = jnp.tile(g_qr_ref[...], (1, reps))[None]
    nope = nope * lax.rsqrt(jnp.mean(nope * nope, axis=1, keepdims=True) + EPS) * (g_qn * Q_SCALE)
    rope = rope * lax.rsqrt(jnp.mean(rope * rope, axis=1, keepdims=True) + EPS) * (g_qr * Q_SCALE)
    x1, x2 = rope[:, :ROPE_HALF], rope[:, ROPE_HALF:]
    cos, sin = cos[None], sin[None]
    pad = jnp.zeros((N_HEADS, HEAD_PAD - QK_NOPE - QK_ROPE, tm), F32)
    qt_ref[...] = jnp.concatenate([nope, x1 * cos - x2 * sin, x1 * sin + x2 * cos, pad], axis=1).astype(BF16)

    ca = jnp.tile(ca_ref[...], (1, N_HEADS))
    sb = jnp.tile(sb_ref[...], (1, N_HEADS))
    k = lax.rsqrt(ms + EPS) * (ka * ga_ref[...] * ca + jnp.tile(krb, (1, N_HEADS)) * gb_ref[...] * sb)
    k_ref[...] = k.astype(BF16)


def _in_proj(x, tabs, w, tm):
    n = x.shape[0]
    ca, sb, cost, sint = tabs
    pos_tiles = ca.shape[0] // tm
    tok = lambda width: pl.BlockSpec((tm, width), lambda i: (i, 0))
    tok_t = lambda rows: pl.BlockSpec((rows, tm), lambda i: (0, i))
    pos = pl.BlockSpec((tm, HEAD_PAD), lambda i: (i % pos_tiles, 0))
    pos_t = pl.BlockSpec((ROPE_HALF, tm), lambda i: (0, i % pos_tiles))
    weights = [w['g_attn'], w['w_in'], w['g_q'], w['w_qt'], w['g_qn'], w['g_qr'],
               w['g_kv'], w['w_kvt'], w['g_kvt'], w['w_krt'], w['g_krt'], w['w_uk'], w['gm'], w['ga'], w['gb']]
    return pl.pallas_call(
        _in_proj_kernel,
        grid=(n // tm,),
        in_specs=[tok(D_MODEL), pos, pos, pos_t, pos_t]
                 + [_full(a.shape) for a in weights],
        out_specs=[pl.BlockSpec((N_HEADS, HEAD_PAD, tm), lambda i: (0, 0, i)),
                   tok(QK_WIDTH), tok(KV_RANK), tok_t(KV_RANK),
                   pl.BlockSpec((1, QK_ROPE, tm), lambda i: (i // pos_tiles, 0, i % pos_tiles)),
                   tok(CG_WIDTH), tok(CG_WIDTH)],
        out_shape=[jax.ShapeDtypeStruct((N_HEADS, HEAD_PAD, n), BF16),
                   jax.ShapeDtypeStruct((n, QK_WIDTH), BF16),
                   jax.ShapeDtypeStruct((n, KV_RANK), F32),
                   jax.ShapeDtypeStruct((KV_RANK, n), BF16),
                   jax.ShapeDtypeStruct((n // (pos_tiles * tm), QK_ROPE, pos_tiles * tm), F32),
                   jax.ShapeDtypeStruct((n, CG_WIDTH), F32),
                   jax.ShapeDtypeStruct((n, CG_WIDTH), F32)],
        compiler_params=_params(("parallel",)),
        name="in_proj",
    )(x, ca, sb, cost, sint, *weights)


def _prompt_attn_kernel(qt_ref, k_ref, vt_ref, o_ref, m_s, acc_s, s_s):
    qi = pl.program_id(1)
    kb = KEY_BLOCK
    per_q = ATT_BLOCK // kb
    m_s[...] = jnp.full(m_s.shape, -jnp.inf, F32)
    acc_s[...] = jnp.zeros(acc_s.shape, F32)
    ones = jnp.ones((ONES_ROWS, kb), BF16)

    def key_block(off, first_q, masked):
        cols = slice(first_q, ATT_BLOCK)
        vals = jnp.concatenate([vt_ref[:, pl.ds(off, kb)], ones], axis=0)
        if masked:
            key = lax.broadcasted_iota(jnp.int32, (kb, ATT_BLOCK - first_q), 0)
            qry = lax.broadcasted_iota(jnp.int32, (kb, ATT_BLOCK - first_q), 1)
            visible = key <= qry

        def score_stage(h):
            s = _dot(k_ref[pl.ds(off, kb), h * HEAD_PAD:(h + 1) * HEAD_PAD], qt_ref[h, :, cols])
            if masked:
                s = jnp.where(visible, s, -jnp.inf)
            m_old = m_s[h, :, cols]
            m_new = jnp.maximum(m_old, jnp.max(s, axis=0, keepdims=True))
            m_s[h, :, cols] = m_new
            s_s[h % 2, :, cols] = s
            return m_old, m_new

        def value_stage(h, m_old, m_new):
            p = jnp.exp2(s_s[h % 2, :, cols] - m_new)
            acc_s[h, :, cols] = jnp.exp2(m_old - m_new) * acc_s[h, :, cols] + _dot(vals, p.astype(BF16))

        pending = score_stage(0)
        for h in range(N_HEADS):
            nxt = score_stage(h + 1) if h + 1 < N_HEADS else None
            value_stage(h, *pending)
            pending = nxt

    def body(j, carry):
        key_block(pl.multiple_of(j * kb, kb), 0, False)
        return carry

    lax.fori_loop(0, qi * per_q, body, 0)
    for d in range(per_q):
        key_block(pl.multiple_of(qi * ATT_BLOCK + d * kb, kb), d * kb, True)
    for h in range(N_HEADS):
        acc = acc_s[h]
        o_t = acc[:KV_RANK] / acc[KV_RANK:KV_RANK + 1]
        o_ref[:, h * HEAD_PAD:(h + 1) * HEAD_PAD] = o_t.T.astype(BF16)


def _prompt_attn(qt, k, vt, batch, seq):
    nq = seq // ATT_BLOCK
    return pl.pallas_call(
        _prompt_attn_kernel,
        grid=(batch, nq),
        in_specs=[pl.BlockSpec((N_HEADS, HEAD_PAD, ATT_BLOCK), lambda b, i: (0, 0, b * nq + i)),
                  pl.BlockSpec((seq, QK_WIDTH), lambda b, i: (b, 0)),
                  pl.BlockSpec((KV_RANK, seq), lambda b, i: (0, b))],
        out_specs=pl.BlockSpec((ATT_BLOCK, N_HEADS * KV_RANK), lambda b, i: (b * nq + i, 0)),
        out_shape=jax.ShapeDtypeStruct((batch * seq, N_HEADS * KV_RANK), BF16),
        scratch_shapes=[pltpu.VMEM((N_HEADS, 1, ATT_BLOCK), F32),
                        pltpu.VMEM((N_HEADS, KV_RANK + ONES_ROWS, ATT_BLOCK), F32),
                        pltpu.VMEM((2, KEY_BLOCK, ATT_BLOCK), F32)],
        compiler_params=_params(("parallel", "parallel")),
        name="prompt_attn",
    )(qt, k, vt)


def _absorb_q_kernel(qn_ref, g_ref, wukt_ref, qa_ref):
    qg = qn_ref[...] * g_ref[...]
    lane = lax.broadcasted_iota(jnp.int32, qg.shape, 1)
    for h in range(N_HEADS):
        q_h = jnp.where((lane >= h * QK_NOPE) & (lane < (h + 1) * QK_NOPE), qg, 0.0)
        qa_ref[:, h * KV_RANK:(h + 1) * KV_RANK] = jnp.dot(
            q_h, wukt_ref[...], precision=lax.Precision.HIGHEST, preferred_element_type=F32)


def _absorb_q(qn, w):
    n_samp = qn.shape[0]
    args = [qn, w['g_kn_row'], w['w_ukt_f32']]
    return pl.pallas_call(
        _absorb_q_kernel,
        grid=(1,),
        in_specs=[_full(a.shape) for a in args],
        out_specs=_full((n_samp, N_HEADS * KV_RANK)),
        out_shape=jax.ShapeDtypeStruct((n_samp, N_HEADS * KV_RANK), F32),
        compiler_params=_params(("arbitrary",)),
        name="absorb_q",
    )(*args)


def _sample_attn_kernel(pt_ref, per_ref, wukb_ref,
                        cache_c, cache_r, o_ref,
                        wext, cbuf, rbuf, sems, *, n_pages):
    b = pl.program_id(0)
    last = pl.num_programs(0) - 1
    slot = lax.rem(b, 2)
    past = n_pages * PAGE_SIZE
    n_tiles = past // KV_TILE
    pages_per_tile = KV_TILE // PAGE_SIZE
    nw = N_HEADS * QK_NOPE
    nxt = jnp.minimum(b + 1, last)

    def start_page(bb, sl, i, prio=0):
        page = pt_ref[bb * n_pages + i]
        off = pl.multiple_of(i * PAGE_SIZE, PAGE_SIZE)
        pltpu.make_async_copy(cache_c.at[page], cbuf.at[sl, pl.ds(off, PAGE_SIZE)],
                              sems.at[0, sl]).start(priority=prio)
        pltpu.make_async_copy(cache_r.at[page], rbuf.at[sl, i], sems.at[1, sl]).start(priority=1 - prio)

    def wait_pages(sl):
        pltpu.make_async_copy(cbuf.at[sl], cbuf.at[sl], sems.at[0, sl]).wait()
        pltpu.make_async_copy(rbuf.at[sl], rbuf.at[sl], sems.at[1, sl]).wait()

    @pl.when(b == 0)
    def _():
        def issue(i, carry):
            start_page(0, 0, i)
            return carry
        lax.fori_loop(0, n_pages, issue, 0)
        wext[:nw, :] = wukb_ref[...]

    q = per_ref[0, 0 * N_HEADS:1 * N_HEADS]
    qa = per_ref[0, 1 * N_HEADS:2 * N_HEADS]
    k_own = per_ref[0, 2 * N_HEADS:3 * N_HEADS]
    c_own = per_ref[0, 3 * N_HEADS:3 * N_HEADS + 1]
    wext[nw:, :] = jnp.concatenate(
        [qa, jnp.zeros((wext.shape[0] - nw - N_HEADS, KV_RANK), F32)], axis=0).astype(BF16)
    qr = q[:, QK_NOPE:QK_NOPE + QK_ROPE].astype(BF16)

    def scores(c_blk, krt_blk):
        tk = c_blk.shape[0]
        cb = c_blk.astype(BF16)
        knt = _dot_nt(wext[...], cb)
        kn3 = knt[:nw].reshape(N_HEADS, QK_NOPE, tk)
        r = lax.rsqrt(jnp.sum(kn3 * kn3, axis=1) * (1.0 / QK_NOPE) + EPS)
        s_rope = _dot(qr, krt_blk.astype(BF16))
        return cb, knt[nw:nw + N_HEADS] * r + s_rope

    def value_part(m_j, p_j, cb_j):
        return m_j, jnp.sum(p_j, axis=-1, keepdims=True), _dot(p_j.astype(BF16), cb_j)

    wait_pages(slot)
    parts, pending = [], None
    for j in range(n_tiles):
        keys = slice(j * KV_TILE, (j + 1) * KV_TILE)
        krt = jnp.concatenate([rbuf[slot, j * pages_per_tile + i] for i in range(pages_per_tile)], axis=1)
        cb, s = scores(cbuf[slot, keys, :], krt)
        for i in range(pages_per_tile):
            start_page(nxt, 1 - slot, j * pages_per_tile + i, prio=i % 2)
        m_j = jnp.max(s, axis=-1, keepdims=True)
        if pending is not None:
            parts.append(value_part(*pending))
        pending = (m_j, jnp.exp2(s - m_j), cb)
    parts.append(value_part(*pending))

    s_own = jnp.sum(q * k_own, axis=-1, keepdims=True)
    m = s_own
    for m_j, _, _ in parts:
        m = jnp.maximum(m, m_j)
    p_own = jnp.exp2(s_own - m)
    round_bf16 = lambda a: a.astype(BF16).astype(F32)
    l = p_own
    acc = round_bf16(p_own) * round_bf16(c_own)
    for m_j, l_j, acc_j in parts:
        w_j = jnp.exp2(m_j - m)
        l = l + w_j * l_j
        acc = acc + w_j * acc_j
    o_ref[0] = acc / l

    @pl.when(b == last)
    def _():
        wait_pages(1 - slot)


def _sample_attn(page_table, q, qa, k_own, c_own, w, cache_c, cache_rt):
    n_samp, n_pages = page_table.shape
    past = n_pages * PAGE_SIZE
    wext_rows = N_HEADS * QK_NOPE + 2 * SUBLANES
    per = jnp.concatenate([q, qa, k_own.astype(F32), jnp.broadcast_to(c_own, (n_samp, N_HEADS, KV_RANK))],
                          axis=1)
    grid_spec = pltpu.PrefetchScalarGridSpec(
        num_scalar_prefetch=1,
        grid=(n_samp,),
        in_specs=[pl.BlockSpec((1,) + per.shape[1:], lambda b, pt: (b, 0, 0)),
                  pl.BlockSpec(w['w_ukt'].shape, lambda b, pt: (0, 0)),
                  pl.BlockSpec(memory_space=pl.ANY),
                  pl.BlockSpec(memory_space=pl.ANY)],
        out_specs=pl.BlockSpec((1, N_HEADS, KV_RANK), lambda b, pt: (b, 0, 0)),
        scratch_shapes=[pltpu.VMEM((wext_rows, KV_RANK), BF16),
                        pltpu.VMEM((2, past, KV_RANK), F32),
                        pltpu.VMEM((2, n_pages, QK_ROPE, PAGE_SIZE), F32),
                        pltpu.SemaphoreType.DMA((2, 2))])
    return pl.pallas_call(
        functools.partial(_sample_attn_kernel, n_pages=n_pages),
        grid_spec=grid_spec,
        out_shape=jax.ShapeDtypeStruct((n_samp, N_HEADS, KV_RANK), F32),
        compiler_params=_params(("arbitrary",)),
        name="sample_attn",
    )(page_table.reshape(-1), per, w['w_ukt'], cache_c, cache_rt)


def _mix_proj_kernel(x_ref, o_ref, u_ref, v_ref, wuv_ref, ws_ref, bias_ref, wo_ref, g_ref,
                     h_ref, hn_ref, *, single_token):
    tm = x_ref.shape[0]
    attn = _dot(o_ref[...], wuv_ref[...])
    v = v_ref[...]
    if single_token:
        mixed = v * ws_ref[...] + bias_ref[...]
    else:
        vb = v.astype(BF16)
        row = lax.broadcasted_iota(jnp.int32, (CHUNK, CHUNK), 0)
        col = lax.broadcasted_iota(jnp.int32, (CHUNK, CHUNK), 1)
        w_tril = [jnp.where(col <= row, ws_ref[g], 0.0).astype(BF16) for g in range(N_CG)]
        w_pair = [jnp.concatenate([w_tril[2 * j], w_tril[2 * j + 1]], axis=1) for j in range(N_CG // 2)]
        low_half = lax.broadcasted_iota(jnp.int32, (CHUNK, LANES), 1) < CG_DIM
        zero = jnp.zeros((CHUNK, LANES), BF16)
        chunks = []
        for c in range(tm // CHUNK):
            cols = []
            for j in range(CG_WIDTH // LANES):
                vp = vb[c * CHUNK:(c + 1) * CHUNK, j * LANES:(j + 1) * LANES]
                stacked = jnp.concatenate([jnp.where(low_half, vp, zero), jnp.where(low_half, zero, vp)], axis=0)
                cols.append(_dot(w_pair[j], stacked))
            chunks.append(jnp.concatenate(cols, axis=1) + bias_ref[...])
        mixed = jnp.concatenate(chunks, axis=0)
    sg = u_ref[...] * mixed
    width = attn.shape[1]
    h = x_ref[...] + _dot(attn.astype(BF16), wo_ref[:width, :]) + _dot(sg.astype(BF16), wo_ref[width:, :])
    h_ref[...] = h
    hn_ref[...] = (_rms(h) * g_ref[...]).astype(BF16)


def _conv_ffn_kernel(*refs, single_token):
    if single_token:
        h_ref, hn_ref, win_ref, cw_ref, cb_ref, wout_ref, h0_ref, h1_ref, out_ref, a_ref, act_s = refs
    else:
        h_ref, hn_ref, win_ref, cw_ref, cb_ref, wout_ref, out_ref, last_ref, act_s, carry = refs
    tm = hn_ref.shape[0]
    hn = hn_ref[...]

    def conv(cols):
        a = _dot(hn, win_ref[:, cols])
        if single_token:
            a2, a1 = h0_ref[:, cols], h1_ref[:, cols]
            a_ref[0, :, cols] = a1
            a_ref[1, :, cols] = a
        else:
            ext = jnp.concatenate([carry[:, cols], a], axis=0)
            a1 = ext[SUBLANES - 1:SUBLANES - 1 + tm]
            a2 = ext[SUBLANES - 2:SUBLANES - 2 + tm]
            tail = a[tm - SUBLANES:]
            carry[:, cols] = tail
            last_ref[0, :, cols] = tail[SUBLANES - (CONV_W - 1):]
        return cb_ref[:, cols] + cw_ref[0:1, cols] * a2 + cw_ref[1:2, cols] * a1 + cw_ref[2:3, cols] * a

    for j in range(D_FF // FF_TILE):
        gate = slice(j * FF_TILE, (j + 1) * FF_TILE)
        up = slice(D_FF + j * FF_TILE, D_FF + (j + 1) * FF_TILE)
        act_s[:, gate] = (jax.nn.silu(conv(gate)) * conv(up)).astype(BF16)
    out_ref[...] = h_ref[...] + _dot(act_s[...], wout_ref[...])


def _ple_kernel(h_ref, p_ref, g_ref, wgate_ref, wproj_ref, gpost_ref, y_ref):
    h = h_ref[...]
    gate = jax.nn.sigmoid(_dot((_rms(h) * g_ref[...]).astype(BF16), wgate_ref[...]))
    e = _rms(_dot(p_ref[...].astype(BF16), wproj_ref[...])) * gpost_ref[...]
    y_ref[...] = h + gate * e


def _post_attn_kernel(*refs, single_token):
    x_ref, o_ref, u_ref, v_ref, p_ref = refs[:5]
    mix_w, ffn_w, ple_w = refs[5:10], refs[10:14], refs[14:18]
    if single_token:
        h0_ref, h1_ref, y_ref, a_ref, act_s, h_s, hn_s, h2_s = refs[18:]
        _mix_proj_kernel(x_ref, o_ref, u_ref, v_ref, *mix_w, h_s, hn_s, single_token=True)
        _conv_ffn_kernel(h_s, hn_s, *ffn_w, h0_ref, h1_ref, h2_s, a_ref, act_s, single_token=True)
        _ple_kernel(h2_s, p_ref, *ple_w, y_ref)
        return
    y_ref, last_ref, act_s, carry, h_s, hn_s, h2_s = refs[18:]

    @pl.when(pl.program_id(1) == 0)
    def _():
        carry[...] = jnp.zeros(carry.shape, F32)

    _mix_proj_kernel(x_ref, o_ref, u_ref, v_ref, *mix_w, h_s, hn_s, single_token=False)
    _conv_ffn_kernel(h_s, hn_s, *ffn_w, h2_s, last_ref, act_s, carry, single_token=False)
    _ple_kernel(h2_s, p_ref, *ple_w, y_ref)


def _post_attn(x, o_lat, u, v, p, w, batch, seq, tm, hist=None):
    single_token = hist is not None
    nt = seq // tm
    tok = lambda width: pl.BlockSpec((tm, width), lambda b, t: (b * nt + t, 0))
    resident = lambda a: pl.BlockSpec(a.shape, lambda b, t: (0,) * a.ndim, pipeline_mode=pl.Buffered(1))
    ws, bias = (w['ws_diag'], w['bias_first']) if single_token else (w['w_s'], w['bias_tab'])
    weights = [w['w_uv_bd'], ws, bias, w['w_o'], w['g_ffn'],
               w['w_ff_in'], w['conv_w'], w['conv_b'], w['w_ff_out'],
               w['g_ple'], w['w_gate'], w['w_proj'], w['g_post']]
    in_specs = ([tok(D_MODEL), tok(N_HEADS * KV_RANK), tok(CG_WIDTH), tok(CG_WIDTH), tok(PLE_DIM)]
                + [resident(a) for a in weights])
    args = [x, o_lat, u, v, p] + weights
    scratch = [pltpu.VMEM((tm, D_FF), BF16)]
    if single_token:
        in_specs += [resident(hist[0]), resident(hist[1])]
        args += list(hist)
        last_spec = pl.BlockSpec((CONV_W - 1, tm, 2 * D_FF), lambda b, t: (0, 0, 0))
        last_shape = jax.ShapeDtypeStruct((CONV_W - 1, tm, 2 * D_FF), F32)
    else:
        last_spec = pl.BlockSpec((1, CONV_W - 1, 2 * D_FF), lambda b, t: (b * nt + t, 0, 0))
        last_shape = jax.ShapeDtypeStruct((batch * nt, CONV_W - 1, 2 * D_FF), F32)
        scratch.append(pltpu.VMEM((SUBLANES, 2 * D_FF), F32))
    scratch += [pltpu.VMEM((tm, D_MODEL), F32), pltpu.VMEM((tm, D_MODEL), BF16),
                pltpu.VMEM((tm, D_MODEL), F32)]
    return pl.pallas_call(
        functools.partial(_post_attn_kernel, single_token=single_token),
        grid=(batch, nt),
        in_specs=in_specs,
        out_specs=[tok(D_MODEL), last_spec],
        out_shape=[jax.ShapeDtypeStruct((batch * seq, D_MODEL), F32), last_shape],
        scratch_shapes=scratch,
        compiler_params=_params(("arbitrary", "arbitrary")),
        name="post_attn_single" if single_token else "post_attn",
    )(*args)


def _rope_tables(pos):
    inv = ROPE_THETA ** (-jnp.arange(ROPE_HALF, dtype=F32) * (2.0 / QK_ROPE))
    ang = pos.astype(F32)[:, None] * inv[None, :]
    cos, sin = jnp.cos(ang), jnp.sin(ang)
    n = pos.shape[0]
    ones = jnp.ones((n, QK_NOPE), F32)
    zq = jnp.zeros((n, QK_NOPE), F32)
    zp = jnp.zeros((n, HEAD_PAD - QK_NOPE - QK_ROPE), F32)
    ca = jnp.concatenate([ones, cos, cos, zp], axis=1)
    sb = jnp.concatenate([zq, -sin, sin, zp], axis=1)
    return ca, sb, cos.T, sin.T


def _prep_weights(i, attn_norm_w, w_in, q_norm_w, w_uq, q_nope_norm_w, q_rope_norm_w, kv_norm_w,
                  k_rope_norm_w, w_uk, k_nope_norm_w, w_uv, w_s, b_s, w_o, ffn_norm_w, w_ff_in,
                  conv_w, conv_b, w_ff_out, ple_norm_w, w_ple_gate, w_ple_proj, ple_post_norm_w):
    row = lambda g: g[i][None, :]
    swap = lambda a: jnp.concatenate([a[..., ROPE_HALF:], a[..., :ROPE_HALF]], axis=-1)
    win = w_in[i]
    wq = w_uq[i].reshape(Q_RANK, N_HEADS, QK_NOPE + QK_ROPE)
    nope, rope = wq[..., :QK_NOPE], wq[..., QK_NOPE:]
    pad = HEAD_PAD - QK_NOPE - QK_ROPE
    w_qt = jnp.concatenate([nope, rope, jnp.zeros((Q_RANK, N_HEADS, pad), F32)],
                           axis=-1).reshape(Q_RANK, QK_WIDTH).T
    col = lambda g: jnp.broadcast_to(g[:, None], (g.shape[0], LANES))
    kr_cols = win[:, OFF_KR:OFF_U]
    w_in_e = jnp.concatenate([win[:, :OFF_KR], win[:, OFF_U:], jnp.zeros((D_MODEL, QK_NOPE), F32),
                              kr_cols, swap(kr_cols)], axis=1)
    w_uk_e = jnp.concatenate([w_uk[i], jnp.zeros((KV_RANK, N_HEADS, HEAD_PAD - QK_NOPE), F32)],
                             axis=-1).reshape(KV_RANK, QK_WIDTH)
    gn, gr = k_nope_norm_w[i], k_rope_norm_w[i]
    ga = jnp.tile(jnp.concatenate([gn, gr, jnp.zeros((pad,), F32)]), N_HEADS)[None, :]
    gb = jnp.tile(jnp.concatenate([jnp.zeros((QK_NOPE,), F32), swap(gr), jnp.zeros((pad,), F32)]),
                  N_HEADS)[None, :]
    lane = jnp.arange(2 * HEAD_PAD)
    slot, off = lane // HEAD_PAD, lane % HEAD_PAD
    grp = jnp.where(off < QK_NOPE, 0, jnp.where(off < QK_NOPE + QK_ROPE, 1, 2))
    same = (slot[:, None] == slot[None, :]) & (grp[:, None] == grp[None, :]) & (grp[:, None] < 2)
    gm = jnp.where(same, jnp.where(grp[:, None] == 0, 1.0 / QK_NOPE, 1.0 / QK_ROPE), 0.0)
    w_ukt = w_uk[i].reshape(KV_RANK, N_HEADS * QK_NOPE).T
    eye = jnp.eye(N_HEADS, dtype=F32)
    w_uv_bd = jnp.einsum('rhd,hg->hrgd', w_uv[i], eye).reshape(N_HEADS * KV_RANK, N_HEADS * V_DIM)
    return {
        'g_attn': row(attn_norm_w),
        'w_in': w_in_e.astype(BF16),
        'g_q': row(q_norm_w), 'w_qt': w_qt.astype(BF16),
        'g_qn': col(q_nope_norm_w[i]), 'g_qr': col(q_rope_norm_w[i]),
        'g_kv': row(kv_norm_w), 'w_kvt': win[:, OFF_KV:OFF_KR].T.astype(BF16), 'g_kvt': col(kv_norm_w[i]),
        'w_krt': kr_cols.T.astype(BF16), 'g_krt': col(k_rope_norm_w[i]),
        'w_uk': w_uk_e.astype(BF16), 'gm': gm.astype(BF16), 'ga': ga, 'gb': gb,
        'w_ukt': w_ukt.astype(BF16), 'w_ukt_f32': w_ukt,
        'g_kn_row': jnp.tile(k_nope_norm_w[i], N_HEADS)[None, :],
        'w_uv_bd': w_uv_bd.astype(BF16),
        'w_s': w_s[i],
        'bias_tab': jnp.repeat(b_s[i].T, CG_DIM, axis=1),
        'ws_diag': jnp.repeat(w_s[i][:, 0, 0], CG_DIM)[None, :],
        'bias_first': jnp.repeat(b_s[i][:, 0], CG_DIM)[None, :],
        'w_o': w_o[i].astype(BF16), 'g_ffn': row(ffn_norm_w),
        'w_ff_in': w_ff_in[i].astype(BF16), 'conv_w': conv_w[i], 'conv_b': row(conv_b),
        'w_ff_out': w_ff_out[i].astype(BF16),
        'g_ple': row(ple_norm_w), 'w_gate': w_ple_gate[i].astype(BF16),
        'w_proj': w_ple_proj[i].astype(BF16), 'g_post': row(ple_post_norm_w),
    }


def kernel(x_prompt, x_sample, cache_ckv, cache_krope, state_conv, page_table, p_prompt, p_sample,
           attn_norm_w, w_in, q_norm_w, w_uq, q_nope_norm_w, q_rope_norm_w, kv_norm_w, k_rope_norm_w,
           w_uk, k_nope_norm_w, w_uv, w_s, b_s, w_o, ffn_norm_w, w_ff_in, conv_w, conv_b, w_ff_out,
           ple_norm_w, w_ple_gate, w_ple_proj, ple_post_norm_w):
    batch, seq, _ = x_prompt.shape
    n_samp, seq_s, _ = x_sample.shape
    depth = w_in.shape[0]
    assert depth == 1 and seq_s == 1 and seq % TOK_TILE == 0 and seq % CHUNK == 0
    past_len = page_table.shape[1] * PAGE_SIZE
    assert past_len % KV_TILE == 0
    w = _prep_weights(0, attn_norm_w, w_in, q_norm_w, w_uq, q_nope_norm_w, q_rope_norm_w, kv_norm_w,
                      k_rope_norm_w, w_uk, k_nope_norm_w, w_uv, w_s, b_s, w_o, ffn_norm_w, w_ff_in,
                      conv_w, conv_b, w_ff_out, ple_norm_w, w_ple_gate, w_ple_proj, ple_post_norm_w)

    n_tok = batch * seq
    tabs = _rope_tables(jnp.arange(seq, dtype=jnp.int32))
    xp = x_prompt.reshape(n_tok, D_MODEL)
    qt, k, ckv, vt, krt, u, v = _in_proj(xp, tabs, w, 2 * TOK_TILE)
    o_lat = _prompt_attn(qt, k, vt, batch, seq)
    y_prompt, a_last = _post_attn(xp, o_lat, u, v, p_prompt[0].reshape(n_tok, PLE_DIM), w,
                                  batch, seq, TOK_TILE)
    y_prompt = y_prompt.reshape(batch, seq, D_MODEL)
    n_keep = (seq - 1) % CHUNK + 1
    new_ckv_p = ckv.reshape(1, batch, seq, KV_RANK)
    new_kr_p = jnp.swapaxes(krt, 1, 2)[None]
    new_v_p = v.reshape(batch, seq, CG_WIDTH)[:, seq - n_keep:][None]
    new_conv_p = a_last.reshape(batch, seq // TOK_TILE, CONV_W - 1, 2 * D_FF)[:, -1][None]

    tabs_s = _rope_tables(jnp.full((n_samp,), past_len, dtype=jnp.int32))
    xs = x_sample.reshape(n_samp, D_MODEL)
    qt_s, k_s, ckv_s, _, krt_s, u_s, v_s = _in_proj(xs, tabs_s, w, n_samp)
    q3 = jnp.transpose(qt_s, (2, 0, 1)).astype(F32)
    kr_s = krt_s[0].T
    qa = _absorb_q(q3[:, :, :QK_NOPE].reshape(n_samp, N_HEADS * QK_NOPE), w)
    cache_rt = jnp.swapaxes(cache_krope[0], 1, 2)
    o_s = _sample_attn(page_table, q3, qa.reshape(n_samp, N_HEADS, KV_RANK),
                       k_s.reshape(n_samp, N_HEADS, HEAD_PAD), ckv_s.reshape(n_samp, 1, KV_RANK),
                       w, cache_ckv[0], cache_rt)
    o_s = o_s.reshape(n_samp, N_HEADS * KV_RANK).astype(BF16)
    hist = (state_conv[0, :, 0, :], state_conv[0, :, 1, :])
    y_sample, a_s = _post_attn(xs, o_s, u_s, v_s, p_sample[0].reshape(n_samp, PLE_DIM), w,
                               1, n_samp, n_samp, hist=hist)
    y_sample = y_sample.reshape(n_samp, 1, D_MODEL)
    new_conv_s = jnp.swapaxes(a_s, 0, 1)[None]

    return (y_prompt, y_sample, new_ckv_p, new_kr_p,
            ckv_s.reshape(1, n_samp, 1, KV_RANK), kr_s.reshape(1, n_samp, 1, QK_ROPE),
            new_v_p, v_s.reshape(1, n_samp, 1, CG_WIDTH), new_conv_p, new_conv_s)
```

```python
import functools

import jax
import jax.numpy as jnp
from jax import lax
from jax.experimental import pallas as pl
from jax.experimental.pallas import tpu as pltpu

F32 = jnp.float32
BF16 = jnp.bfloat16

D_MODEL = 1024
N_HEADS = 8
QK_NOPE = 64
QK_ROPE = 32
ROPE_HALF = QK_ROPE // 2
V_DIM = 64
Q_RANK = 256
KV_RANK = 128
CHUNK = 128
N_CG = 8
CG_WIDTH = 512
CG_DIM = 64
D_FF = 2816
CONV_W = 3
PLE_DIM = 256
PAGE_SIZE = 128
ROPE_THETA = 10000.0
EPS = 1e-6
SCALE = (QK_NOPE + QK_ROPE) ** -0.5
Q_SCALE = SCALE * 1.4426950408889634
OFF_KV = Q_RANK
OFF_KR = OFF_KV + KV_RANK
OFF_U = OFF_KR + QK_ROPE
OFF_V = OFF_U + CG_WIDTH
IN_WIDTH = OFF_V + CG_WIDTH
HEAD_PAD = 128
QK_WIDTH = N_HEADS * HEAD_PAD
Z_WIDTH = Q_RANK + KV_RANK + 2 * CG_WIDTH

LANES = 128
SUBLANES = 8
VMEM_LIMIT_BYTES = 56 * 1024 * 1024

TOK_TILE = 512
ATT_BLOCK = 1024
KEY_BLOCK = 512
ONES_ROWS = 16
FF_TILE = 256
KV_TILE = 2048
ISSUE_TILES = 4

NT_DIMS = (((1,), (1,)), ((), ()))


def _params(semantics):
    return pltpu.CompilerParams(dimension_semantics=semantics,
                                vmem_limit_bytes=VMEM_LIMIT_BYTES)


def _rms(x):
    return x * lax.rsqrt(jnp.mean(x * x, axis=-1, keepdims=True) + EPS)


def _dot(a, b):
    return jnp.dot(a, b, preferred_element_type=F32)


def _dot_nt(a, b):
    return lax.dot_general(a, b, NT_DIMS, preferred_element_type=F32)


def _full(shape):
    zeros = (0,) * len(shape)
    return pl.BlockSpec(shape, lambda *_: zeros)


def _in_proj_kernel(x_ref, ca_ref, sb_ref, cost_ref, sint_ref,
                    g_attn_ref, w_in_ref, g_q_ref, w_qt_ref, g_qn_ref, g_qr_ref,
                    g_kv_ref, w_kvt_ref, g_kvt_ref, w_krt_ref, g_krt_ref, w_uk_ref, gm_ref, ga_ref, gb_ref,
                    qt_ref, k_ref, ckv_ref, vt_ref, krt_ref, u_ref, v_ref):
    tm = x_ref.shape[0]
    reps = tm // LANES
    cos, sin = cost_ref[...], sint_ref[...]
    hb = (_rms(x_ref[...]) * g_attn_ref[...]).astype(BF16)
    z = _dot(hb, w_in_ref[...])
    ckv = _rms(z[:, OFF_KV:OFF_KR]) * g_kv_ref[...]
    ckv_ref[...] = ckv
    kr_slot = z[:, Z_WIDTH:]
    ka = _dot(ckv.astype(BF16), w_uk_ref[...]) + jnp.tile(kr_slot, (1, N_HEADS))
    ct = _dot_nt(w_kvt_ref[...], hb)
    krt = _dot_nt(w_krt_ref[...], hb)
    qln = (_rms(z[:, :Q_RANK]) * g_q_ref[...]).astype(BF16)
    qt = _dot_nt(w_qt_ref[...], qln).reshape(N_HEADS, HEAD_PAD, tm)
    sq = (ka * ka).astype(BF16)
    gm = gm_ref[...]
    pair = 2 * HEAD_PAD
    ms = jnp.concatenate([_dot(sq[:, i * pair:(i + 1) * pair], gm) for i in range(N_HEADS // 2)], axis=1)
    krb = pltpu.roll(kr_slot, HEAD_PAD - QK_ROPE, axis=1)

    u_ref[...] = jax.nn.gelu(z[:, OFF_KR:OFF_KR + CG_WIDTH])
    v_ref[...] = jax.nn.gelu(z[:, OFF_KR + CG_WIDTH:OFF_KR + 2 * CG_WIDTH])
    ct = ct * lax.rsqrt(jnp.mean(ct * ct, axis=0, keepdims=True) + EPS)
    vt_ref[...] = (ct * jnp.tile(g_kvt_ref[...], (1, reps))).astype(BF16)
    krt = krt * lax.rsqrt(jnp.mean(krt * krt, axis=0, keepdims=True) + EPS) * jnp.tile(g_krt_ref[...], (1, reps))
    kx1, kx2 = krt[:ROPE_HALF], krt[ROPE_HALF:]
    krt_ref[0] = jnp.concatenate([kx1 * cos - kx2 * sin, kx1 * sin + kx2 * cos], axis=0)

    nope = qt[:, :QK_NOPE]
    rope = qt[:, QK_NOPE:QK_NOPE + QK_ROPE]
    g_qn = jnp.tile(g_qn_ref[...], (1, reps))[None]
    g_qr = jnp.tile(g_qr_ref[...], (1, reps))[None]
    nope = nope * lax.rsqrt(jnp.mean(nope * nope, axis=1, keepdims=True) + EPS) * (g_qn * Q_SCALE)
    rope = rope * lax.rsqrt(jnp.mean(rope * rope, axis=1, keepdims=True) + EPS) * (g_qr * Q_SCALE)
    x1, x2 = rope[:, :ROPE_HALF], rope[:, ROPE_HALF:]
    cos, sin = cos[None], sin[None]
    pad = jnp.zeros((N_HEADS, HEAD_PAD - QK_NOPE - QK_ROPE, tm), F32)
    qt_ref[...] = jnp.concatenate([nope, x1 * cos - x2 * sin, x1 * sin + x2 * cos, pad], axis=1).astype(BF16)

    ca = jnp.tile(ca_ref[...], (1, N_HEADS))
    sb = jnp.tile(sb_ref[...], (1, N_HEADS))
    k = lax.rsqrt(ms + EPS) * (ka * ga_ref[...] * ca + jnp.tile(krb, (1, N_HEADS)) * gb_ref[...] * sb)
    k_ref[...] = k.astype(BF16)


def _in_proj(x, tabs, w, tm):
    n = x.shape[0]
    ca, sb, cost, sint = tabs
    pos_tiles = ca.shape[0] // tm
    tok = lambda width: pl.BlockSpec((tm, width), lambda i: (i, 0))
    tok_t = lambda rows: pl.BlockSpec((rows, tm), lambda i: (0, i))
    pos = pl.BlockSpec((tm, HEAD_PAD), lambda i: (i % pos_tiles, 0))
    pos_t = pl.BlockSpec((ROPE_HALF, tm), lambda i: (0, i % pos_tiles))
    weights = [w['g_attn'], w['w_in'], w['g_q'], w['w_qt'], w['g_qn'], w['g_qr'],
               w['g_kv'], w['w_kvt'], w['g_kvt'], w['w_krt'], w['g_krt'], w['w_uk'], w['gm'], w['ga'], w['gb']]
    return pl.pallas_call(
        _in_proj_kernel,
        grid=(n // tm,),
        in_specs=[tok(D_MODEL), pos, pos, pos_t, pos_t]
                 + [_full(a.shape) for a in weights],
        out_specs=[pl.BlockSpec((N_HEADS, HEAD_PAD, tm), lambda i: (0, 0, i)),
                   tok(QK_WIDTH), tok(KV_RANK), tok_t(KV_RANK),
                   pl.BlockSpec((1, QK_ROPE, tm), lambda i: (i // pos_tiles, 0, i % pos_tiles)),
                   tok(CG_WIDTH), tok(CG_WIDTH)],
        out_shape=[jax.ShapeDtypeStruct((N_HEADS, HEAD_PAD, n), BF16),
                   jax.ShapeDtypeStruct((n, QK_WIDTH), BF16),
                   jax.ShapeDtypeStruct((n, KV_RANK), F32),
                   jax.ShapeDtypeStruct((KV_RANK, n), BF16),
                   jax.ShapeDtypeStruct((n // (pos_tiles * tm), QK_ROPE, pos_tiles * tm), F32),
                   jax.ShapeDtypeStruct((n, CG_WIDTH), F32),
                   jax.ShapeDtypeStruct((n, CG_WIDTH), F32)],
        compiler_params=_params(("parallel",)),
        name="in_proj",
    )(x, ca, sb, cost, sint, *weights)


def _prompt_attn_kernel(qt_ref, k_ref, vt_ref, o_ref, m_s, acc_s, s_s):
    qi = pl.program_id(1)
    kb = KEY_BLOCK
    per_q = ATT_BLOCK // kb
    m_s[...] = jnp.full(m_s.shape, -jnp.inf, F32)
    acc_s[...] = jnp.zeros(acc_s.shape, F32)
    ones = jnp.ones((ONES_ROWS, kb), BF16)

    def key_block(off, first_q, masked):
        cols = slice(first_q, ATT_BLOCK)
        vals = jnp.concatenate([vt_ref[:, pl.ds(off, kb)], ones], axis=0)
        if masked:
            key = lax.broadcasted_iota(jnp.int32, (kb, ATT_BLOCK - first_q), 0)
            qry = lax.broadcasted_iota(jnp.int32, (kb, ATT_BLOCK - first_q), 1)
            visible = key <= qry

        def score_stage(h):
            s = _dot(k_ref[pl.ds(off, kb), h * HEAD_PAD:(h + 1) * HEAD_PAD], qt_ref[h, :, cols])
            if masked:
                s = jnp.where(visible, s, -jnp.inf)
            m_old = m_s[h, :, cols]
            m_new = jnp.maximum(m_old, jnp.max(s, axis=0, keepdims=True))
            m_s[h, :, cols] = m_new
            s_s[h % 2, :, cols] = s
            return m_old, m_new

        def value_stage(h, m_old, m_new):
            p = jnp.exp2(s_s[h % 2, :, cols] - m_new)
            acc_s[h, :, cols] = jnp.exp2(m_old - m_new) * acc_s[h, :, cols] + _dot(vals, p.astype(BF16))

        pending = score_stage(0)
        for h in range(N_HEADS):
            nxt = score_stage(h + 1) if h + 1 < N_HEADS else None
            value_stage(h, *pending)
            pending = nxt

    def body(j, carry):
        key_block(pl.multiple_of(j * kb, kb), 0, False)
        return carry

    lax.fori_loop(0, qi * per_q, body, 0)
    for d in range(per_q):
        key_block(pl.multiple_of(qi * ATT_BLOCK + d * kb, kb), d * kb, True)
    for h in range(N_HEADS):
        acc = acc_s[h]
        o_t = acc[:KV_RANK] / acc[KV_RANK:KV_RANK + 1]
        o_ref[:, h * HEAD_PAD:(h + 1) * HEAD_PAD] = o_t.T.astype(BF16)


def _prompt_attn(qt, k, vt, batch, seq):
    nq = seq // ATT_BLOCK
    return pl.pallas_call(
        _prompt_attn_kernel,
        grid=(batch, nq),
        in_specs=[pl.BlockSpec((N_HEADS, HEAD_PAD, ATT_BLOCK), lambda b, i: (0, 0, b * nq + i)),
                  pl.BlockSpec((seq, QK_WIDTH), lambda b, i: (b, 0)),
                  pl.BlockSpec((KV_RANK, seq), lambda b, i: (0, b))],
        out_specs=pl.BlockSpec((ATT_BLOCK, N_HEADS * KV_RANK), lambda b, i: (b * nq + i, 0)),
        out_shape=jax.ShapeDtypeStruct((batch * seq, N_HEADS * KV_RANK), BF16),
        scratch_shapes=[pltpu.VMEM((N_HEADS, 1, ATT_BLOCK), F32),
                        pltpu.VMEM((N_HEADS, KV_RANK + ONES_ROWS, ATT_BLOCK), F32),
                        pltpu.VMEM((2, KEY_BLOCK, ATT_BLOCK), F32)],
        compiler_params=_params(("parallel", "parallel")),
        name="prompt_attn",
    )(qt, k, vt)


def _absorb_q_kernel(qn_ref, g_ref, wukt_ref, qa_ref):
    qg = qn_ref[...] * g_ref[...]
    lane = lax.broadcasted_iota(jnp.int32, qg.shape, 1)
    for h in range(N_HEADS):
        q_h = jnp.where((lane >= h * QK_NOPE) & (lane < (h + 1) * QK_NOPE), qg, 0.0)
        qa_ref[:, h * KV_RANK:(h + 1) * KV_RANK] = jnp.dot(
            q_h, wukt_ref[...], precision=lax.Precision.HIGHEST, preferred_element_type=F32)


def _absorb_q(qn, w):
    n_samp = qn.shape[0]
    args = [qn, w['g_kn_row'], w['w_ukt_f32']]
    return pl.pallas_call(
        _absorb_q_kernel,
        grid=(1,),
        in_specs=[_full(a.shape) for a in args],
        out_specs=_full((n_samp, N_HEADS * KV_RANK)),
        out_shape=jax.ShapeDtypeStruct((n_samp, N_HEADS * KV_RANK), F32),
        compiler_params=_params(("arbitrary",)),
        name="absorb_q",
    )(*args)


def _sample_attn_kernel(pt_ref, per_ref, wukb_ref,
                        cache_c, cache_r, o_ref,
                        wext, cbuf, rbuf, sems, *, n_pages):
    b = pl.program_id(0)
    last = pl.num_programs(0) - 1
    slot = lax.rem(b, 2)
    past = n_pages * PAGE_SIZE
    n_tiles = past // KV_TILE
    pages_per_tile = KV_TILE // PAGE_SIZE
    nw = N_HEADS * QK_NOPE
    nxt = jnp.minimum(b + 1, last)

    def start_page(bb, sl, i, prio=0):
        page = pt_ref[bb * n_pages + i]
        off = pl.multiple_of(i * PAGE_SIZE, PAGE_SIZE)
        pltpu.make_async_copy(cache_c.at[page], cbuf.at[sl, pl.ds(off, PAGE_SIZE)],
                              sems.at[0, sl]).start(priority=prio)
        pltpu.make_async_copy(cache_r.at[page], rbuf.at[sl, i], sems.at[1, sl]).start(priority=1 - prio)

    def wait_pages(sl):
        pltpu.make_async_copy(cbuf.at[sl], cbuf.at[sl], sems.at[0, sl]).wait()
        pltpu.make_async_copy(rbuf.at[sl], rbuf.at[sl], sems.at[1, sl]).wait()

    @pl.when(b == 0)
    def _():
        def issue(i, carry):
            start_page(0, 0, i)
            return carry
        lax.fori_loop(0, n_pages, issue, 0)
        wext[:nw, :] = wukb_ref[...]

    q = per_ref[0, 0 * N_HEADS:1 * N_HEADS]
    qa = per_ref[0, 1 * N_HEADS:2 * N_HEADS]
    k_own = per_ref[0, 2 * N_HEADS:3 * N_HEADS]
    c_own = per_ref[0, 3 * N_HEADS:3 * N_HEADS + 1]
    wext[nw:, :] = jnp.concatenate(
        [qa, jnp.zeros((wext.shape[0] - nw - N_HEADS, KV_RANK), F32)], axis=0).astype(BF16)
    qr = q[:, QK_NOPE:QK_NOPE + QK_ROPE].astype(BF16)

    def scores(c_blk, krt_blk):
        tk = c_blk.shape[0]
        cb = c_blk.astype(BF16)
        knt = _dot_nt(wext[...], cb)
        kn3 = knt[:nw].reshape(N_HEADS, QK_NOPE, tk)
        r = lax.rsqrt(jnp.sum(kn3 * kn3, axis=1) * (1.0 / QK_NOPE) + EPS)
        s_rope = _dot(qr, krt_blk.astype(BF16))
        return cb, knt[nw:nw + N_HEADS] * r + s_rope

    def value_part(m_j, p_j, cb_j):
        return m_j, jnp.sum(p_j, axis=-1, keepdims=True), _dot(p_j.astype(BF16), cb_j)

    wait_pages(slot)
    parts, pending = [], None
    per_group = n_pages // ISSUE_TILES

    def issue_group(g):
        for i in range(per_group):
            start_page(nxt, 1 - slot, g * per_group + i, prio=i % 2)

    for j in range(n_tiles):
        if 0 < j < ISSUE_TILES:
            issue_group(j)
        keys = slice(j * KV_TILE, (j + 1) * KV_TILE)
        krt = jnp.concatenate([rbuf[slot, j * pages_per_tile + i] for i in range(pages_per_tile)], axis=1)
        cb, s = scores(cbuf[slot, keys, :], krt)
        if j == 0:
            issue_group(0)
        m_j = jnp.max(s, axis=-1, keepdims=True)
        if pending is not None:
            parts.append(value_part(*pending))
        pending = (m_j, jnp.exp2(s - m_j), cb)
    parts.append(value_part(*pending))

    s_own = jnp.sum(q * k_own, axis=-1, keepdims=True)
    m = s_own
    for m_j, _, _ in parts:
        m = jnp.maximum(m, m_j)
    p_own = jnp.exp2(s_own - m)
    round_bf16 = lambda a: a.astype(BF16).astype(F32)
    l = p_own
    acc = round_bf16(p_own) * round_bf16(c_own)
    for m_j, l_j, acc_j in parts:
        w_j = jnp.exp2(m_j - m)
        l = l + w_j * l_j
        acc = acc + w_j * acc_j
    o_ref[0] = acc / l

    @pl.when(b == last)
    def _():
        wait_pages(1 - slot)


def _sample_attn(page_table, q, qa, k_own, c_own, w, cache_c, cache_rt):
    n_samp, n_pages = page_table.shape
    past = n_pages * PAGE_SIZE
    assert past % KV_TILE == 0 and n_pages % ISSUE_TILES == 0 and past // KV_TILE >= ISSUE_TILES
    wext_rows = N_HEADS * QK_NOPE + 2 * SUBLANES
    per = jnp.concatenate([q, qa, k_own.astype(F32), jnp.broadcast_to(c_own, (n_samp, N_HEADS, KV_RANK))],
                          axis=1)
    grid_spec = pltpu.PrefetchScalarGridSpec(
        num_scalar_prefetch=1,
        grid=(n_samp,),
        in_specs=[pl.BlockSpec((1,) + per.shape[1:], lambda b, pt: (b, 0, 0)),
                  pl.BlockSpec(w['w_ukt'].shape, lambda b, pt: (0, 0)),
                  pl.BlockSpec(memory_space=pl.ANY),
                  pl.BlockSpec(memory_space=pl.ANY)],
        out_specs=pl.BlockSpec((1, N_HEADS, KV_RANK), lambda b, pt: (b, 0, 0)),
        scratch_shapes=[pltpu.VMEM((wext_rows, KV_RANK), BF16),
                        pltpu.VMEM((2, past, KV_RANK), F32),
                        pltpu.VMEM((2, n_pages, QK_ROPE, PAGE_SIZE), F32),
                        pltpu.SemaphoreType.DMA((2, 2))])
    return pl.pallas_call(
        functools.partial(_sample_attn_kernel, n_pages=n_pages),
        grid_spec=grid_spec,
        out_shape=jax.ShapeDtypeStruct((n_samp, N_HEADS, KV_RANK), F32),
        compiler_params=_params(("arbitrary",)),
        name="sample_attn",
    )(page_table.reshape(-1), per, w['w_ukt'], cache_c, cache_rt)


def _mix_proj_kernel(x_ref, o_ref, u_ref, v_ref, wuv_ref, ws_ref, bias_ref, wo_ref, g_ref,
                     h_ref, hn_ref, *, single_token):
    tm = x_ref.shape[0]
    attn = _dot(o_ref[...], wuv_ref[...])
    v = v_ref[...]
    if single_token:
        mixed = v * ws_ref[...] + bias_ref[...]
    else:
        vb = v.astype(BF16)
        row = lax.broadcasted_iota(jnp.int32, (CHUNK, CHUNK), 0)
        col = lax.broadcasted_iota(jnp.int32, (CHUNK, CHUNK), 1)
        w_tril = [jnp.where(col <= row, ws_ref[g], 0.0).astype(BF16) for g in range(N_CG)]
        w_pair = [jnp.concatenate([w_tril[2 * j], w_tril[2 * j + 1]], axis=1) for j in range(N_CG // 2)]
        low_half = lax.broadcasted_iota(jnp.int32, (CHUNK, LANES), 1) < CG_DIM
        zero = jnp.zeros((CHUNK, LANES), BF16)
        chunks = []
        for c in range(tm // CHUNK):
            cols = []
            for j in range(CG_WIDTH // LANES):
                vp = vb[c * CHUNK:(c + 1) * CHUNK, j * LANES:(j + 1) * LANES]
                stacked = jnp.concatenate([jnp.where(low_half, vp, zero), jnp.where(low_half, zero, vp)], axis=0)
                cols.append(_dot(w_pair[j], stacked))
            chunks.append(jnp.concatenate(cols, axis=1) + bias_ref[...])
        mixed = jnp.concatenate(chunks, axis=0)
    sg = u_ref[...] * mixed
    width = attn.shape[1]
    h = x_ref[...] + _dot(attn.astype(BF16), wo_ref[:width, :]) + _dot(sg.astype(BF16), wo_ref[width:, :])
    h_ref[...] = h
    hn_ref[...] = (_rms(h) * g_ref[...]).astype(BF16)


def _conv_ffn_kernel(*refs, single_token):
    if single_token:
        h_ref, hn_ref, win_ref, cw_ref, cb_ref, wout_ref, h0_ref, h1_ref, out_ref, a_ref, act_s = refs
    else:
        h_ref, hn_ref, win_ref, cw_ref, cb_ref, wout_ref, out_ref, last_ref, act_s, carry = refs
    tm = hn_ref.shape[0]
    hn = hn_ref[...]

    def conv(cols):
        a = _dot(hn, win_ref[:, cols])
        if single_token:
            a2, a1 = h0_ref[:, cols], h1_ref[:, cols]
            a_ref[0, :, cols] = a1
            a_ref[1, :, cols] = a
        else:
            ext = jnp.concatenate([carry[:, cols], a], axis=0)
            a1 = ext[SUBLANES - 1:SUBLANES - 1 + tm]
            a2 = ext[SUBLANES - 2:SUBLANES - 2 + tm]
            tail = a[tm - SUBLANES:]
            carry[:, cols] = tail
            last_ref[0, :, cols] = tail[SUBLANES - (CONV_W - 1):]
        return cb_ref[:, cols] + cw_ref[0:1, cols] * a2 + cw_ref[1:2, cols] * a1 + cw_ref[2:3, cols] * a

    for j in range(D_FF // FF_TILE):
        gate = slice(j * FF_TILE, (j + 1) * FF_TILE)
        up = slice(D_FF + j * FF_TILE, D_FF + (j + 1) * FF_TILE)
        act_s[:, gate] = (jax.nn.silu(conv(gate)) * conv(up)).astype(BF16)
    out_ref[...] = h_ref[...] + _dot(act_s[...], wout_ref[...])


def _ple_kernel(h_ref, p_ref, g_ref, wgate_ref, wproj_ref, gpost_ref, y_ref):
    h = h_ref[...]
    gate = jax.nn.sigmoid(_dot((_rms(h) * g_ref[...]).astype(BF16), wgate_ref[...]))
    e = _rms(_dot(p_ref[...].astype(BF16), wproj_ref[...])) * gpost_ref[...]
    y_ref[...] = h + gate * e


def _post_attn_kernel(*refs, single_token):
    x_ref, o_ref, u_ref, v_ref, p_ref = refs[:5]
    mix_w, ffn_w, ple_w = refs[5:10], refs[10:14], refs[14:18]
    if single_token:
        h0_ref, h1_ref, y_ref, a_ref, act_s, h_s, hn_s, h2_s = refs[18:]
        _mix_proj_kernel(x_ref, o_ref, u_ref, v_ref, *mix_w, h_s, hn_s, single_token=True)
        _conv_ffn_kernel(h_s, hn_s, *ffn_w, h0_ref, h1_ref, h2_s, a_ref, act_s, single_token=True)
        _ple_kernel(h2_s, p_ref, *ple_w, y_ref)
        return
    y_ref, last_ref, act_s, carry, h_s, hn_s, h2_s = refs[18:]

    @pl.when(pl.program_id(1) == 0)
    def _():
        carry[...] = jnp.zeros(carry.shape, F32)

    _mix_proj_kernel(x_ref, o_ref, u_ref, v_ref, *mix_w, h_s, hn_s, single_token=False)
    _conv_ffn_kernel(h_s, hn_s, *ffn_w, h2_s, last_ref, act_s, carry, single_token=False)
    _ple_kernel(h2_s, p_ref, *ple_w, y_ref)


def _post_attn(x, o_lat, u, v, p, w, batch, seq, tm, hist=None):
    single_token = hist is not None
    nt = seq // tm
    tok = lambda width: pl.BlockSpec((tm, width), lambda b, t: (b * nt + t, 0))
    resident = lambda a: pl.BlockSpec(a.shape, lambda b, t: (0,) * a.ndim, pipeline_mode=pl.Buffered(1))
    ws, bias = (w['ws_diag'], w['bias_first']) if single_token else (w['w_s'], w['bias_tab'])
    weights = [w['w_uv_bd'], ws, bias, w['w_o'], w['g_ffn'],
               w['w_ff_in'], w['conv_w'], w['conv_b'], w['w_ff_out'],
               w['g_ple'], w['w_gate'], w['w_proj'], w['g_post']]
    in_specs = ([tok(D_MODEL), tok(N_HEADS * KV_RANK), tok(CG_WIDTH), tok(CG_WIDTH), tok(PLE_DIM)]
                + [resident(a) for a in weights])
    args = [x, o_lat, u, v, p] + weights
    scratch = [pltpu.VMEM((tm, D_FF), BF16)]
    if single_token:
        in_specs += [resident(hist[0]), resident(hist[1])]
        args += list(hist)
        last_spec = pl.BlockSpec((CONV_W - 1, tm, 2 * D_FF), lambda b, t: (0, 0, 0))
        last_shape = jax.ShapeDtypeStruct((CONV_W - 1, tm, 2 * D_FF), F32)
    else:
        last_spec = pl.BlockSpec((1, CONV_W - 1, 2 * D_FF), lambda b, t: (b * nt + t, 0, 0))
        last_shape = jax.ShapeDtypeStruct((batch * nt, CONV_W - 1, 2 * D_FF), F32)
        scratch.append(pltpu.VMEM((SUBLANES, 2 * D_FF), F32))
    scratch += [pltpu.VMEM((tm, D_MODEL), F32), pltpu.VMEM((tm, D_MODEL), BF16),
                pltpu.VMEM((tm, D_MODEL), F32)]
    return pl.pallas_call(
        functools.partial(_post_attn_kernel, single_token=single_token),
        grid=(batch, nt),
        in_specs=in_specs,
        out_specs=[tok(D_MODEL), last_spec],
        out_shape=[jax.ShapeDtypeStruct((batch * seq, D_MODEL), F32), last_shape],
        scratch_shapes=scratch,
        compiler_params=_params(("arbitrary", "arbitrary")),
        name="post_attn_single" if single_token else "post_attn",
    )(*args)


def _rope_tables(pos):
    inv = ROPE_THETA ** (-jnp.arange(ROPE_HALF, dtype=F32) * (2.0 / QK_ROPE))
    ang = pos.astype(F32)[:, None] * inv[None, :]
    cos, sin = jnp.cos(ang), jnp.sin(ang)
    n = pos.shape[0]
    ones = jnp.ones((n, QK_NOPE), F32)
    zq = jnp.zeros((n, QK_NOPE), F32)
    zp = jnp.zeros((n, HEAD_PAD - QK_NOPE - QK_ROPE), F32)
    ca = jnp.concatenate([ones, cos, cos, zp], axis=1)
    sb = jnp.concatenate([zq, -sin, sin, zp], axis=1)
    return ca, sb, cos.T, sin.T


def _prep_weights(i, attn_norm_w, w_in, q_norm_w, w_uq, q_nope_norm_w, q_rope_norm_w, kv_norm_w,
                  k_rope_norm_w, w_uk, k_nope_norm_w, w_uv, w_s, b_s, w_o, ffn_norm_w, w_ff_in,
                  conv_w, conv_b, w_ff_out, ple_norm_w, w_ple_gate, w_ple_proj, ple_post_norm_w):
    row = lambda g: g[i][None, :]
    swap = lambda a: jnp.concatenate([a[..., ROPE_HALF:], a[..., :ROPE_HALF]], axis=-1)
    win = w_in[i]
    wq = w_uq[i].reshape(Q_RANK, N_HEADS, QK_NOPE + QK_ROPE)
    nope, rope = wq[..., :QK_NOPE], wq[..., QK_NOPE:]
    pad = HEAD_PAD - QK_NOPE - QK_ROPE
    w_qt = jnp.concatenate([nope, rope, jnp.zeros((Q_RANK, N_HEADS, pad), F32)],
                           axis=-1).reshape(Q_RANK, QK_WIDTH).T
    col = lambda g: jnp.broadcast_to(g[:, None], (g.shape[0], LANES))
    kr_cols = win[:, OFF_KR:OFF_U]
    w_in_e = jnp.concatenate([win[:, :OFF_KR], win[:, OFF_U:], jnp.zeros((D_MODEL, QK_NOPE), F32),
                              kr_cols, swap(kr_cols)], axis=1)
    w_uk_e = jnp.concatenate([w_uk[i], jnp.zeros((KV_RANK, N_HEADS, HEAD_PAD - QK_NOPE), F32)],
                             axis=-1).reshape(KV_RANK, QK_WIDTH)
    gn, gr = k_nope_norm_w[i], k_rope_norm_w[i]
    ga = jnp.tile(jnp.concatenate([gn, gr, jnp.zeros((pad,), F32)]), N_HEADS)[None, :]
    gb = jnp.tile(jnp.concatenate([jnp.zeros((QK_NOPE,), F32), swap(gr), jnp.zeros((pad,), F32)]),
                  N_HEADS)[None, :]
    lane = jnp.arange(2 * HEAD_PAD)
    slot, off = lane // HEAD_PAD, lane % HEAD_PAD
    grp = jnp.where(off < QK_NOPE, 0, jnp.where(off < QK_NOPE + QK_ROPE, 1, 2))
    same = (slot[:, None] == slot[None, :]) & (grp[:, None] == grp[None, :]) & (grp[:, None] < 2)
    gm = jnp.where(same, jnp.where(grp[:, None] == 0, 1.0 / QK_NOPE, 1.0 / QK_ROPE), 0.0)
    w_ukt = w_uk[i].reshape(KV_RANK, N_HEADS * QK_NOPE).T
    eye = jnp.eye(N_HEADS, dtype=F32)
    w_uv_bd = jnp.einsum('rhd,hg->hrgd', w_uv[i], eye).reshape(N_HEADS * KV_RANK, N_HEADS * V_DIM)
    return {
        'g_attn': row(attn_norm_w),
        'w_in': w_in_e.astype(BF16),
        'g_q': row(q_norm_w), 'w_qt': w_qt.astype(BF16),
        'g_qn': col(q_nope_norm_w[i]), 'g_qr': col(q_rope_norm_w[i]),
        'g_kv': row(kv_norm_w), 'w_kvt': win[:, OFF_KV:OFF_KR].T.astype(BF16), 'g_kvt': col(kv_norm_w[i]),
        'w_krt': kr_cols.T.astype(BF16), 'g_krt': col(k_rope_norm_w[i]),
        'w_uk': w_uk_e.astype(BF16), 'gm': gm.astype(BF16), 'ga': ga, 'gb': gb,
        'w_ukt': w_ukt.astype(BF16), 'w_ukt_f32': w_ukt,
        'g_kn_row': jnp.tile(k_nope_norm_w[i], N_HEADS)[None, :],
        'w_uv_bd': w_uv_bd.astype(BF16),
        'w_s': w_s[i],
        'bias_tab': jnp.repeat(b_s[i].T, CG_DIM, axis=1),
        'ws_diag': jnp.repeat(w_s[i][:, 0, 0], CG_DIM)[None, :],
        'bias_first': jnp.repeat(b_s[i][:, 0], CG_DIM)[None, :],
        'w_o': w_o[i].astype(BF16), 'g_ffn': row(ffn_norm_w),
        'w_ff_in': w_ff_in[i].astype(BF16), 'conv_w': conv_w[i], 'conv_b': row(conv_b),
        'w_ff_out': w_ff_out[i].astype(BF16),
        'g_ple': row(ple_norm_w), 'w_gate': w_ple_gate[i].astype(BF16),
        'w_proj': w_ple_proj[i].astype(BF16), 'g_post': row(ple_post_norm_w),
    }


def kernel(x_prompt, x_sample, cache_ckv, cache_krope, state_conv, page_table, p_prompt, p_sample,
           attn_norm_w, w_in, q_norm_w, w_uq, q_nope_norm_w, q_rope_norm_w, kv_norm_w, k_rope_norm_w,
           w_uk, k_nope_norm_w, w_uv, w_s, b_s, w_o, ffn_norm_w, w_ff_in, conv_w, conv_b, w_ff_out,
           ple_norm_w, w_ple_gate, w_ple_proj, ple_post_norm_w):
    batch, seq, _ = x_prompt.shape
    n_samp, seq_s, _ = x_sample.shape
    depth = w_in.shape[0]
    assert depth == 1 and seq_s == 1 and seq % TOK_TILE == 0 and seq % CHUNK == 0
    past_len = page_table.shape[1] * PAGE_SIZE
    assert past_len % KV_TILE == 0
    w = _prep_weights(0, attn_norm_w, w_in, q_norm_w, w_uq, q_nope_norm_w, q_rope_norm_w, kv_norm_w,
                      k_rope_norm_w, w_uk, k_nope_norm_w, w_uv, w_s, b_s, w_o, ffn_norm_w, w_ff_in,
                      conv_w, conv_b, w_ff_out, ple_norm_w, w_ple_gate, w_ple_proj, ple_post_norm_w)

    n_tok = batch * seq
    tabs = _rope_tables(jnp.arange(seq, dtype=jnp.int32))
    xp = x_prompt.reshape(n_tok, D_MODEL)
    qt, k, ckv, vt, krt, u, v = _in_proj(xp, tabs, w, 2 * TOK_TILE)
    o_lat = _prompt_attn(qt, k, vt, batch, seq)
    y_prompt, a_last = _post_attn(xp, o_lat, u, v, p_prompt[0].reshape(n_tok, PLE_DIM), w,
                                  batch, seq, TOK_TILE)
    y_prompt = y_prompt.reshape(batch, seq, D_MODEL)
    n_keep = (seq - 1) % CHUNK + 1
    new_ckv_p = ckv.reshape(1, batch, seq, KV_RANK)
    new_kr_p = jnp.swapaxes(krt, 1, 2)[None]
    new_v_p = v.reshape(batch, seq, CG_WIDTH)[:, seq - n_keep:][None]
    new_conv_p = a_last.reshape(batch, seq // TOK_TILE, CONV_W - 1, 2 * D_FF)[:, -1][None]

    tabs_s = _rope_tables(jnp.full((n_samp,), past_len, dtype=jnp.int32))
    xs = x_sample.reshape(n_samp, D_MODEL)
    qt_s, k_s, ckv_s, _, krt_s, u_s, v_s = _in_proj(xs, tabs_s, w, n_samp)
    q3 = jnp.transpose(qt_s, (2, 0, 1)).astype(F32)
    kr_s = krt_s[0].T
    qa = _absorb_q(q3[:, :, :QK_NOPE].reshape(n_samp, N_HEADS * QK_NOPE), w)
    cache_rt = jnp.swapaxes(cache_krope[0], 1, 2)
    o_s = _sample_attn(page_table, q3, qa.reshape(n_samp, N_HEADS, KV_RANK),
                       k_s.reshape(n_samp, N_HEADS, HEAD_PAD), ckv_s.reshape(n_samp, 1, KV_RANK),
                       w, cache_ckv[0], cache_rt)
    o_s = o_s.reshape(n_samp, N_HEADS * KV_RANK).astype(BF16)
    hist = (state_conv[0, :, 0, :], state_conv[0, :, 1, :])
    y_sample, a_s = _post_attn(xs, o_s, u_s, v_s, p_sample[0].reshape(n_samp, PLE_DIM), w,
                               1, n_samp, n_samp, hist=hist)
    y_sample = y_sample.reshape(n_samp, 1, D_MODEL)
    new_conv_s = jnp.swapaxes(a_s, 0, 1)[None]

    return (y_prompt, y_sample, new_ckv_p, new_kr_p,
            ckv_s.reshape(1, n_samp, 1, KV_RANK), kr_s.reshape(1, n_samp, 1, QK_ROPE),
            new_v_p, v_s.reshape(1, n_samp, 1, CG_WIDTH), new_conv_p, new_conv_s)
```

```python
import functools

import jax
import jax.numpy as jnp
from jax import lax
from jax.experimental import pallas as pl
from jax.experimental.pallas import tpu as pltpu

F32 = jnp.float32
BF16 = jnp.bfloat16

D_MODEL = 1024
N_HEADS = 8
QK_NOPE = 64
QK_ROPE = 32
ROPE_HALF = QK_ROPE // 2
V_DIM = 64
Q_RANK = 256
KV_RANK = 128
CHUNK = 128
N_CG = 8
CG_WIDTH = 512
CG_DIM = 64
D_FF = 2816
CONV_W = 3
PLE_DIM = 256
PAGE_SIZE = 128
ROPE_THETA = 10000.0
EPS = 1e-6
SCALE = (QK_NOPE + QK_ROPE) ** -0.5
Q_SCALE = SCALE * 1.4426950408889634
OFF_KV = Q_RANK
OFF_KR = OFF_KV + KV_RANK
OFF_U = OFF_KR + QK_ROPE
OFF_V = OFF_U + CG_WIDTH
IN_WIDTH = OFF_V + CG_WIDTH
HEAD_PAD = 128
QK_WIDTH = N_HEADS * HEAD_PAD
Z_WIDTH = Q_RANK + KV_RANK + 2 * CG_WIDTH

LANES = 128
SUBLANES = 8
VMEM_LIMIT_BYTES = 56 * 1024 * 1024

TOK_TILE = 512
ATT_BLOCK = 1024
KEY_BLOCK = 512
ONES_ROWS = 16
FF_TILE = 256
KV_TILE = 2048
ISSUE_TILES = 4

NT_DIMS = (((1,), (1,)), ((), ()))


def _params(semantics):
    return pltpu.CompilerParams(dimension_semantics=semantics,
                                vmem_limit_bytes=VMEM_LIMIT_BYTES)


def _rms(x):
    return x * lax.rsqrt(jnp.mean(x * x, axis=-1, keepdims=True) + EPS)


def _dot(a, b):
    return jnp.dot(a, b, preferred_element_type=F32)


def _dot_nt(a, b):
    return lax.dot_general(a, b, NT_DIMS, preferred_element_type=F32)


def _full(shape):
    zeros = (0,) * len(shape)
    return pl.BlockSpec(shape, lambda *_: zeros)


def _in_proj_kernel(x_ref, ca_ref, sb_ref, cost_ref, sint_ref,
                    g_attn_ref, w_in_ref, g_q_ref, w_qt_ref, g_qn_ref, g_qr_ref,
                    g_kv_ref, w_kvt_ref, g_kvt_ref, w_krt_ref, g_krt_ref, w_uk_ref, gm_ref, ga_ref, gb_ref,
                    qt_ref, k_ref, ckv_ref, vt_ref, krt_ref, u_ref, v_ref):
    tm = x_ref.shape[0]
    reps = tm // LANES
    cos, sin = cost_ref[...], sint_ref[...]
    hb = (_rms(x_ref[...]) * g_attn_ref[...]).astype(BF16)
    z = _dot(hb, w_in_ref[...])
    ckv = _rms(z[:, OFF_KV:OFF_KR]) * g_kv_ref[...]
    ckv_ref[...] = ckv
    kr_slot = z[:, Z_WIDTH:]
    ka = _dot(ckv.astype(BF16), w_uk_ref[...]) + jnp.tile(kr_slot, (1, N_HEADS))
    ct = _dot_nt(w_kvt_ref[...], hb)
    krt = _dot_nt(w_krt_ref[...], hb)
    qln = (_rms(z[:, :Q_RANK]) * g_q_ref[...]).astype(BF16)
    qt = _dot_nt(w_qt_ref[...], qln).reshape(N_HEADS, HEAD_PAD, tm)
    sq = (ka * ka).astype(BF16)
    gm = gm_ref[...]
    pair = 2 * HEAD_PAD
    ms = jnp.concatenate([_dot(sq[:, i * pair:(i + 1) * pair], gm) for i in range(N_HEADS // 2)], axis=1)
    krb = pltpu.roll(kr_slot, HEAD_PAD - QK_ROPE, axis=1)

    u_ref[...] = jax.nn.gelu(z[:, OFF_KR:OFF_KR + CG_WIDTH])
    v_ref[...] = jax.nn.gelu(z[:, OFF_KR + CG_WIDTH:OFF_KR + 2 * CG_WIDTH])
    ct = ct * lax.rsqrt(jnp.mean(ct * ct, axis=0, keepdims=True) + EPS)
    vt_ref[...] = (ct * jnp.tile(g_kvt_ref[...], (1, reps))).astype(BF16)
    krt = krt * lax.rsqrt(jnp.mean(krt * krt, axis=0, keepdims=True) + EPS) * jnp.tile(g_krt_ref[...], (1, reps))
    kx1, kx2 = krt[:ROPE_HALF], krt[ROPE_HALF:]
    krt_ref[0] = jnp.concatenate([kx1 * cos - kx2 * sin, kx1 * sin + kx2 * cos], axis=0)

    nope = qt[:, :QK_NOPE]
    rope = qt[:, QK_NOPE:QK_NOPE + QK_ROPE]
    g_qn = jnp.tile(g_qn_ref[...], (1, reps))[None]
    g_qr = jnp.tile(g_qr_ref[...], (1, reps))[None]
    nope = nope * lax.rsqrt(jnp.mean(nope * nope, axis=1, keepdims=True) + EPS) * (g_qn * Q_SCALE)
    rope = rope * lax.rsqrt(jnp.mean(rope * rope, axis=1, keepdims=True) + EPS) * (g_qr * Q_SCALE)
    x1, x2 = rope[:, :ROPE_HALF], rope[:, ROPE_HALF:]
    cos, sin = cos[None], sin[None]
    pad = jnp.zeros((N_HEADS, HEAD_PAD - QK_NOPE - QK_ROPE, tm), F32)
    qt_ref[...] = jnp.concatenate([nope, x1 * cos - x2 * sin, x1 * sin + x2 * cos, pad], axis=1).astype(BF16)

    ca = jnp.tile(ca_ref[...], (1, N_HEADS))
    sb = jnp.tile(sb_ref[...], (1, N_HEADS))
    k = lax.rsqrt(ms + EPS) * (ka * ga_ref[...] * ca + jnp.tile(krb, (1, N_HEADS)) * gb_ref[...] * sb)
    k_ref[...] = k.astype(BF16)


def _in_proj(x, tabs, w, tm):
    n = x.shape[0]
    ca, sb, cost, sint = tabs
    pos_tiles = ca.shape[0] // tm
    tok = lambda width: pl.BlockSpec((tm, width), lambda i: (i, 0))
    tok_t = lambda rows: pl.BlockSpec((rows, tm), lambda i: (0, i))
    pos = pl.BlockSpec((tm, HEAD_PAD), lambda i: (i % pos_tiles, 0))
    pos_t = pl.BlockSpec((ROPE_HALF, tm), lambda i: (0, i % pos_tiles))
    weights = [w['g_attn'], w['w_in'], w['g_q'], w['w_qt'], w['g_qn'], w['g_qr'],
               w['g_kv'], w['w_kvt'], w['g_kvt'], w['w_krt'], w['g_krt'], w['w_uk'], w['gm'], w['ga'], w['gb']]
    return pl.pallas_call(
        _in_proj_kernel,
        grid=(n // tm,),
        in_specs=[tok(D_MODEL), pos, pos, pos_t, pos_t]
                 + [_full(a.shape) for a in weights],
        out_specs=[pl.BlockSpec((N_HEADS, HEAD_PAD, tm), lambda i: (0, 0, i)),
                   tok(QK_WIDTH), tok(KV_RANK), tok_t(KV_RANK),
                   pl.BlockSpec((1, QK_ROPE, tm), lambda i: (i // pos_tiles, 0, i % pos_tiles)),
                   tok(CG_WIDTH), tok(CG_WIDTH)],
        out_shape=[jax.ShapeDtypeStruct((N_HEADS, HEAD_PAD, n), BF16),
                   jax.ShapeDtypeStruct((n, QK_WIDTH), BF16),
                   jax.ShapeDtypeStruct((n, KV_RANK), F32),
                   jax.ShapeDtypeStruct((KV_RANK, n), BF16),
                   jax.ShapeDtypeStruct((n // (pos_tiles * tm), QK_ROPE, pos_tiles * tm), F32),
                   jax.ShapeDtypeStruct((n, CG_WIDTH), F32),
                   jax.ShapeDtypeStruct((n, CG_WIDTH), F32)],
        compiler_params=_params(("parallel",)),
        name="in_proj",
    )(x, ca, sb, cost, sint, *weights)


def _prompt_attn_kernel(qt_ref, k_ref, vt_ref, o_ref, m_s, acc_s, s_s):
    qi = pl.program_id(1)
    kb = KEY_BLOCK
    per_q = ATT_BLOCK // kb
    m_s[...] = jnp.full(m_s.shape, -jnp.inf, F32)
    acc_s[...] = jnp.zeros(acc_s.shape, F32)
    ones = jnp.ones((ONES_ROWS, kb), BF16)

    def key_block(off, first_q, masked):
        cols = slice(first_q, ATT_BLOCK)
        vals = jnp.concatenate([vt_ref[:, pl.ds(off, kb)], ones], axis=0)
        if masked:
            key = lax.broadcasted_iota(jnp.int32, (kb, ATT_BLOCK - first_q), 0)
            qry = lax.broadcasted_iota(jnp.int32, (kb, ATT_BLOCK - first_q), 1)
            visible = key <= qry

        def score_stage(h):
            s = _dot(k_ref[pl.ds(off, kb), h * HEAD_PAD:(h + 1) * HEAD_PAD], qt_ref[h, :, cols])
            if masked:
                s = jnp.where(visible, s, -jnp.inf)
            m_old = m_s[h, :, cols]
            m_new = jnp.maximum(m_old, jnp.max(s, axis=0, keepdims=True))
            m_s[h, :, cols] = m_new
            s_s[h % 2, :, cols] = s
            return m_old, m_new

        def value_stage(h, m_old, m_new):
            p = jnp.exp2(s_s[h % 2, :, cols] - m_new)
            acc_s[h, :, cols] = jnp.exp2(m_old - m_new) * acc_s[h, :, cols] + _dot(vals, p.astype(BF16))

        pending = score_stage(0)
        for h in range(N_HEADS):
            nxt = score_stage(h + 1) if h + 1 < N_HEADS else None
            value_stage(h, *pending)
            pending = nxt

    def body(j, carry):
        key_block(pl.multiple_of(j * kb, kb), 0, False)
        return carry

    lax.fori_loop(0, qi * per_q, body, 0)
    for d in range(per_q):
        key_block(pl.multiple_of(qi * ATT_BLOCK + d * kb, kb), d * kb, True)
    for h in range(N_HEADS):
        acc = acc_s[h]
        o_t = acc[:KV_RANK] / acc[KV_RANK:KV_RANK + 1]
        o_ref[:, h * HEAD_PAD:(h + 1) * HEAD_PAD] = o_t.T.astype(BF16)


def _prompt_attn(qt, k, vt, batch, seq):
    nq = seq // ATT_BLOCK
    return pl.pallas_call(
        _prompt_attn_kernel,
        grid=(batch, nq),
        in_specs=[pl.BlockSpec((N_HEADS, HEAD_PAD, ATT_BLOCK), lambda b, i: (0, 0, b * nq + i)),
                  pl.BlockSpec((seq, QK_WIDTH), lambda b, i: (b, 0)),
                  pl.BlockSpec((KV_RANK, seq), lambda b, i: (0, b))],
        out_specs=pl.BlockSpec((ATT_BLOCK, N_HEADS * KV_RANK), lambda b, i: (b * nq + i, 0)),
        out_shape=jax.ShapeDtypeStruct((batch * seq, N_HEADS * KV_RANK), BF16),
        scratch_shapes=[pltpu.VMEM((N_HEADS, 1, ATT_BLOCK), F32),
                        pltpu.VMEM((N_HEADS, KV_RANK + ONES_ROWS, ATT_BLOCK), F32),
                        pltpu.VMEM((2, KEY_BLOCK, ATT_BLOCK), F32)],
        compiler_params=_params(("parallel", "parallel")),
        name="prompt_attn",
    )(qt, k, vt)


def _absorb_q_kernel(qn_ref, g_ref, wukt_ref, qa_ref):
    qg = qn_ref[...] * g_ref[...]
    lane = lax.broadcasted_iota(jnp.int32, qg.shape, 1)
    for h in range(N_HEADS):
        q_h = jnp.where((lane >= h * QK_NOPE) & (lane < (h + 1) * QK_NOPE), qg, 0.0)
        qa_ref[:, h * KV_RANK:(h + 1) * KV_RANK] = jnp.dot(
            q_h, wukt_ref[...], precision=lax.Precision.HIGHEST, preferred_element_type=F32)


def _absorb_q(qn, w):
    n_samp = qn.shape[0]
    args = [qn, w['g_kn_row'], w['w_ukt_f32']]
    return pl.pallas_call(
        _absorb_q_kernel,
        grid=(1,),
        in_specs=[_full(a.shape) for a in args],
        out_specs=_full((n_samp, N_HEADS * KV_RANK)),
        out_shape=jax.ShapeDtypeStruct((n_samp, N_HEADS * KV_RANK), F32),
        compiler_params=_params(("arbitrary",)),
        name="absorb_q",
    )(*args)


def _sample_attn_kernel(pt_ref, per_ref, wukb_ref,
                        cache_c, cache_r, o_ref,
                        wext, cbuf, rbuf, sems, *, n_pages):
    b = pl.program_id(0)
    last = pl.num_programs(0) - 1
    slot = lax.rem(b, 2)
    past = n_pages * PAGE_SIZE
    n_tiles = past // KV_TILE
    pages_per_tile = KV_TILE // PAGE_SIZE
    nw = N_HEADS * QK_NOPE
    nxt = jnp.minimum(b + 1, last)

    def start_page(bb, sl, i, prio=0):
        page = pt_ref[bb * n_pages + i]
        off = pl.multiple_of(i * PAGE_SIZE, PAGE_SIZE)
        pltpu.make_async_copy(cache_c.at[page], cbuf.at[sl, pl.ds(off, PAGE_SIZE)],
                              sems.at[0, sl]).start(priority=prio)
        pltpu.make_async_copy(cache_r.at[page], rbuf.at[sl, i], sems.at[1, sl]).start(priority=1 - prio)

    def wait_pages(sl):
        pltpu.make_async_copy(cbuf.at[sl], cbuf.at[sl], sems.at[0, sl]).wait()
        pltpu.make_async_copy(rbuf.at[sl], rbuf.at[sl], sems.at[1, sl]).wait()

    @pl.when(b == 0)
    def _():
        def issue(i, carry):
            start_page(0, 0, i)
            return carry
        lax.fori_loop(0, n_pages, issue, 0)
        wext[:nw, :] = wukb_ref[...]

    q = per_ref[0, 0 * N_HEADS:1 * N_HEADS]
    qa = per_ref[0, 1 * N_HEADS:2 * N_HEADS]
    k_own = per_ref[0, 2 * N_HEADS:3 * N_HEADS]
    c_own = per_ref[0, 3 * N_HEADS:3 * N_HEADS + 1]
    wext[nw:, :] = jnp.concatenate(
        [qa, jnp.zeros((wext.shape[0] - nw - N_HEADS, KV_RANK), F32)], axis=0).astype(BF16)
    qr = q[:, QK_NOPE:QK_NOPE + QK_ROPE].astype(BF16)

    def scores(c_blk, krt_blk):
        tk = c_blk.shape[0]
        cb = c_blk.astype(BF16)
        knt = _dot_nt(wext[...], cb)
        kn3 = knt[:nw].reshape(N_HEADS, QK_NOPE, tk)
        r = lax.rsqrt(jnp.sum(kn3 * kn3, axis=1) * (1.0 / QK_NOPE) + EPS)
        s_rope = _dot(qr, krt_blk.astype(BF16))
        return cb, knt[nw:nw + N_HEADS] * r + s_rope

    def value_part(m_j, p_j, cb_j):
        return m_j, jnp.sum(p_j, axis=-1, keepdims=True), _dot(p_j.astype(BF16), cb_j)

    wait_pages(slot)
    parts, pending = [], None
    per_group = n_pages // ISSUE_TILES

    def issue_group(g):
        for i in range(per_group):
            start_page(nxt, 1 - slot, g * per_group + i, prio=i % 2)

    for j in range(n_tiles):
        keys = slice(j * KV_TILE, (j + 1) * KV_TILE)
        krt = jnp.concatenate([rbuf[slot, j * pages_per_tile + i] for i in range(pages_per_tile)], axis=1)
        cb, s = scores(cbuf[slot, keys, :], krt)
        if j < ISSUE_TILES:
            issue_group(j)
        m_j = jnp.max(s, axis=-1, keepdims=True)
        if pending is not None:
            parts.append(value_part(*pending))
        pending = (m_j, jnp.exp2(s - m_j), cb)
    parts.append(value_part(*pending))

    s_own = jnp.sum(q * k_own, axis=-1, keepdims=True)
    m = s_own
    for m_j, _, _ in parts:
        m = jnp.maximum(m, m_j)
    p_own = jnp.exp2(s_own - m)
    round_bf16 = lambda a: a.astype(BF16).astype(F32)
    l = p_own
    acc = round_bf16(p_own) * round_bf16(c_own)
    for m_j, l_j, acc_j in parts:
        w_j = jnp.exp2(m_j - m)
        l = l + w_j * l_j
        acc = acc + w_j * acc_j
    o_ref[0] = acc / l

    @pl.when(b == last)
    def _():
        wait_pages(1 - slot)


def _sample_attn(page_table, q, qa, k_own, c_own, w, cache_c, cache_rt):
    n_samp, n_pages = page_table.shape
    past = n_pages * PAGE_SIZE
    assert past % KV_TILE == 0 and n_pages % ISSUE_TILES == 0 and past // KV_TILE >= ISSUE_TILES
    wext_rows = N_HEADS * QK_NOPE + 2 * SUBLANES
    per = jnp.concatenate([q, qa, k_own.astype(F32), jnp.broadcast_to(c_own, (n_samp, N_HEADS, KV_RANK))],
                          axis=1)
    grid_spec = pltpu.PrefetchScalarGridSpec(
        num_scalar_prefetch=1,
        grid=(n_samp,),
        in_specs=[pl.BlockSpec((1,) + per.shape[1:], lambda b, pt: (b, 0, 0)),
                  pl.BlockSpec(w['w_ukt'].shape, lambda b, pt: (0, 0)),
                  pl.BlockSpec(memory_space=pl.ANY),
                  pl.BlockSpec(memory_space=pl.ANY)],
        out_specs=pl.BlockSpec((1, N_HEADS, KV_RANK), lambda b, pt: (b, 0, 0)),
        scratch_shapes=[pltpu.VMEM((wext_rows, KV_RANK), BF16),
                        pltpu.VMEM((2, past, KV_RANK), F32),
                        pltpu.VMEM((2, n_pages, QK_ROPE, PAGE_SIZE), F32),
                        pltpu.SemaphoreType.DMA((2, 2))])
    return pl.pallas_call(
        functools.partial(_sample_attn_kernel, n_pages=n_pages),
        grid_spec=grid_spec,
        out_shape=jax.ShapeDtypeStruct((n_samp, N_HEADS, KV_RANK), F32),
        compiler_params=_params(("arbitrary",)),
        name="sample_attn",
    )(page_table.reshape(-1), per, w['w_ukt'], cache_c, cache_rt)


def _mix_proj_kernel(x_ref, o_ref, u_ref, v_ref, wuv_ref, ws_ref, bias_ref, wo_ref, g_ref,
                     h_ref, hn_ref, *, single_token):
    tm = x_ref.shape[0]
    attn = _dot(o_ref[...], wuv_ref[...])
    v = v_ref[...]
    if single_token:
        mixed = v * ws_ref[...] + bias_ref[...]
    else:
        vb = v.astype(BF16)
        row = lax.broadcasted_iota(jnp.int32, (CHUNK, CHUNK), 0)
        col = lax.broadcasted_iota(jnp.int32, (CHUNK, CHUNK), 1)
        w_tril = [jnp.where(col <= row, ws_ref[g], 0.0).astype(BF16) for g in range(N_CG)]
        w_pair = [jnp.concatenate([w_tril[2 * j], w_tril[2 * j + 1]], axis=1) for j in range(N_CG // 2)]
        low_half = lax.broadcasted_iota(jnp.int32, (CHUNK, LANES), 1) < CG_DIM
        zero = jnp.zeros((CHUNK, LANES), BF16)
        chunks = []
        for c in range(tm // CHUNK):
            cols = []
            for j in range(CG_WIDTH // LANES):
                vp = vb[c * CHUNK:(c + 1) * CHUNK, j * LANES:(j + 1) * LANES]
                stacked = jnp.concatenate([jnp.where(low_half, vp, zero), jnp.where(low_half, zero, vp)], axis=0)
                cols.append(_dot(w_pair[j], stacked))
            chunks.append(jnp.concatenate(cols, axis=1) + bias_ref[...])
        mixed = jnp.concatenate(chunks, axis=0)
    sg = u_ref[...] * mixed
    width = attn.shape[1]
    h = x_ref[...] + _dot(attn.astype(BF16), wo_ref[:width, :]) + _dot(sg.astype(BF16), wo_ref[width:, :])
    h_ref[...] = h
    hn_ref[...] = (_rms(h) * g_ref[...]).astype(BF16)


def _conv_ffn_kernel(*refs, single_token):
    if single_token:
        h_ref, hn_ref, win_ref, cw_ref, cb_ref, wout_ref, h0_ref, h1_ref, out_ref, a_ref, act_s = refs
    else:
        h_ref, hn_ref, win_ref, cw_ref, cb_ref, wout_ref, out_ref, last_ref, act_s, carry = refs
    tm = hn_ref.shape[0]
    hn = hn_ref[...]

    def conv(cols):
        a = _dot(hn, win_ref[:, cols])
        if single_token:
            a2, a1 = h0_ref[:, cols], h1_ref[:, cols]
            a_ref[0, :, cols] = a1
            a_ref[1, :, cols] = a
        else:
            ext = jnp.concatenate([carry[:, cols], a], axis=0)
            a1 = ext[SUBLANES - 1:SUBLANES - 1 + tm]
            a2 = ext[SUBLANES - 2:SUBLANES - 2 + tm]
            tail = a[tm - SUBLANES:]
            carry[:, cols] = tail
            last_ref[0, :, cols] = tail[SUBLANES - (CONV_W - 1):]
        return cb_ref[:, cols] + cw_ref[0:1, cols] * a2 + cw_ref[1:2, cols] * a1 + cw_ref[2:3, cols] * a

    for j in range(D_FF // FF_TILE):
        gate = slice(j * FF_TILE, (j + 1) * FF_TILE)
        up = slice(D_FF + j * FF_TILE, D_FF + (j + 1) * FF_TILE)
        act_s[:, gate] = (jax.nn.silu(conv(gate)) * conv(up)).astype(BF16)
    out_ref[...] = h_ref[...] + _dot(act_s[...], wout_ref[...])


def _ple_kernel(h_ref, p_ref, g_ref, wgate_ref, wproj_ref, gpost_ref, y_ref):
    h = h_ref[...]
    gate = jax.nn.sigmoid(_dot((_rms(h) * g_ref[...]).astype(BF16), wgate_ref[...]))
    e = _rms(_dot(p_ref[...].astype(BF16), wproj_ref[...])) * gpost_ref[...]
    y_ref[...] = h + gate * e


def _post_attn_kernel(*refs, single_token):
    x_ref, o_ref, u_ref, v_ref, p_ref = refs[:5]
    mix_w, ffn_w, ple_w = refs[5:10], refs[10:14], refs[14:18]
    if single_token:
        h0_ref, h1_ref, y_ref, a_ref, act_s, h_s, hn_s, h2_s = refs[18:]
        _mix_proj_kernel(x_ref, o_ref, u_ref, v_ref, *mix_w, h_s, hn_s, single_token=True)
        _conv_ffn_kernel(h_s, hn_s, *ffn_w, h0_ref, h1_ref, h2_s, a_ref, act_s, single_token=True)
        _ple_kernel(h2_s, p_ref, *ple_w, y_ref)
        return
    y_ref, last_ref, act_s, carry, h_s, hn_s, h2_s = refs[18:]

    @pl.when(pl.program_id(1) == 0)
    def _():
        carry[...] = jnp.zeros(carry.shape, F32)

    _mix_proj_kernel(x_ref, o_ref, u_ref, v_ref, *mix_w, h_s, hn_s, single_token=False)
    _conv_ffn_kernel(h_s, hn_s, *ffn_w, h2_s, last_ref, act_s, carry, single_token=False)
    _ple_kernel(h2_s, p_ref, *ple_w, y_ref)


def _post_attn(x, o_lat, u, v, p, w, batch, seq, tm, hist=None):
    single_token = hist is not None
    nt = seq // tm
    tok = lambda width: pl.BlockSpec((tm, width), lambda b, t: (b * nt + t, 0))
    resident = lambda a: pl.BlockSpec(a.shape, lambda b, t: (0,) * a.ndim, pipeline_mode=pl.Buffered(1))
    ws, bias = (w['ws_diag'], w['bias_first']) if single_token else (w['w_s'], w['bias_tab'])
    weights = [w['w_uv_bd'], ws, bias, w['w_o'], w['g_ffn'],
               w['w_ff_in'], w['conv_w'], w['conv_b'], w['w_ff_out'],
               w['g_ple'], w['w_gate'], w['w_proj'], w['g_post']]
    in_specs = ([tok(D_MODEL), tok(N_HEADS * KV_RANK), tok(CG_WIDTH), tok(CG_WIDTH), tok(PLE_DIM)]
                + [resident(a) for a in weights])
    args = [x, o_lat, u, v, p] + weights
    scratch = [pltpu.VMEM((tm, D_FF), BF16)]
    if single_token:
        in_specs += [resident(hist[0]), resident(hist[1])]
        args += list(hist)
        last_spec = pl.BlockSpec((CONV_W - 1, tm, 2 * D_FF), lambda b, t: (0, 0, 0))
        last_shape = jax.ShapeDtypeStruct((CONV_W - 1, tm, 2 * D_FF), F32)
    else:
        last_spec = pl.BlockSpec((1, CONV_W - 1, 2 * D_FF), lambda b, t: (b * nt + t, 0, 0))
        last_shape = jax.ShapeDtypeStruct((batch * nt, CONV_W - 1, 2 * D_FF), F32)
        scratch.append(pltpu.VMEM((SUBLANES, 2 * D_FF), F32))
    scratch += [pltpu.VMEM((tm, D_MODEL), F32), pltpu.VMEM((tm, D_MODEL), BF16),
                pltpu.VMEM((tm, D_MODEL), F32)]
    return pl.pallas_call(
        functools.partial(_post_attn_kernel, single_token=single_token),
        grid=(batch, nt),
        in_specs=in_specs,
        out_specs=[tok(D_MODEL), last_spec],
        out_shape=[jax.ShapeDtypeStruct((batch * seq, D_MODEL), F32), last_shape],
        scratch_shapes=scratch,
        compiler_params=_params(("arbitrary", "arbitrary")),
        name="post_attn_single" if single_token else "post_attn",
    )(*args)


def _rope_tables(pos):
    inv = ROPE_THETA ** (-jnp.arange(ROPE_HALF, dtype=F32) * (2.0 / QK_ROPE))
    ang = pos.astype(F32)[:, None] * inv[None, :]
    cos, sin = jnp.cos(ang), jnp.sin(ang)
    n = pos.shape[0]
    ones = jnp.ones((n, QK_NOPE), F32)
    zq = jnp.zeros((n, QK_NOPE), F32)
    zp = jnp.zeros((n, HEAD_PAD - QK_NOPE - QK_ROPE), F32)
    ca = jnp.concatenate([ones, cos, cos, zp], axis=1)
    sb = jnp.concatenate([zq, -sin, sin, zp], axis=1)
    return ca, sb, cos.T, sin.T


def _prep_weights(i, attn_norm_w, w_in, q_norm_w, w_uq, q_nope_norm_w, q_rope_norm_w, kv_norm_w,
                  k_rope_norm_w, w_uk, k_nope_norm_w, w_uv, w_s, b_s, w_o, ffn_norm_w, w_ff_in,
                  conv_w, conv_b, w_ff_out, ple_norm_w, w_ple_gate, w_ple_proj, ple_post_norm_w):
    row = lambda g: g[i][None, :]
    swap = lambda a: jnp.concatenate([a[..., ROPE_HALF:], a[..., :ROPE_HALF]], axis=-1)
    win = w_in[i]
    wq = w_uq[i].reshape(Q_RANK, N_HEADS, QK_NOPE + QK_ROPE)
    nope, rope = wq[..., :QK_NOPE], wq[..., QK_NOPE:]
    pad = HEAD_PAD - QK_NOPE - QK_ROPE
    w_qt = jnp.concatenate([nope, rope, jnp.zeros((Q_RANK, N_HEADS, pad), F32)],
                           axis=-1).reshape(Q_RANK, QK_WIDTH).T
    col = lambda g: jnp.broadcast_to(g[:, None], (g.shape[0], LANES))
    kr_cols = win[:, OFF_KR:OFF_U]
    w_in_e = jnp.concatenate([win[:, :OFF_KR], win[:, OFF_U:], jnp.zeros((D_MODEL, QK_NOPE), F32),
                              kr_cols, swap(kr_cols)], axis=1)
    w_uk_e = jnp.concatenate([w_uk[i], jnp.zeros((KV_RANK, N_HEADS, HEAD_PAD - QK_NOPE), F32)],
                             axis=-1).reshape(KV_RANK, QK_WIDTH)
    gn, gr = k_nope_norm_w[i], k_rope_norm_w[i]
    ga = jnp.tile(jnp.concatenate([gn, gr, jnp.zeros((pad,), F32)]), N_HEADS)[None, :]
    gb = jnp.tile(jnp.concatenate([jnp.zeros((QK_NOPE,), F32), swap(gr), jnp.zeros((pad,), F32)]),
                  N_HEADS)[None, :]
    lane = jnp.arange(2 * HEAD_PAD)
    slot, off = lane // HEAD_PAD, lane % HEAD_PAD
    grp = jnp.where(off < QK_NOPE, 0, jnp.where(off < QK_NOPE + QK_ROPE, 1, 2))
    same = (slot[:, None] == slot[None, :]) & (grp[:, None] == grp[None, :]) & (grp[:, None] < 2)
    gm = jnp.where(same, jnp.where(grp[:, None] == 0, 1.0 / QK_NOPE, 1.0 / QK_ROPE), 0.0)
    w_ukt = w_uk[i].reshape(KV_RANK, N_HEADS * QK_NOPE).T
    eye = jnp.eye(N_HEADS, dtype=F32)
    w_uv_bd = jnp.einsum('rhd,hg->hrgd', w_uv[i], eye).reshape(N_HEADS * KV_RANK, N_HEADS * V_DIM)
    return {
        'g_attn': row(attn_norm_w),
        'w_in': w_in_e.astype(BF16),
        'g_q': row(q_norm_w), 'w_qt': w_qt.astype(BF16),
        'g_qn': col(q_nope_norm_w[i]), 'g_qr': col(q_rope_norm_w[i]),
        'g_kv': row(kv_norm_w), 'w_kvt': win[:, OFF_KV:OFF_KR].T.astype(BF16), 'g_kvt': col(kv_norm_w[i]),
        'w_krt': kr_cols.T.astype(BF16), 'g_krt': col(k_rope_norm_w[i]),
        'w_uk': w_uk_e.astype(BF16), 'gm': gm.astype(BF16), 'ga': ga, 'gb': gb,
        'w_ukt': w_ukt.astype(BF16), 'w_ukt_f32': w_ukt,
        'g_kn_row': jnp.tile(k_nope_norm_w[i], N_HEADS)[None, :],
        'w_uv_bd': w_uv_bd.astype(BF16),
        'w_s': w_s[i],
        'bias_tab': jnp.repeat(b_s[i].T, CG_DIM, axis=1),
        'ws_diag': jnp.repeat(w_s[i][:, 0, 0], CG_DIM)[None, :],
        'bias_first': jnp.repeat(b_s[i][:, 0], CG_DIM)[None, :],
        'w_o': w_o[i].astype(BF16), 'g_ffn': row(ffn_norm_w),
        'w_ff_in': w_ff_in[i].astype(BF16), 'conv_w': conv_w[i], 'conv_b': row(conv_b),
        'w_ff_out': w_ff_out[i].astype(BF16),
        'g_ple': row(ple_norm_w), 'w_gate': w_ple_gate[i].astype(BF16),
        'w_proj': w_ple_proj[i].astype(BF16), 'g_post': row(ple_post_norm_w),
    }


def kernel(x_prompt, x_sample, cache_ckv, cache_krope, state_conv, page_table, p_prompt, p_sample,
           attn_norm_w, w_in, q_norm_w, w_uq, q_nope_norm_w, q_rope_norm_w, kv_norm_w, k_rope_norm_w,
           w_uk, k_nope_norm_w, w_uv, w_s, b_s, w_o, ffn_norm_w, w_ff_in, conv_w, conv_b, w_ff_out,
           ple_norm_w, w_ple_gate, w_ple_proj, ple_post_norm_w):
    batch, seq, _ = x_prompt.shape
    n_samp, seq_s, _ = x_sample.shape
    depth = w_in.shape[0]
    assert depth == 1 and seq_s == 1 and seq % TOK_TILE == 0 and seq % CHUNK == 0
    past_len = page_table.shape[1] * PAGE_SIZE
    assert past_len % KV_TILE == 0
    w = _prep_weights(0, attn_norm_w, w_in, q_norm_w, w_uq, q_nope_norm_w, q_rope_norm_w, kv_norm_w,
                      k_rope_norm_w, w_uk, k_nope_norm_w, w_uv, w_s, b_s, w_o, ffn_norm_w, w_ff_in,
                      conv_w, conv_b, w_ff_out, ple_norm_w, w_ple_gate, w_ple_proj, ple_post_norm_w)

    n_tok = batch * seq
    tabs = _rope_tables(jnp.arange(seq, dtype=jnp.int32))
    xp = x_prompt.reshape(n_tok, D_MODEL)
    qt, k, ckv, vt, krt, u, v = _in_proj(xp, tabs, w, 2 * TOK_TILE)
    o_lat = _prompt_attn(qt, k, vt, batch, seq)
    y_prompt, a_last = _post_attn(xp, o_lat, u, v, p_prompt[0].reshape(n_tok, PLE_DIM), w,
                                  batch, seq, TOK_TILE)
    y_prompt = y_prompt.reshape(batch, seq, D_MODEL)
    n_keep = (seq - 1) % CHUNK + 1
    new_ckv_p = ckv.reshape(1, batch, seq, KV_RANK)
    new_kr_p = jnp.swapaxes(krt, 1, 2)[None]
    new_v_p = v.reshape(batch, seq, CG_WIDTH)[:, seq - n_keep:][None]
    new_conv_p = a_last.reshape(batch, seq // TOK_TILE, CONV_W - 1, 2 * D_FF)[:, -1][None]

    tabs_s = _rope_tables(jnp.full((n_samp,), past_len, dtype=jnp.int32))
    xs = x_sample.reshape(n_samp, D_MODEL)
    qt_s, k_s, ckv_s, _, krt_s, u_s, v_s = _in_proj(xs, tabs_s, w, n_samp)
    q3 = jnp.transpose(qt_s, (2, 0, 1)).astype(F32)
    kr_s = krt_s[0].T
    qa = _absorb_q(q3[:, :, :QK_NOPE].reshape(n_samp, N_HEADS * QK_NOPE), w)
    cache_rt = jnp.swapaxes(cache_krope[0], 1, 2)
    o_s = _sample_attn(page_table, q3, qa.reshape(n_samp, N_HEADS, KV_RANK),
                       k_s.reshape(n_samp, N_HEADS, HEAD_PAD), ckv_s.reshape(n_samp, 1, KV_RANK),
                       w, cache_ckv[0], cache_rt)
    o_s = o_s.reshape(n_samp, N_HEADS * KV_RANK).astype(BF16)
    hist = (state_conv[0, :, 0, :], state_conv[0, :, 1, :])
    y_sample, a_s = _post_attn(xs, o_s, u_s, v_s, p_sample[0].reshape(n_samp, PLE_DIM), w,
                               1, n_samp, n_samp, hist=hist)
    y_sample = y_sample.reshape(n_samp, 1, D_MODEL)
    new_conv_s = jnp.swapaxes(a_s, 0, 1)[None]

    return (y_prompt, y_sample, new_ckv_p, new_kr_p,
            ckv_s.reshape(1, n_samp, 1, KV_RANK), kr_s.reshape(1, n_samp, 1, QK_ROPE),
            new_v_p, v_s.reshape(1, n_samp, 1, CG_WIDTH), new_conv_p, new_conv_s)
```

```python
import functools

import jax
import jax.numpy as jnp
from jax import lax
from jax.experimental import pallas as pl
from jax.experimental.pallas import tpu as pltpu

F32 = jnp.float32
BF16 = jnp.bfloat16

D_MODEL = 1024
N_HEADS = 8
QK_NOPE = 64
QK_ROPE = 32
ROPE_HALF = QK_ROPE // 2
V_DIM = 64
Q_RANK = 256
KV_RANK = 128
CHUNK = 128
N_CG = 8
CG_WIDTH = 512
CG_DIM = 64
D_FF = 2816
CONV_W = 3
PLE_DIM = 256
PAGE_SIZE = 128
ROPE_THETA = 10000.0
EPS = 1e-6
SCALE = (QK_NOPE + QK_ROPE) ** -0.5
Q_SCALE = SCALE * 1.4426950408889634
OFF_KV = Q_RANK
OFF_KR = OFF_KV + KV_RANK
OFF_U = OFF_KR + QK_ROPE
HEAD_PAD = 128
QK_WIDTH = N_HEADS * HEAD_PAD
Z_WIDTH = Q_RANK + KV_RANK + 2 * CG_WIDTH

LANES = 128
SUBLANES = 8
VMEM_LIMIT_BYTES = 56 * 1024 * 1024

TOK_TILE = 512
ATT_BLOCK = 1024
KEY_BLOCK = 512
ONES_ROWS = 16
FF_TILE = 256
KV_TILE = 2048
ISSUE_TILES = 4

NT_DIMS = (((1,), (1,)), ((), ()))


def _params(semantics):
    return pltpu.CompilerParams(dimension_semantics=semantics,
                                vmem_limit_bytes=VMEM_LIMIT_BYTES)


def _rms(x):
    return x * lax.rsqrt(jnp.mean(x * x, axis=-1, keepdims=True) + EPS)


def _dot(a, b):
    return jnp.dot(a, b, preferred_element_type=F32)


def _dot_nt(a, b):
    return lax.dot_general(a, b, NT_DIMS, preferred_element_type=F32)


def _full(shape):
    zeros = (0,) * len(shape)
    return pl.BlockSpec(shape, lambda *_: zeros)


def _in_proj_kernel(x_ref, ca_ref, sb_ref, cost_ref, sint_ref,
                    g_attn_ref, w_in_ref, g_q_ref, w_qt_ref, g_qn_ref, g_qr_ref,
                    g_kv_ref, w_kvt_ref, g_kvt_ref, w_krt_ref, g_krt_ref, w_uk_ref, gm_ref, ga_ref, gb_ref,
                    qt_ref, k_ref, ckv_ref, vt_ref, krt_ref, u_ref, v_ref):
    tm = x_ref.shape[0]
    reps = tm // LANES
    cos, sin = cost_ref[...], sint_ref[...]
    hb = (_rms(x_ref[...]) * g_attn_ref[...]).astype(BF16)
    z = _dot(hb, w_in_ref[...])
    ckv = _rms(z[:, OFF_KV:OFF_KR]) * g_kv_ref[...]
    ckv_ref[...] = ckv
    kr_slot = z[:, Z_WIDTH:]
    ka = _dot(ckv.astype(BF16), w_uk_ref[...]) + jnp.tile(kr_slot, (1, N_HEADS))
    ct = _dot_nt(w_kvt_ref[...], hb)
    krt = _dot_nt(w_krt_ref[...], hb)
    qln = (_rms(z[:, :Q_RANK]) * g_q_ref[...]).astype(BF16)
    qt = _dot_nt(w_qt_ref[...], qln).reshape(N_HEADS, HEAD_PAD, tm)
    sq = (ka * ka).astype(BF16)
    gm = gm_ref[...]
    pair = 2 * HEAD_PAD
    ms = jnp.concatenate([_dot(sq[:, i * pair:(i + 1) * pair], gm) for i in range(N_HEADS // 2)], axis=1)
    krb = pltpu.roll(kr_slot, HEAD_PAD - QK_ROPE, axis=1)

    u_ref[...] = jax.nn.gelu(z[:, OFF_KR:OFF_KR + CG_WIDTH])
    v_ref[...] = jax.nn.gelu(z[:, OFF_KR + CG_WIDTH:OFF_KR + 2 * CG_WIDTH])
    ct = ct * lax.rsqrt(jnp.mean(ct * ct, axis=0, keepdims=True) + EPS)
    vt_ref[...] = (ct * jnp.tile(g_kvt_ref[...], (1, reps))).astype(BF16)
    krt = krt * lax.rsqrt(jnp.mean(krt * krt, axis=0, keepdims=True) + EPS) * jnp.tile(g_krt_ref[...], (1, reps))
    kx1, kx2 = krt[:ROPE_HALF], krt[ROPE_HALF:]
    krt_ref[0] = jnp.concatenate([kx1 * cos - kx2 * sin, kx1 * sin + kx2 * cos], axis=0)

    nope = qt[:, :QK_NOPE]
    rope = qt[:, QK_NOPE:QK_NOPE + QK_ROPE]
    g_qn = jnp.tile(g_qn_ref[...], (1, reps))[None]
    g_qr = jnp.tile(g_qr_ref[...], (1, reps))[None]
    nope = nope * lax.rsqrt(jnp.mean(nope * nope, axis=1, keepdims=True) + EPS) * (g_qn * Q_SCALE)
    rope = rope * lax.rsqrt(jnp.mean(rope * rope, axis=1, keepdims=True) + EPS) * (g_qr * Q_SCALE)
    x1, x2 = rope[:, :ROPE_HALF], rope[:, ROPE_HALF:]
    cos, sin = cos[None], sin[None]
    pad = jnp.zeros((N_HEADS, HEAD_PAD - QK_NOPE - QK_ROPE, tm), F32)
    qt_ref[...] = jnp.concatenate([nope, x1 * cos - x2 * sin, x1 * sin + x2 * cos, pad], axis=1).astype(BF16)

    ca = jnp.tile(ca_ref[...], (1, N_HEADS))
    sb = jnp.tile(sb_ref[...], (1, N_HEADS))
    k = lax.rsqrt(ms + EPS) * (ka * ga_ref[...] * ca + jnp.tile(krb, (1, N_HEADS)) * gb_ref[...] * sb)
    k_ref[...] = k.astype(BF16)


def _in_proj(x, tabs, w, tm):
    n = x.shape[0]
    ca, sb, cost, sint = tabs
    pos_tiles = ca.shape[0] // tm
    tok = lambda width: pl.BlockSpec((tm, width), lambda i: (i, 0))
    tok_t = lambda rows: pl.BlockSpec((rows, tm), lambda i: (0, i))
    pos = pl.BlockSpec((tm, HEAD_PAD), lambda i: (i % pos_tiles, 0))
    pos_t = pl.BlockSpec((ROPE_HALF, tm), lambda i: (0, i % pos_tiles))
    weights = [w['g_attn'], w['w_in'], w['g_q'], w['w_qt'], w['g_qn'], w['g_qr'],
               w['g_kv'], w['w_kvt'], w['g_kvt'], w['w_krt'], w['g_krt'], w['w_uk'], w['gm'], w['ga'], w['gb']]
    return pl.pallas_call(
        _in_proj_kernel,
        grid=(n // tm,),
        in_specs=[tok(D_MODEL), pos, pos, pos_t, pos_t]
                 + [_full(a.shape) for a in weights],
        out_specs=[pl.BlockSpec((N_HEADS, HEAD_PAD, tm), lambda i: (0, 0, i)),
                   tok(QK_WIDTH), tok(KV_RANK), tok_t(KV_RANK),
                   pl.BlockSpec((1, QK_ROPE, tm), lambda i: (i // pos_tiles, 0, i % pos_tiles)),
                   tok(CG_WIDTH), tok(CG_WIDTH)],
        out_shape=[jax.ShapeDtypeStruct((N_HEADS, HEAD_PAD, n), BF16),
                   jax.ShapeDtypeStruct((n, QK_WIDTH), BF16),
                   jax.ShapeDtypeStruct((n, KV_RANK), F32),
                   jax.ShapeDtypeStruct((KV_RANK, n), BF16),
                   jax.ShapeDtypeStruct((n // (pos_tiles * tm), QK_ROPE, pos_tiles * tm), F32),
                   jax.ShapeDtypeStruct((n, CG_WIDTH), F32),
                   jax.ShapeDtypeStruct((n, CG_WIDTH), F32)],
        compiler_params=_params(("parallel",)),
        name="in_proj",
    )(x, ca, sb, cost, sint, *weights)


def _prompt_attn_kernel(qt_ref, k_ref, vt_ref, o_ref, m_s, acc_s, s_s):
    qi = pl.program_id(1)
    kb = KEY_BLOCK
    per_q = ATT_BLOCK // kb
    m_s[...] = jnp.full(m_s.shape, -jnp.inf, F32)
    acc_s[...] = jnp.zeros(acc_s.shape, F32)
    ones = jnp.ones((ONES_ROWS, kb), BF16)

    def key_block(off, first_q, masked):
        cols = slice(first_q, ATT_BLOCK)
        vals = jnp.concatenate([vt_ref[:, pl.ds(off, kb)], ones], axis=0)
        if masked:
            key = lax.broadcasted_iota(jnp.int32, (kb, ATT_BLOCK - first_q), 0)
            qry = lax.broadcasted_iota(jnp.int32, (kb, ATT_BLOCK - first_q), 1)
            visible = key <= qry

        def score_stage(h):
            s = _dot(k_ref[pl.ds(off, kb), h * HEAD_PAD:(h + 1) * HEAD_PAD], qt_ref[h, :, cols])
            if masked:
                s = jnp.where(visible, s, -jnp.inf)
            m_old = m_s[h, :, cols]
            m_new = jnp.maximum(m_old, jnp.max(s, axis=0, keepdims=True))
            m_s[h, :, cols] = m_new
            s_s[h % 2, :, cols] = s
            return m_old, m_new

        def value_stage(h, m_old, m_new):
            p = jnp.exp2(s_s[h % 2, :, cols] - m_new)
            acc_s[h, :, cols] = jnp.exp2(m_old - m_new) * acc_s[h, :, cols] + _dot(vals, p.astype(BF16))

        pending = score_stage(0)
        for h in range(N_HEADS):
            nxt = score_stage(h + 1) if h + 1 < N_HEADS else None
            value_stage(h, *pending)
            pending = nxt

    def body(j, carry):
        key_block(pl.multiple_of(j * kb, kb), 0, False)
        return carry

    lax.fori_loop(0, qi * per_q, body, 0)
    for d in range(per_q):
        key_block(pl.multiple_of(qi * ATT_BLOCK + d * kb, kb), d * kb, True)
    for h in range(N_HEADS):
        acc = acc_s[h]
        o_t = acc[:KV_RANK] / acc[KV_RANK:KV_RANK + 1]
        o_ref[:, h * HEAD_PAD:(h + 1) * HEAD_PAD] = o_t.T.astype(BF16)


def _prompt_attn(qt, k, vt, batch, seq):
    nq = seq // ATT_BLOCK
    return pl.pallas_call(
        _prompt_attn_kernel,
        grid=(batch, nq),
        in_specs=[pl.BlockSpec((N_HEADS, HEAD_PAD, ATT_BLOCK), lambda b, i: (0, 0, b * nq + i)),
                  pl.BlockSpec((seq, QK_WIDTH), lambda b, i: (b, 0)),
                  pl.BlockSpec((KV_RANK, seq), lambda b, i: (0, b))],
        out_specs=pl.BlockSpec((ATT_BLOCK, N_HEADS * KV_RANK), lambda b, i: (b * nq + i, 0)),
        out_shape=jax.ShapeDtypeStruct((batch * seq, N_HEADS * KV_RANK), BF16),
        scratch_shapes=[pltpu.VMEM((N_HEADS, 1, ATT_BLOCK), F32),
                        pltpu.VMEM((N_HEADS, KV_RANK + ONES_ROWS, ATT_BLOCK), F32),
                        pltpu.VMEM((2, KEY_BLOCK, ATT_BLOCK), F32)],
        compiler_params=_params(("parallel", "parallel")),
        name="prompt_attn",
    )(qt, k, vt)


def _absorb_q_kernel(qn_ref, g_ref, wukt_ref, qa_ref):
    qg = qn_ref[...] * g_ref[...]
    lane = lax.broadcasted_iota(jnp.int32, qg.shape, 1)
    for h in range(N_HEADS):
        q_h = jnp.where((lane >= h * QK_NOPE) & (lane < (h + 1) * QK_NOPE), qg, 0.0)
        qa_ref[:, h * KV_RANK:(h + 1) * KV_RANK] = jnp.dot(
            q_h, wukt_ref[...], precision=lax.Precision.HIGHEST, preferred_element_type=F32)


def _absorb_q(qn, w):
    n_samp = qn.shape[0]
    args = [qn, w['g_kn_row'], w['w_ukt_f32']]
    return pl.pallas_call(
        _absorb_q_kernel,
        grid=(1,),
        in_specs=[_full(a.shape) for a in args],
        out_specs=_full((n_samp, N_HEADS * KV_RANK)),
        out_shape=jax.ShapeDtypeStruct((n_samp, N_HEADS * KV_RANK), F32),
        compiler_params=_params(("arbitrary",)),
        name="absorb_q",
    )(*args)


def _sample_attn_kernel(pt_ref, per_ref, wukb_ref,
                        cache_c, cache_r, o_ref,
                        wext, cbuf, rbuf, sems, *, n_pages):
    b = pl.program_id(0)
    last = pl.num_programs(0) - 1
    slot = lax.rem(b, 2)
    past = n_pages * PAGE_SIZE
    n_tiles = past // KV_TILE
    pages_per_tile = KV_TILE // PAGE_SIZE
    nw = N_HEADS * QK_NOPE
    nxt = jnp.minimum(b + 1, last)

    def start_page(bb, sl, i, prio=0):
        page = pt_ref[bb * n_pages + i]
        off = pl.multiple_of(i * PAGE_SIZE, PAGE_SIZE)
        pltpu.make_async_copy(cache_c.at[page], cbuf.at[sl, pl.ds(off, PAGE_SIZE)],
                              sems.at[0, sl]).start(priority=prio)
        pltpu.make_async_copy(cache_r.at[page], rbuf.at[sl, i], sems.at[1, sl]).start(priority=1 - prio)

    def wait_pages(sl):
        pltpu.make_async_copy(cbuf.at[sl], cbuf.at[sl], sems.at[0, sl]).wait()
        pltpu.make_async_copy(rbuf.at[sl], rbuf.at[sl], sems.at[1, sl]).wait()

    @pl.when(b == 0)
    def _():
        def issue(i, carry):
            start_page(0, 0, i)
            return carry
        lax.fori_loop(0, n_pages, issue, 0)
        wext[:nw, :] = wukb_ref[...]

    q = per_ref[0, 0 * N_HEADS:1 * N_HEADS]
    qa = per_ref[0, 1 * N_HEADS:2 * N_HEADS]
    k_own = per_ref[0, 2 * N_HEADS:3 * N_HEADS]
    c_own = per_ref[0, 3 * N_HEADS:3 * N_HEADS + 1]
    wext[nw:, :] = jnp.concatenate(
        [qa, jnp.zeros((wext.shape[0] - nw - N_HEADS, KV_RANK), F32)], axis=0).astype(BF16)
    qr = q[:, QK_NOPE:QK_NOPE + QK_ROPE].astype(BF16)

    def scores(c_blk, krt_blk):
        tk = c_blk.shape[0]
        cb = c_blk.astype(BF16)
        knt = _dot_nt(wext[...], cb)
        kn3 = knt[:nw].reshape(N_HEADS, QK_NOPE, tk)
        r = lax.rsqrt(jnp.sum(kn3 * kn3, axis=1) * (1.0 / QK_NOPE) + EPS)
        s_rope = _dot(qr, krt_blk.astype(BF16))
        return cb, knt[nw:nw + N_HEADS] * r + s_rope

    def value_part(m_j, p_j, cb_j):
        return m_j, jnp.sum(p_j, axis=-1, keepdims=True), _dot(p_j.astype(BF16), cb_j)

    wait_pages(slot)
    parts, pending = [], None
    per_group = n_pages // ISSUE_TILES

    def issue_group(g):
        for i in range(per_group):
            start_page(nxt, 1 - slot, g * per_group + i, prio=i % 2)

    for j in range(n_tiles):
        keys = slice(j * KV_TILE, (j + 1) * KV_TILE)
        krt = jnp.concatenate([rbuf[slot, j * pages_per_tile + i] for i in range(pages_per_tile)], axis=1)
        cb, s = scores(cbuf[slot, keys, :], krt)
        if j < ISSUE_TILES:
            issue_group(j)
        m_j = jnp.max(s, axis=-1, keepdims=True)
        if pending is not None:
            parts.append(value_part(*pending))
        pending = (m_j, jnp.exp2(s - m_j), cb)
    parts.append(value_part(*pending))

    s_own = jnp.sum(q * k_own, axis=-1, keepdims=True)
    m = s_own
    for m_j, _, _ in parts:
        m = jnp.maximum(m, m_j)
    p_own = jnp.exp2(s_own - m)
    round_bf16 = lambda a: a.astype(BF16).astype(F32)
    l = p_own
    acc = round_bf16(p_own) * round_bf16(c_own)
    for m_j, l_j, acc_j in parts:
        w_j = jnp.exp2(m_j - m)
        l = l + w_j * l_j
        acc = acc + w_j * acc_j
    o_ref[0] = acc / l

    @pl.when(b == last)
    def _():
        wait_pages(1 - slot)


def _sample_attn(page_table, q, qa, k_own, c_own, w, cache_c, cache_rt):
    n_samp, n_pages = page_table.shape
    past = n_pages * PAGE_SIZE
    assert past % KV_TILE == 0 and n_pages % ISSUE_TILES == 0 and past // KV_TILE >= ISSUE_TILES
    wext_rows = N_HEADS * QK_NOPE + 2 * SUBLANES
    per = jnp.concatenate([q, qa, k_own.astype(F32), jnp.broadcast_to(c_own, (n_samp, N_HEADS, KV_RANK))],
                          axis=1)
    grid_spec = pltpu.PrefetchScalarGridSpec(
        num_scalar_prefetch=1,
        grid=(n_samp,),
        in_specs=[pl.BlockSpec((1,) + per.shape[1:], lambda b, pt: (b, 0, 0)),
                  pl.BlockSpec(w['w_ukt'].shape, lambda b, pt: (0, 0)),
                  pl.BlockSpec(memory_space=pl.ANY),
                  pl.BlockSpec(memory_space=pl.ANY)],
        out_specs=pl.BlockSpec((1, N_HEADS, KV_RANK), lambda b, pt: (b, 0, 0)),
        scratch_shapes=[pltpu.VMEM((wext_rows, KV_RANK), BF16),
                        pltpu.VMEM((2, past, KV_RANK), F32),
                        pltpu.VMEM((2, n_pages, QK_ROPE, PAGE_SIZE), F32),
                        pltpu.SemaphoreType.DMA((2, 2))])
    return pl.pallas_call(
        functools.partial(_sample_attn_kernel, n_pages=n_pages),
        grid_spec=grid_spec,
        out_shape=jax.ShapeDtypeStruct((n_samp, N_HEADS, KV_RANK), F32),
        compiler_params=_params(("arbitrary",)),
        name="sample_attn",
    )(page_table.reshape(-1), per, w['w_ukt'], cache_c, cache_rt)


def _mix_proj_kernel(x_ref, o_ref, u_ref, v_ref, wuv_ref, ws_ref, bias_ref, wo_ref, g_ref,
                     h_ref, hn_ref, *, single_token):
    tm = x_ref.shape[0]
    attn = _dot(o_ref[...], wuv_ref[...])
    v = v_ref[...]
    if single_token:
        mixed = v * ws_ref[...] + bias_ref[...]
    else:
        vb = v.astype(BF16)
        row = lax.broadcasted_iota(jnp.int32, (CHUNK, CHUNK), 0)
        col = lax.broadcasted_iota(jnp.int32, (CHUNK, CHUNK), 1)
        w_tril = [jnp.where(col <= row, ws_ref[g], 0.0).astype(BF16) for g in range(N_CG)]
        w_pair = [jnp.concatenate([w_tril[2 * j], w_tril[2 * j + 1]], axis=1) for j in range(N_CG // 2)]
        low_half = lax.broadcasted_iota(jnp.int32, (CHUNK, LANES), 1) < CG_DIM
        zero = jnp.zeros((CHUNK, LANES), BF16)
        chunks = []
        for c in range(tm // CHUNK):
            cols = []
            for j in range(CG_WIDTH // LANES):
                vp = vb[c * CHUNK:(c + 1) * CHUNK, j * LANES:(j + 1) * LANES]
                stacked = jnp.concatenate([jnp.where(low_half, vp, zero), jnp.where(low_half, zero, vp)], axis=0)
                cols.append(_dot(w_pair[j], stacked))
            chunks.append(jnp.concatenate(cols, axis=1) + bias_ref[...])
        mixed = jnp.concatenate(chunks, axis=0)
    sg = u_ref[...] * mixed
    width = attn.shape[1]
    h = x_ref[...] + _dot(attn.astype(BF16), wo_ref[:width, :]) + _dot(sg.astype(BF16), wo_ref[width:, :])
    h_ref[...] = h
    hn_ref[...] = (_rms(h) * g_ref[...]).astype(BF16)


def _conv_ffn_kernel(*refs, single_token):
    if single_token:
        h_ref, hn_ref, win_ref, cw_ref, cb_ref, wout_ref, h0_ref, h1_ref, out_ref, a_ref, act_s = refs
    else:
        h_ref, hn_ref, win_ref, cw_ref, cb_ref, wout_ref, out_ref, last_ref, act_s, carry = refs
    tm = hn_ref.shape[0]
    hn = hn_ref[...]

    def conv(cols):
        a = _dot(hn, win_ref[:, cols])
        if single_token:
            a2, a1 = h0_ref[:, cols], h1_ref[:, cols]
            a_ref[0, :, cols] = a1
            a_ref[1, :, cols] = a
        else:
            ext = jnp.concatenate([carry[:, cols], a], axis=0)
            a1 = ext[SUBLANES - 1:SUBLANES - 1 + tm]
            a2 = ext[SUBLANES - 2:SUBLANES - 2 + tm]
            tail = a[tm - SUBLANES:]
            carry[:, cols] = tail
            last_ref[0, :, cols] = tail[SUBLANES - (CONV_W - 1):]
        return cb_ref[:, cols] + cw_ref[0:1, cols] * a2 + cw_ref[1:2, cols] * a1 + cw_ref[2:3, cols] * a

    for j in range(D_FF // FF_TILE):
        gate = slice(j * FF_TILE, (j + 1) * FF_TILE)
        up = slice(D_FF + j * FF_TILE, D_FF + (j + 1) * FF_TILE)
        act_s[:, gate] = (jax.nn.silu(conv(gate)) * conv(up)).astype(BF16)
    out_ref[...] = h_ref[...] + _dot(act_s[...], wout_ref[...])


def _ple_kernel(h_ref, p_ref, g_ref, wgate_ref, wproj_ref, gpost_ref, y_ref):
    h = h_ref[...]
    gate = jax.nn.sigmoid(_dot((_rms(h) * g_ref[...]).astype(BF16), wgate_ref[...]))
    e = _rms(_dot(p_ref[...].astype(BF16), wproj_ref[...])) * gpost_ref[...]
    y_ref[...] = h + gate * e


def _post_attn_kernel(*refs, single_token):
    x_ref, o_ref, u_ref, v_ref, p_ref = refs[:5]
    mix_w, ffn_w, ple_w = refs[5:10], refs[10:14], refs[14:18]
    if single_token:
        h0_ref, h1_ref, y_ref, a_ref, act_s, h_s, hn_s, h2_s = refs[18:]
        _mix_proj_kernel(x_ref, o_ref, u_ref, v_ref, *mix_w, h_s, hn_s, single_token=True)
        _conv_ffn_kernel(h_s, hn_s, *ffn_w, h0_ref, h1_ref, h2_s, a_ref, act_s, single_token=True)
        _ple_kernel(h2_s, p_ref, *ple_w, y_ref)
        return
    y_ref, last_ref, act_s, carry, h_s, hn_s, h2_s = refs[18:]

    @pl.when(pl.program_id(1) == 0)
    def _():
        carry[...] = jnp.zeros(carry.shape, F32)

    _mix_proj_kernel(x_ref, o_ref, u_ref, v_ref, *mix_w, h_s, hn_s, single_token=False)
    _conv_ffn_kernel(h_s, hn_s, *ffn_w, h2_s, last_ref, act_s, carry, single_token=False)
    _ple_kernel(h2_s, p_ref, *ple_w, y_ref)


def _post_attn(x, o_lat, u, v, p, w, batch, seq, tm, hist=None):
    single_token = hist is not None
    nt = seq // tm
    tok = lambda width: pl.BlockSpec((tm, width), lambda b, t: (b * nt + t, 0))
    resident = lambda a: pl.BlockSpec(a.shape, lambda b, t: (0,) * a.ndim, pipeline_mode=pl.Buffered(1))
    ws, bias = (w['ws_diag'], w['bias_first']) if single_token else (w['w_s'], w['bias_tab'])
    weights = [w['w_uv_bd'], ws, bias, w['w_o'], w['g_ffn'],
               w['w_ff_in'], w['conv_w'], w['conv_b'], w['w_ff_out'],
               w['g_ple'], w['w_gate'], w['w_proj'], w['g_post']]
    in_specs = ([tok(D_MODEL), tok(N_HEADS * KV_RANK), tok(CG_WIDTH), tok(CG_WIDTH), tok(PLE_DIM)]
                + [resident(a) for a in weights])
    args = [x, o_lat, u, v, p] + weights
    scratch = [pltpu.VMEM((tm, D_FF), BF16)]
    if single_token:
        in_specs += [resident(hist[0]), resident(hist[1])]
        args += list(hist)
        last_spec = pl.BlockSpec((CONV_W - 1, tm, 2 * D_FF), lambda b, t: (0, 0, 0))
        last_shape = jax.ShapeDtypeStruct((CONV_W - 1, tm, 2 * D_FF), F32)
    else:
        last_spec = pl.BlockSpec((1, CONV_W - 1, 2 * D_FF), lambda b, t: (b * nt + t, 0, 0))
        last_shape = jax.ShapeDtypeStruct((batch * nt, CONV_W - 1, 2 * D_FF), F32)
        scratch.append(pltpu.VMEM((SUBLANES, 2 * D_FF), F32))
    scratch += [pltpu.VMEM((tm, D_MODEL), F32), pltpu.VMEM((tm, D_MODEL), BF16),
                pltpu.VMEM((tm, D_MODEL), F32)]
    return pl.pallas_call(
        functools.partial(_post_attn_kernel, single_token=single_token),
        grid=(batch, nt),
        in_specs=in_specs,
        out_specs=[tok(D_MODEL), last_spec],
        out_shape=[jax.ShapeDtypeStruct((batch * seq, D_MODEL), F32), last_shape],
        scratch_shapes=scratch,
        compiler_params=_params(("arbitrary", "arbitrary")),
        name="post_attn_single" if single_token else "post_attn",
    )(*args)


def _rope_tables(pos):
    inv = ROPE_THETA ** (-jnp.arange(ROPE_HALF, dtype=F32) * (2.0 / QK_ROPE))
    ang = pos.astype(F32)[:, None] * inv[None, :]
    cos, sin = jnp.cos(ang), jnp.sin(ang)
    n = pos.shape[0]
    ones = jnp.ones((n, QK_NOPE), F32)
    zq = jnp.zeros((n, QK_NOPE), F32)
    zp = jnp.zeros((n, HEAD_PAD - QK_NOPE - QK_ROPE), F32)
    ca = jnp.concatenate([ones, cos, cos, zp], axis=1)
    sb = jnp.concatenate([zq, -sin, sin, zp], axis=1)
    return ca, sb, cos.T, sin.T


def _prep_weights(i, attn_norm_w, w_in, q_norm_w, w_uq, q_nope_norm_w, q_rope_norm_w, kv_norm_w,
                  k_rope_norm_w, w_uk, k_nope_norm_w, w_uv, w_s, b_s, w_o, ffn_norm_w, w_ff_in,
                  conv_w, conv_b, w_ff_out, ple_norm_w, w_ple_gate, w_ple_proj, ple_post_norm_w):
    row = lambda g: g[i][None, :]
    swap = lambda a: jnp.concatenate([a[..., ROPE_HALF:], a[..., :ROPE_HALF]], axis=-1)
    win = w_in[i]
    wq = w_uq[i].reshape(Q_RANK, N_HEADS, QK_NOPE + QK_ROPE)
    nope, rope = wq[..., :QK_NOPE], wq[..., QK_NOPE:]
    pad = HEAD_PAD - QK_NOPE - QK_ROPE
    w_qt = jnp.concatenate([nope, rope, jnp.zeros((Q_RANK, N_HEADS, pad), F32)],
                           axis=-1).reshape(Q_RANK, QK_WIDTH).T
    col = lambda g: jnp.broadcast_to(g[:, None], (g.shape[0], LANES))
    kr_cols = win[:, OFF_KR:OFF_U]
    w_in_e = jnp.concatenate([win[:, :OFF_KR], win[:, OFF_U:], jnp.zeros((D_MODEL, QK_NOPE), F32),
                              kr_cols, swap(kr_cols)], axis=1)
    w_uk_e = jnp.concatenate([w_uk[i], jnp.zeros((KV_RANK, N_HEADS, HEAD_PAD - QK_NOPE), F32)],
                             axis=-1).reshape(KV_RANK, QK_WIDTH)
    gn, gr = k_nope_norm_w[i], k_rope_norm_w[i]
    ga = jnp.tile(jnp.concatenate([gn, gr, jnp.zeros((pad,), F32)]), N_HEADS)[None, :]
    gb = jnp.tile(jnp.concatenate([jnp.zeros((QK_NOPE,), F32), swap(gr), jnp.zeros((pad,), F32)]),
                  N_HEADS)[None, :]
    lane = jnp.arange(2 * HEAD_PAD)
    slot, off = lane // HEAD_PAD, lane % HEAD_PAD
    grp = jnp.where(off < QK_NOPE, 0, jnp.where(off < QK_NOPE + QK_ROPE, 1, 2))
    same = (slot[:, None] == slot[None, :]) & (grp[:, None] == grp[None, :]) & (grp[:, None] < 2)
    gm = jnp.where(same, jnp.where(grp[:, None] == 0, 1.0 / QK_NOPE, 1.0 / QK_ROPE), 0.0)
    w_ukt = w_uk[i].reshape(KV_RANK, N_HEADS * QK_NOPE).T
    eye = jnp.eye(N_HEADS, dtype=F32)
    w_uv_bd = jnp.einsum('rhd,hg->hrgd', w_uv[i], eye).reshape(N_HEADS * KV_RANK, N_HEADS * V_DIM)
    return {
        'g_attn': row(attn_norm_w),
        'w_in': w_in_e.astype(BF16),
        'g_q': row(q_norm_w), 'w_qt': w_qt.astype(BF16),
        'g_qn': col(q_nope_norm_w[i]), 'g_qr': col(q_rope_norm_w[i]),
        'g_kv': row(kv_norm_w), 'w_kvt': win[:, OFF_KV:OFF_KR].T.astype(BF16), 'g_kvt': col(kv_norm_w[i]),
        'w_krt': kr_cols.T.astype(BF16), 'g_krt': col(k_rope_norm_w[i]),
        'w_uk': w_uk_e.astype(BF16), 'gm': gm.astype(BF16), 'ga': ga, 'gb': gb,
        'w_ukt': w_ukt.astype(BF16), 'w_ukt_f32': w_ukt,
        'g_kn_row': jnp.tile(k_nope_norm_w[i], N_HEADS)[None, :],
        'w_uv_bd': w_uv_bd.astype(BF16),
        'w_s': w_s[i],
        'bias_tab': jnp.repeat(b_s[i].T, CG_DIM, axis=1),
        'ws_diag': jnp.repeat(w_s[i][:, 0, 0], CG_DIM)[None, :],
        'bias_first': jnp.repeat(b_s[i][:, 0], CG_DIM)[None, :],
        'w_o': w_o[i].astype(BF16), 'g_ffn': row(ffn_norm_w),
        'w_ff_in': w_ff_in[i].astype(BF16), 'conv_w': conv_w[i], 'conv_b': row(conv_b),
        'w_ff_out': w_ff_out[i].astype(BF16),
        'g_ple': row(ple_norm_w), 'w_gate': w_ple_gate[i].astype(BF16),
        'w_proj': w_ple_proj[i].astype(BF16), 'g_post': row(ple_post_norm_w),
    }


def kernel(x_prompt, x_sample, cache_ckv, cache_krope, state_conv, page_table, p_prompt, p_sample,
           attn_norm_w, w_in, q_norm_w, w_uq, q_nope_norm_w, q_rope_norm_w, kv_norm_w, k_rope_norm_w,
           w_uk, k_nope_norm_w, w_uv, w_s, b_s, w_o, ffn_norm_w, w_ff_in, conv_w, conv_b, w_ff_out,
           ple_norm_w, w_ple_gate, w_ple_proj, ple_post_norm_w):
    batch, seq, _ = x_prompt.shape
    n_samp, seq_s, _ = x_sample.shape
    depth = w_in.shape[0]
    assert depth == 1 and seq_s == 1 and seq % TOK_TILE == 0 and seq % CHUNK == 0
    past_len = page_table.shape[1] * PAGE_SIZE
    assert past_len % KV_TILE == 0
    w = _prep_weights(0, attn_norm_w, w_in, q_norm_w, w_uq, q_nope_norm_w, q_rope_norm_w, kv_norm_w,
                      k_rope_norm_w, w_uk, k_nope_norm_w, w_uv, w_s, b_s, w_o, ffn_norm_w, w_ff_in,
                      conv_w, conv_b, w_ff_out, ple_norm_w, w_ple_gate, w_ple_proj, ple_post_norm_w)

    n_tok = batch * seq
    tabs = _rope_tables(jnp.arange(seq, dtype=jnp.int32))
    xp = x_prompt.reshape(n_tok, D_MODEL)
    qt, k, ckv, vt, krt, u, v = _in_proj(xp, tabs, w, 2 * TOK_TILE)
    o_lat = _prompt_attn(qt, k, vt, batch, seq)
    y_prompt, a_last = _post_attn(xp, o_lat, u, v, p_prompt[0].reshape(n_tok, PLE_DIM), w,
                                  batch, seq, TOK_TILE)
    y_prompt = y_prompt.reshape(batch, seq, D_MODEL)
    n_keep = (seq - 1) % CHUNK + 1
    new_ckv_p = ckv.reshape(1, batch, seq, KV_RANK)
    new_kr_p = jnp.swapaxes(krt, 1, 2)[None]
    new_v_p = v.reshape(batch, seq, CG_WIDTH)[:, seq - n_keep:][None]
    new_conv_p = a_last.reshape(batch, seq // TOK_TILE, CONV_W - 1, 2 * D_FF)[:, -1][None]

    tabs_s = _rope_tables(jnp.full((n_samp,), past_len, dtype=jnp.int32))
    xs = x_sample.reshape(n_samp, D_MODEL)
    qt_s, k_s, ckv_s, _, krt_s, u_s, v_s = _in_proj(xs, tabs_s, w, n_samp)
    q3 = jnp.transpose(qt_s, (2, 0, 1)).astype(F32)
    kr_s = krt_s[0].T
    qa = _absorb_q(q3[:, :, :QK_NOPE].reshape(n_samp, N_HEADS * QK_NOPE), w)
    cache_rt = jnp.swapaxes(cache_krope[0], 1, 2)
    o_s = _sample_attn(page_table, q3, qa.reshape(n_samp, N_HEADS, KV_RANK),
                       k_s.reshape(n_samp, N_HEADS, HEAD_PAD), ckv_s.reshape(n_samp, 1, KV_RANK),
                       w, cache_ckv[0], cache_rt)
    o_s = o_s.reshape(n_samp, N_HEADS * KV_RANK).astype(BF16)
    hist = (state_conv[0, :, 0, :], state_conv[0, :, 1, :])
    y_sample, a_s = _post_attn(xs, o_s, u_s, v_s, p_sample[0].reshape(n_samp, PLE_DIM), w,
                               1, n_samp, n_samp, hist=hist)
    y_sample = y_sample.reshape(n_samp, 1, D_MODEL)
    new_conv_s = jnp.swapaxes(a_s, 0, 1)[None]

    return (y_prompt, y_sample, new_ckv_p, new_kr_p,
            ckv_s.reshape(1, n_samp, 1, KV_RANK), kr_s.reshape(1, n_samp, 1, QK_ROPE),
            new_v_p, v_s.reshape(1, n_samp, 1, CG_WIDTH), new_conv_p, new_conv_s)
```

```python
import functools

import jax
import jax.numpy as jnp
from jax import lax
from jax.experimental import pallas as pl
from jax.experimental.pallas import tpu as pltpu

F32 = jnp.float32
BF16 = jnp.bfloat16

D_MODEL = 1024
N_HEADS = 8
QK_NOPE = 64
QK_ROPE = 32
ROPE_HALF = QK_ROPE // 2
V_DIM = 64
Q_RANK = 256
KV_RANK = 128
CHUNK = 128
N_CG = 8
CG_WIDTH = 512
CG_DIM = 64
D_FF = 2816
CONV_W = 3
PLE_DIM = 256
PAGE_SIZE = 128
ROPE_THETA = 10000.0
EPS = 1e-6
SCALE = (QK_NOPE + QK_ROPE) ** -0.5
Q_SCALE = SCALE * 1.4426950408889634
OFF_KV = Q_RANK
OFF_KR = OFF_KV + KV_RANK
OFF_U = OFF_KR + QK_ROPE
HEAD_PAD = 128
QK_WIDTH = N_HEADS * HEAD_PAD
Z_WIDTH = Q_RANK + KV_RANK + 2 * CG_WIDTH

LANES = 128
SUBLANES = 8
VMEM_LIMIT_BYTES = 56 * 1024 * 1024

TOK_TILE = 512
ATT_BLOCK = 1024
KEY_BLOCK = 512
ONES_ROWS = 16
FF_TILE = 256
KV_TILE = 4096
ISSUE_TILES = 2

NT_DIMS = (((1,), (1,)), ((), ()))


def _params(semantics):
    return pltpu.CompilerParams(dimension_semantics=semantics,
                                vmem_limit_bytes=VMEM_LIMIT_BYTES)


def _rms(x):
    return x * lax.rsqrt(jnp.mean(x * x, axis=-1, keepdims=True) + EPS)


def _dot(a, b):
    return jnp.dot(a, b, preferred_element_type=F32)


def _dot_nt(a, b):
    return lax.dot_general(a, b, NT_DIMS, preferred_element_type=F32)


def _full(shape):
    zeros = (0,) * len(shape)
    return pl.BlockSpec(shape, lambda *_: zeros)


def _in_proj_kernel(x_ref, ca_ref, sb_ref, cost_ref, sint_ref,
                    g_attn_ref, w_in_ref, g_q_ref, w_qt_ref, g_qn_ref, g_qr_ref,
                    g_kv_ref, w_kvt_ref, g_kvt_ref, w_krt_ref, g_krt_ref, w_uk_ref, gm_ref, ga_ref, gb_ref,
                    qt_ref, k_ref, ckv_ref, vt_ref, krt_ref, u_ref, v_ref):
    tm = x_ref.shape[0]
    reps = tm // LANES
    cos, sin = cost_ref[...], sint_ref[...]
    hb = (_rms(x_ref[...]) * g_attn_ref[...]).astype(BF16)
    z = _dot(hb, w_in_ref[...])
    ckv = _rms(z[:, OFF_KV:OFF_KR]) * g_kv_ref[...]
    ckv_ref[...] = ckv
    kr_slot = z[:, Z_WIDTH:]
    ka = _dot(ckv.astype(BF16), w_uk_ref[...]) + jnp.tile(kr_slot, (1, N_HEADS))
    ct = _dot_nt(w_kvt_ref[...], hb)
    krt = _dot_nt(w_krt_ref[...], hb)
    qln = (_rms(z[:, :Q_RANK]) * g_q_ref[...]).astype(BF16)
    qt = _dot_nt(w_qt_ref[...], qln).reshape(N_HEADS, HEAD_PAD, tm)
    sq = (ka * ka).astype(BF16)
    gm = gm_ref[...]
    pair = 2 * HEAD_PAD
    ms = jnp.concatenate([_dot(sq[:, i * pair:(i + 1) * pair], gm) for i in range(N_HEADS // 2)], axis=1)
    krb = pltpu.roll(kr_slot, HEAD_PAD - QK_ROPE, axis=1)

    u_ref[...] = jax.nn.gelu(z[:, OFF_KR:OFF_KR + CG_WIDTH])
    v_ref[...] = jax.nn.gelu(z[:, OFF_KR + CG_WIDTH:OFF_KR + 2 * CG_WIDTH])
    ct = ct * lax.rsqrt(jnp.mean(ct * ct, axis=0, keepdims=True) + EPS)
    vt_ref[...] = (ct * jnp.tile(g_kvt_ref[...], (1, reps))).astype(BF16)
    krt = krt * lax.rsqrt(jnp.mean(krt * krt, axis=0, keepdims=True) + EPS) * jnp.tile(g_krt_ref[...], (1, reps))
    kx1, kx2 = krt[:ROPE_HALF], krt[ROPE_HALF:]
    krt_ref[0] = jnp.concatenate([kx1 * cos - kx2 * sin, kx1 * sin + kx2 * cos], axis=0)

    nope = qt[:, :QK_NOPE]
    rope = qt[:, QK_NOPE:QK_NOPE + QK_ROPE]
    g_qn = jnp.tile(g_qn_ref[...], (1, reps))[None]
    g_qr = jnp.tile(g_qr_ref[...], (1, reps))[None]
    nope = nope * lax.rsqrt(jnp.mean(nope * nope, axis=1, keepdims=True) + EPS) * (g_qn * Q_SCALE)
    rope = rope * lax.rsqrt(jnp.mean(rope * rope, axis=1, keepdims=True) + EPS) * (g_qr * Q_SCALE)
    x1, x2 = rope[:, :ROPE_HALF], rope[:, ROPE_HALF:]
    cos, sin = cos[None], sin[None]
    pad = jnp.zeros((N_HEADS, HEAD_PAD - QK_NOPE - QK_ROPE, tm), F32)
    qt_ref[...] = jnp.concatenate([nope, x1 * cos - x2 * sin, x1 * sin + x2 * cos, pad], axis=1).astype(BF16)

    ca = jnp.tile(ca_ref[...], (1, N_HEADS))
    sb = jnp.tile(sb_ref[...], (1, N_HEADS))
    k = lax.rsqrt(ms + EPS) * (ka * ga_ref[...] * ca + jnp.tile(krb, (1, N_HEADS)) * gb_ref[...] * sb)
    k_ref[...] = k.astype(BF16)


def _in_proj(x, tabs, w, tm):
    n = x.shape[0]
    ca, sb, cost, sint = tabs
    pos_tiles = ca.shape[0] // tm
    tok = lambda width: pl.BlockSpec((tm, width), lambda i: (i, 0))
    tok_t = lambda rows: pl.BlockSpec((rows, tm), lambda i: (0, i))
    pos = pl.BlockSpec((tm, HEAD_PAD), lambda i: (i % pos_tiles, 0))
    pos_t = pl.BlockSpec((ROPE_HALF, tm), lambda i: (0, i % pos_tiles))
    weights = [w['g_attn'], w['w_in'], w['g_q'], w['w_qt'], w['g_qn'], w['g_qr'],
               w['g_kv'], w['w_kvt'], w['g_kvt'], w['w_krt'], w['g_krt'], w['w_uk'], w['gm'], w['ga'], w['gb']]
    return pl.pallas_call(
        _in_proj_kernel,
        grid=(n // tm,),
        in_specs=[tok(D_MODEL), pos, pos, pos_t, pos_t]
                 + [_full(a.shape) for a in weights],
        out_specs=[pl.BlockSpec((N_HEADS, HEAD_PAD, tm), lambda i: (0, 0, i)),
                   tok(QK_WIDTH), tok(KV_RANK), tok_t(KV_RANK),
                   pl.BlockSpec((1, QK_ROPE, tm), lambda i: (i // pos_tiles, 0, i % pos_tiles)),
                   tok(CG_WIDTH), tok(CG_WIDTH)],
        out_shape=[jax.ShapeDtypeStruct((N_HEADS, HEAD_PAD, n), BF16),
                   jax.ShapeDtypeStruct((n, QK_WIDTH), BF16),
                   jax.ShapeDtypeStruct((n, KV_RANK), F32),
                   jax.ShapeDtypeStruct((KV_RANK, n), BF16),
                   jax.ShapeDtypeStruct((n // (pos_tiles * tm), QK_ROPE, pos_tiles * tm), F32),
                   jax.ShapeDtypeStruct((n, CG_WIDTH), F32),
                   jax.ShapeDtypeStruct((n, CG_WIDTH), F32)],
        compiler_params=_params(("parallel",)),
        name="in_proj",
    )(x, ca, sb, cost, sint, *weights)


def _prompt_attn_kernel(qt_ref, k_ref, vt_ref, o_ref, m_s, acc_s, s_s):
    qi = pl.program_id(1)
    kb = KEY_BLOCK
    per_q = ATT_BLOCK // kb
    m_s[...] = jnp.full(m_s.shape, -jnp.inf, F32)
    acc_s[...] = jnp.zeros(acc_s.shape, F32)
    ones = jnp.ones((ONES_ROWS, kb), BF16)

    def key_block(off, first_q, masked):
        cols = slice(first_q, ATT_BLOCK)
        vals = jnp.concatenate([vt_ref[:, pl.ds(off, kb)], ones], axis=0)
        if masked:
            key = lax.broadcasted_iota(jnp.int32, (kb, ATT_BLOCK - first_q), 0)
            qry = lax.broadcasted_iota(jnp.int32, (kb, ATT_BLOCK - first_q), 1)
            visible = key <= qry

        def score_stage(h):
            s = _dot(k_ref[pl.ds(off, kb), h * HEAD_PAD:(h + 1) * HEAD_PAD], qt_ref[h, :, cols])
            if masked:
                s = jnp.where(visible, s, -jnp.inf)
            m_old = m_s[h, :, cols]
            m_new = jnp.maximum(m_old, jnp.max(s, axis=0, keepdims=True))
            m_s[h, :, cols] = m_new
            s_s[h % 2, :, cols] = s
            return m_old, m_new

        def value_stage(h, m_old, m_new):
            p = jnp.exp2(s_s[h % 2, :, cols] - m_new)
            acc_s[h, :, cols] = jnp.exp2(m_old - m_new) * acc_s[h, :, cols] + _dot(vals, p.astype(BF16))

        pending = score_stage(0)
        for h in range(N_HEADS):
            nxt = score_stage(h + 1) if h + 1 < N_HEADS else None
            value_stage(h, *pending)
            pending = nxt

    def body(j, carry):
        key_block(pl.multiple_of(j * kb, kb), 0, False)
        return carry

    lax.fori_loop(0, qi * per_q, body, 0)
    for d in range(per_q):
        key_block(pl.multiple_of(qi * ATT_BLOCK + d * kb, kb), d * kb, True)
    for h in range(N_HEADS):
        acc = acc_s[h]
        o_t = acc[:KV_RANK] / acc[KV_RANK:KV_RANK + 1]
        o_ref[:, h * HEAD_PAD:(h + 1) * HEAD_PAD] = o_t.T.astype(BF16)


def _prompt_attn(qt, k, vt, batch, seq):
    nq = seq // ATT_BLOCK
    return pl.pallas_call(
        _prompt_attn_kernel,
        grid=(batch, nq),
        in_specs=[pl.BlockSpec((N_HEADS, HEAD_PAD, ATT_BLOCK), lambda b, i: (0, 0, b * nq + i)),
                  pl.BlockSpec((seq, QK_WIDTH), lambda b, i: (b, 0)),
                  pl.BlockSpec((KV_RANK, seq), lambda b, i: (0, b))],
        out_specs=pl.BlockSpec((ATT_BLOCK, N_HEADS * KV_RANK), lambda b, i: (b * nq + i, 0)),
        out_shape=jax.ShapeDtypeStruct((batch * seq, N_HEADS * KV_RANK), BF16),
        scratch_shapes=[pltpu.VMEM((N_HEADS, 1, ATT_BLOCK), F32),
                        pltpu.VMEM((N_HEADS, KV_RANK + ONES_ROWS, ATT_BLOCK), F32),
                        pltpu.VMEM((2, KEY_BLOCK, ATT_BLOCK), F32)],
        compiler_params=_params(("parallel", "parallel")),
        name="prompt_attn",
    )(qt, k, vt)


def _absorb_q_kernel(qn_ref, g_ref, wukt_ref, qa_ref):
    qg = qn_ref[...] * g_ref[...]
    lane = lax.broadcasted_iota(jnp.int32, qg.shape, 1)
    for h in range(N_HEADS):
        q_h = jnp.where((lane >= h * QK_NOPE) & (lane < (h + 1) * QK_NOPE), qg, 0.0)
        qa_ref[:, h * KV_RANK:(h + 1) * KV_RANK] = jnp.dot(
            q_h, wukt_ref[...], precision=lax.Precision.HIGHEST, preferred_element_type=F32)


def _absorb_q(qn, w):
    n_samp = qn.shape[0]
    args = [qn, w['g_kn_row'], w['w_ukt_f32']]
    return pl.pallas_call(
        _absorb_q_kernel,
        grid=(1,),
        in_specs=[_full(a.shape) for a in args],
        out_specs=_full((n_samp, N_HEADS * KV_RANK)),
        out_shape=jax.ShapeDtypeStruct((n_samp, N_HEADS * KV_RANK), F32),
        compiler_params=_params(("arbitrary",)),
        name="absorb_q",
    )(*args)


def _sample_attn_kernel(pt_ref, per_ref, wukb_ref,
                        cache_c, cache_r, o_ref,
                        wext, cbuf, rbuf, sems, *, n_pages):
    b = pl.program_id(0)
    last = pl.num_programs(0) - 1
    slot = lax.rem(b, 2)
    past = n_pages * PAGE_SIZE
    n_tiles = past // KV_TILE
    pages_per_tile = KV_TILE // PAGE_SIZE
    nw = N_HEADS * QK_NOPE
    nxt = jnp.minimum(b + 1, last)

    def start_page(bb, sl, i, prio=0):
        page = pt_ref[bb * n_pages + i]
        off = pl.multiple_of(i * PAGE_SIZE, PAGE_SIZE)
        pltpu.make_async_copy(cache_c.at[page], cbuf.at[sl, pl.ds(off, PAGE_SIZE)],
                              sems.at[0, sl]).start(priority=prio)
        pltpu.make_async_copy(cache_r.at[page], rbuf.at[sl, i], sems.at[1, sl]).start(priority=1 - prio)

    def wait_pages(sl):
        pltpu.make_async_copy(cbuf.at[sl], cbuf.at[sl], sems.at[0, sl]).wait()
        pltpu.make_async_copy(rbuf.at[sl], rbuf.at[sl], sems.at[1, sl]).wait()

    @pl.when(b == 0)
    def _():
        def issue(i, carry):
            start_page(0, 0, i)
            return carry
        lax.fori_loop(0, n_pages, issue, 0)
        wext[:nw, :] = wukb_ref[...]

    q = per_ref[0, 0 * N_HEADS:1 * N_HEADS]
    qa = per_ref[0, 1 * N_HEADS:2 * N_HEADS]
    k_own = per_ref[0, 2 * N_HEADS:3 * N_HEADS]
    c_own = per_ref[0, 3 * N_HEADS:3 * N_HEADS + 1]
    wext[nw:, :] = jnp.concatenate(
        [qa, jnp.zeros((wext.shape[0] - nw - N_HEADS, KV_RANK), F32)], axis=0).astype(BF16)
    qr = q[:, QK_NOPE:QK_NOPE + QK_ROPE].astype(BF16)

    def scores(c_blk, krt_blk):
        tk = c_blk.shape[0]
        cb = c_blk.astype(BF16)
        knt = _dot_nt(wext[...], cb)
        kn3 = knt[:nw].reshape(N_HEADS, QK_NOPE, tk)
        r = lax.rsqrt(jnp.sum(kn3 * kn3, axis=1) * (1.0 / QK_NOPE) + EPS)
        s_rope = _dot(qr, krt_blk.astype(BF16))
        return cb, knt[nw:nw + N_HEADS] * r + s_rope

    def value_part(m_j, p_j, cb_j):
        return m_j, jnp.sum(p_j, axis=-1, keepdims=True), _dot(p_j.astype(BF16), cb_j)

    wait_pages(slot)
    parts, pending = [], None
    per_group = n_pages // ISSUE_TILES

    def issue_group(g):
        for i in range(per_group):
            start_page(nxt, 1 - slot, g * per_group + i, prio=i % 2)

    for j in range(n_tiles):
        keys = slice(j * KV_TILE, (j + 1) * KV_TILE)
        krt = jnp.concatenate([rbuf[slot, j * pages_per_tile + i] for i in range(pages_per_tile)], axis=1)
        cb, s = scores(cbuf[slot, keys, :], krt)
        if j < ISSUE_TILES:
            issue_group(j)
        m_j = jnp.max(s, axis=-1, keepdims=True)
        if pending is not None:
            parts.append(value_part(*pending))
        pending = (m_j, jnp.exp2(s - m_j), cb)
    parts.append(value_part(*pending))

    s_own = jnp.sum(q * k_own, axis=-1, keepdims=True)
    m = s_own
    for m_j, _, _ in parts:
        m = jnp.maximum(m, m_j)
    p_own = jnp.exp2(s_own - m)
    round_bf16 = lambda a: a.astype(BF16).astype(F32)
    l = p_own
    acc = round_bf16(p_own) * round_bf16(c_own)
    for m_j, l_j, acc_j in parts:
        w_j = jnp.exp2(m_j - m)
        l = l + w_j * l_j
        acc = acc + w_j * acc_j
    o_ref[0] = acc / l

    @pl.when(b == last)
    def _():
        wait_pages(1 - slot)


def _sample_attn(page_table, q, qa, k_own, c_own, w, cache_c, cache_rt):
    n_samp, n_pages = page_table.shape
    past = n_pages * PAGE_SIZE
    assert past % KV_TILE == 0 and n_pages % ISSUE_TILES == 0 and past // KV_TILE >= ISSUE_TILES
    wext_rows = N_HEADS * QK_NOPE + 2 * SUBLANES
    per = jnp.concatenate([q, qa, k_own.astype(F32), jnp.broadcast_to(c_own, (n_samp, N_HEADS, KV_RANK))],
                          axis=1)
    grid_spec = pltpu.PrefetchScalarGridSpec(
        num_scalar_prefetch=1,
        grid=(n_samp,),
        in_specs=[pl.BlockSpec((1,) + per.shape[1:], lambda b, pt: (b, 0, 0)),
                  pl.BlockSpec(w['w_ukt'].shape, lambda b, pt: (0, 0)),
                  pl.BlockSpec(memory_space=pl.ANY),
                  pl.BlockSpec(memory_space=pl.ANY)],
        out_specs=pl.BlockSpec((1, N_HEADS, KV_RANK), lambda b, pt: (b, 0, 0)),
        scratch_shapes=[pltpu.VMEM((wext_rows, KV_RANK), BF16),
                        pltpu.VMEM((2, past, KV_RANK), F32),
                        pltpu.VMEM((2, n_pages, QK_ROPE, PAGE_SIZE), F32),
                        pltpu.SemaphoreType.DMA((2, 2))])
    return pl.pallas_call(
        functools.partial(_sample_attn_kernel, n_pages=n_pages),
        grid_spec=grid_spec,
        out_shape=jax.ShapeDtypeStruct((n_samp, N_HEADS, KV_RANK), F32),
        compiler_params=_params(("arbitrary",)),
        name="sample_attn",
    )(page_table.reshape(-1), per, w['w_ukt'], cache_c, cache_rt)


def _mix_proj_kernel(x_ref, o_ref, u_ref, v_ref, wuv_ref, ws_ref, bias_ref, wo_ref, g_ref,
                     h_ref, hn_ref, *, single_token):
    tm = x_ref.shape[0]
    attn = _dot(o_ref[...], wuv_ref[...])
    v = v_ref[...]
    if single_token:
        mixed = v * ws_ref[...] + bias_ref[...]
    else:
        vb = v.astype(BF16)
        row = lax.broadcasted_iota(jnp.int32, (CHUNK, CHUNK), 0)
        col = lax.broadcasted_iota(jnp.int32, (CHUNK, CHUNK), 1)
        w_tril = [jnp.where(col <= row, ws_ref[g], 0.0).astype(BF16) for g in range(N_CG)]
        w_pair = [jnp.concatenate([w_tril[2 * j], w_tril[2 * j + 1]], axis=1) for j in range(N_CG // 2)]
        low_half = lax.broadcasted_iota(jnp.int32, (CHUNK, LANES), 1) < CG_DIM
        zero = jnp.zeros((CHUNK, LANES), BF16)
        chunks = []
        for c in range(tm // CHUNK):
            cols = []
            for j in range(CG_WIDTH // LANES):
                vp = vb[c * CHUNK:(c + 1) * CHUNK, j * LANES:(j + 1) * LANES]
                stacked = jnp.concatenate([jnp.where(low_half, vp, zero), jnp.where(low_half, zero, vp)], axis=0)
                cols.append(_dot(w_pair[j], stacked))
            chunks.append(jnp.concatenate(cols, axis=1) + bias_ref[...])
        mixed = jnp.concatenate(chunks, axis=0)
    sg = u_ref[...] * mixed
    width = attn.shape[1]
    h = x_ref[...] + _dot(attn.astype(BF16), wo_ref[:width, :]) + _dot(sg.astype(BF16), wo_ref[width:, :])
    h_ref[...] = h
    hn_ref[...] = (_rms(h) * g_ref[...]).astype(BF16)


def _conv_ffn_kernel(*refs, single_token):
    if single_token:
        h_ref, hn_ref, win_ref, cw_ref, cb_ref, wout_ref, h0_ref, h1_ref, out_ref, a_ref, act_s = refs
    else:
        h_ref, hn_ref, win_ref, cw_ref, cb_ref, wout_ref, out_ref, last_ref, act_s, carry = refs
    tm = hn_ref.shape[0]
    hn = hn_ref[...]

    def conv(cols):
        a = _dot(hn, win_ref[:, cols])
        if single_token:
            a2, a1 = h0_ref[:, cols], h1_ref[:, cols]
            a_ref[0, :, cols] = a1
            a_ref[1, :, cols] = a
        else:
            ext = jnp.concatenate([carry[:, cols], a], axis=0)
            a1 = ext[SUBLANES - 1:SUBLANES - 1 + tm]
            a2 = ext[SUBLANES - 2:SUBLANES - 2 + tm]
            tail = a[tm - SUBLANES:]
            carry[:, cols] = tail
            last_ref[0, :, cols] = tail[SUBLANES - (CONV_W - 1):]
        return cb_ref[:, cols] + cw_ref[0:1, cols] * a2 + cw_ref[1:2, cols] * a1 + cw_ref[2:3, cols] * a

    for j in range(D_FF // FF_TILE):
        gate = slice(j * FF_TILE, (j + 1) * FF_TILE)
        up = slice(D_FF + j * FF_TILE, D_FF + (j + 1) * FF_TILE)
        act_s[:, gate] = (jax.nn.silu(conv(gate)) * conv(up)).astype(BF16)
    out_ref[...] = h_ref[...] + _dot(act_s[...], wout_ref[...])


def _ple_kernel(h_ref, p_ref, g_ref, wgate_ref, wproj_ref, gpost_ref, y_ref):
    h = h_ref[...]
    gate = jax.nn.sigmoid(_dot((_rms(h) * g_ref[...]).astype(BF16), wgate_ref[...]))
    e = _rms(_dot(p_ref[...].astype(BF16), wproj_ref[...])) * gpost_ref[...]
    y_ref[...] = h + gate * e


def _post_attn_kernel(*refs, single_token):
    x_ref, o_ref, u_ref, v_ref, p_ref = refs[:5]
    mix_w, ffn_w, ple_w = refs[5:10], refs[10:14], refs[14:18]
    if single_token:
        h0_ref, h1_ref, y_ref, a_ref, act_s, h_s, hn_s, h2_s = refs[18:]
        _mix_proj_kernel(x_ref, o_ref, u_ref, v_ref, *mix_w, h_s, hn_s, single_token=True)
        _conv_ffn_kernel(h_s, hn_s, *ffn_w, h0_ref, h1_ref, h2_s, a_ref, act_s, single_token=True)
        _ple_kernel(h2_s, p_ref, *ple_w, y_ref)
        return
    y_ref, last_ref, act_s, carry, h_s, hn_s, h2_s = refs[18:]

    @pl.when(pl.program_id(1) == 0)
    def _():
        carry[...] = jnp.zeros(carry.shape, F32)

    _mix_proj_kernel(x_ref, o_ref, u_ref, v_ref, *mix_w, h_s, hn_s, single_token=False)
    _conv_ffn_kernel(h_s, hn_s, *ffn_w, h2_s, last_ref, act_s, carry, single_token=False)
    _ple_kernel(h2_s, p_ref, *ple_w, y_ref)


def _post_attn(x, o_lat, u, v, p, w, batch, seq, tm, hist=None):
    single_token = hist is not None
    nt = seq // tm
    tok = lambda width: pl.BlockSpec((tm, width), lambda b, t: (b * nt + t, 0))
    resident = lambda a: pl.BlockSpec(a.shape, lambda b, t: (0,) * a.ndim, pipeline_mode=pl.Buffered(1))
    ws, bias = (w['ws_diag'], w['bias_first']) if single_token else (w['w_s'], w['bias_tab'])
    weights = [w['w_uv_bd'], ws, bias, w['w_o'], w['g_ffn'],
               w['w_ff_in'], w['conv_w'], w['conv_b'], w['w_ff_out'],
               w['g_ple'], w['w_gate'], w['w_proj'], w['g_post']]
    in_specs = ([tok(D_MODEL), tok(N_HEADS * KV_RANK), tok(CG_WIDTH), tok(CG_WIDTH), tok(PLE_DIM)]
                + [resident(a) for a in weights])
    args = [x, o_lat, u, v, p] + weights
    scratch = [pltpu.VMEM((tm, D_FF), BF16)]
    if single_token:
        in_specs += [resident(hist[0]), resident(hist[1])]
        args += list(hist)
        last_spec = pl.BlockSpec((CONV_W - 1, tm, 2 * D_FF), lambda b, t: (0, 0, 0))
        last_shape = jax.ShapeDtypeStruct((CONV_W - 1, tm, 2 * D_FF), F32)
    else:
        last_spec = pl.BlockSpec((1, CONV_W - 1, 2 * D_FF), lambda b, t: (b * nt + t, 0, 0))
        last_shape = jax.ShapeDtypeStruct((batch * nt, CONV_W - 1, 2 * D_FF), F32)
        scratch.append(pltpu.VMEM((SUBLANES, 2 * D_FF), F32))
    scratch += [pltpu.VMEM((tm, D_MODEL), F32), pltpu.VMEM((tm, D_MODEL), BF16),
                pltpu.VMEM((tm, D_MODEL), F32)]
    return pl.pallas_call(
        functools.partial(_post_attn_kernel, single_token=single_token),
        grid=(batch, nt),
        in_specs=in_specs,
        out_specs=[tok(D_MODEL), last_spec],
        out_shape=[jax.ShapeDtypeStruct((batch * seq, D_MODEL), F32), last_shape],
        scratch_shapes=scratch,
        compiler_params=_params(("arbitrary", "arbitrary")),
        name="post_attn_single" if single_token else "post_attn",
    )(*args)


def _rope_tables(pos):
    inv = ROPE_THETA ** (-jnp.arange(ROPE_HALF, dtype=F32) * (2.0 / QK_ROPE))
    ang = pos.astype(F32)[:, None] * inv[None, :]
    cos, sin = jnp.cos(ang), jnp.sin(ang)
    n = pos.shape[0]
    ones = jnp.ones((n, QK_NOPE), F32)
    zq = jnp.zeros((n, QK_NOPE), F32)
    zp = jnp.zeros((n, HEAD_PAD - QK_NOPE - QK_ROPE), F32)
    ca = jnp.concatenate([ones, cos, cos, zp], axis=1)
    sb = jnp.concatenate([zq, -sin, sin, zp], axis=1)
    return ca, sb, cos.T, sin.T


def _prep_weights(i, attn_norm_w, w_in, q_norm_w, w_uq, q_nope_norm_w, q_rope_norm_w, kv_norm_w,
                  k_rope_norm_w, w_uk, k_nope_norm_w, w_uv, w_s, b_s, w_o, ffn_norm_w, w_ff_in,
                  conv_w, conv_b, w_ff_out, ple_norm_w, w_ple_gate, w_ple_proj, ple_post_norm_w):
    row = lambda g: g[i][None, :]
    swap = lambda a: jnp.concatenate([a[..., ROPE_HALF:], a[..., :ROPE_HALF]], axis=-1)
    win = w_in[i]
    wq = w_uq[i].reshape(Q_RANK, N_HEADS, QK_NOPE + QK_ROPE)
    nope, rope = wq[..., :QK_NOPE], wq[..., QK_NOPE:]
    pad = HEAD_PAD - QK_NOPE - QK_ROPE
    w_qt = jnp.concatenate([nope, rope, jnp.zeros((Q_RANK, N_HEADS, pad), F32)],
                           axis=-1).reshape(Q_RANK, QK_WIDTH).T
    col = lambda g: jnp.broadcast_to(g[:, None], (g.shape[0], LANES))
    kr_cols = win[:, OFF_KR:OFF_U]
    w_in_e = jnp.concatenate([win[:, :OFF_KR], win[:, OFF_U:], jnp.zeros((D_MODEL, QK_NOPE), F32),
                              kr_cols, swap(kr_cols)], axis=1)
    w_uk_e = jnp.concatenate([w_uk[i], jnp.zeros((KV_RANK, N_HEADS, HEAD_PAD - QK_NOPE), F32)],
                             axis=-1).reshape(KV_RANK, QK_WIDTH)
    gn, gr = k_nope_norm_w[i], k_rope_norm_w[i]
    ga = jnp.tile(jnp.concatenate([gn, gr, jnp.zeros((pad,), F32)]), N_HEADS)[None, :]
    gb = jnp.tile(jnp.concatenate([jnp.zeros((QK_NOPE,), F32), swap(gr), jnp.zeros((pad,), F32)]),
                  N_HEADS)[None, :]
    lane = jnp.arange(2 * HEAD_PAD)
    slot, off = lane // HEAD_PAD, lane % HEAD_PAD
    grp = jnp.where(off < QK_NOPE, 0, jnp.where(off < QK_NOPE + QK_ROPE, 1, 2))
    same = (slot[:, None] == slot[None, :]) & (grp[:, None] == grp[None, :]) & (grp[:, None] < 2)
    gm = jnp.where(same, jnp.where(grp[:, None] == 0, 1.0 / QK_NOPE, 1.0 / QK_ROPE), 0.0)
    w_ukt = w_uk[i].reshape(KV_RANK, N_HEADS * QK_NOPE).T
    eye = jnp.eye(N_HEADS, dtype=F32)
    w_uv_bd = jnp.einsum('rhd,hg->hrgd', w_uv[i], eye).reshape(N_HEADS * KV_RANK, N_HEADS * V_DIM)
    return {
        'g_attn': row(attn_norm_w),
        'w_in': w_in_e.astype(BF16),
        'g_q': row(q_norm_w), 'w_qt': w_qt.astype(BF16),
        'g_qn': col(q_nope_norm_w[i]), 'g_qr': col(q_rope_norm_w[i]),
        'g_kv': row(kv_norm_w), 'w_kvt': win[:, OFF_KV:OFF_KR].T.astype(BF16), 'g_kvt': col(kv_norm_w[i]),
        'w_krt': kr_cols.T.astype(BF16), 'g_krt': col(k_rope_norm_w[i]),
        'w_uk': w_uk_e.astype(BF16), 'gm': gm.astype(BF16), 'ga': ga, 'gb': gb,
        'w_ukt': w_ukt.astype(BF16), 'w_ukt_f32': w_ukt,
        'g_kn_row': jnp.tile(k_nope_norm_w[i], N_HEADS)[None, :],
        'w_uv_bd': w_uv_bd.astype(BF16),
        'w_s': w_s[i],
        'bias_tab': jnp.repeat(b_s[i].T, CG_DIM, axis=1),
        'ws_diag': jnp.repeat(w_s[i][:, 0, 0], CG_DIM)[None, :],
        'bias_first': jnp.repeat(b_s[i][:, 0], CG_DIM)[None, :],
        'w_o': w_o[i].astype(BF16), 'g_ffn': row(ffn_norm_w),
        'w_ff_in': w_ff_in[i].astype(BF16), 'conv_w': conv_w[i], 'conv_b': row(conv_b),
        'w_ff_out': w_ff_out[i].astype(BF16),
        'g_ple': row(ple_norm_w), 'w_gate': w_ple_gate[i].astype(BF16),
        'w_proj': w_ple_proj[i].astype(BF16), 'g_post': row(ple_post_norm_w),
    }


def kernel(x_prompt, x_sample, cache_ckv, cache_krope, state_conv, page_table, p_prompt, p_sample,
           attn_norm_w, w_in, q_norm_w, w_uq, q_nope_norm_w, q_rope_norm_w, kv_norm_w, k_rope_norm_w,
           w_uk, k_nope_norm_w, w_uv, w_s, b_s, w_o, ffn_norm_w, w_ff_in, conv_w, conv_b, w_ff_out,
           ple_norm_w, w_ple_gate, w_ple_proj, ple_post_norm_w):
    batch, seq, _ = x_prompt.shape
    n_samp, seq_s, _ = x_sample.shape
    depth = w_in.shape[0]
    assert depth == 1 and seq_s == 1 and seq % TOK_TILE == 0 and seq % CHUNK == 0
    past_len = page_table.shape[1] * PAGE_SIZE
    assert past_len % KV_TILE == 0
    w = _prep_weights(0, attn_norm_w, w_in, q_norm_w, w_uq, q_nope_norm_w, q_rope_norm_w, kv_norm_w,
                      k_rope_norm_w, w_uk, k_nope_norm_w, w_uv, w_s, b_s, w_o, ffn_norm_w, w_ff_in,
                      conv_w, conv_b, w_ff_out, ple_norm_w, w_ple_gate, w_ple_proj, ple_post_norm_w)

    n_tok = batch * seq
    tabs = _rope_tables(jnp.arange(seq, dtype=jnp.int32))
    xp = x_prompt.reshape(n_tok, D_MODEL)
    qt, k, ckv, vt, krt, u, v = _in_proj(xp, tabs, w, 2 * TOK_TILE)
    o_lat = _prompt_attn(qt, k, vt, batch, seq)
    y_prompt, a_last = _post_attn(xp, o_lat, u, v, p_prompt[0].reshape(n_tok, PLE_DIM), w,
                                  batch, seq, TOK_TILE)
    y_prompt = y_prompt.reshape(batch, seq, D_MODEL)
    n_keep = (seq - 1) % CHUNK + 1
    new_ckv_p = ckv.reshape(1, batch, seq, KV_RANK)
    new_kr_p = jnp.swapaxes(krt, 1, 2)[None]
    new_v_p = v.reshape(batch, seq, CG_WIDTH)[:, seq - n_keep:][None]
    new_conv_p = a_last.reshape(batch, seq // TOK_TILE, CONV_W - 1, 2 * D_FF)[:, -1][None]

    tabs_s = _rope_tables(jnp.full((n_samp,), past_len, dtype=jnp.int32))
    xs = x_sample.reshape(n_samp, D_MODEL)
    qt_s, k_s, ckv_s, _, krt_s, u_s, v_s = _in_proj(xs, tabs_s, w, n_samp)
    q3 = jnp.transpose(qt_s, (2, 0, 1)).astype(F32)
    kr_s = krt_s[0].T
    qa = _absorb_q(q3[:, :, :QK_NOPE].reshape(n_samp, N_HEADS * QK_NOPE), w)
    cache_rt = jnp.swapaxes(cache_krope[0], 1, 2)
    o_s = _sample_attn(page_table, q3, qa.reshape(n_samp, N_HEADS, KV_RANK),
                       k_s.reshape(n_samp, N_HEADS, HEAD_PAD), ckv_s.reshape(n_samp, 1, KV_RANK),
                       w, cache_ckv[0], cache_rt)
    o_s = o_s.reshape(n_samp, N_HEADS * KV_RANK).astype(BF16)
    hist = (state_conv[0, :, 0, :], state_conv[0, :, 1, :])
    y_sample, a_s = _post_attn(xs, o_s, u_s, v_s, p_sample[0].reshape(n_samp, PLE_DIM), w,
                               1, n_samp, n_samp, hist=hist)
    y_sample = y_sample.reshape(n_samp, 1, D_MODEL)
    new_conv_s = jnp.swapaxes(a_s, 0, 1)[None]

    return (y_prompt, y_sample, new_ckv_p, new_kr_p,
            ckv_s.reshape(1, n_samp, 1, KV_RANK), kr_s.reshape(1, n_samp, 1, QK_ROPE),
            new_v_p, v_s.reshape(1, n_samp, 1, CG_WIDTH), new_conv_p, new_conv_s)
```
